```python
import math
import jax
import jax.numpy as jnp
from jax import lax
import numpy as np

D_MODEL = 1024
BATCH = 32
SEQ = 2048
DEPTH = 2
DEC_BATCH = 32
DEC_SEQ = 32
PAST_LEN = 1024

CHUNK = 64
WINDOW = 128
BAND_PREV = WINDOW // CHUNK
BAND = (BAND_PREV + 1) * CHUNK
HEAD_DIM = 64
N_Q_HEADS = 8
N_KV_HEADS = 2
GQ = N_Q_HEADS // N_KV_HEADS
N_BUCKETS = 32
MAX_DISTANCE = 128
D_SSM = 512
CH_PER_GROUP = 16
N_SSM_GROUPS = D_SSM // CH_PER_GROUP
STATE_DIM = 64
N_EXPERTS = 256
TOP_K = 8
N_ROUTE_GROUPS = 8
TOPK_GROUPS = 4
EXPERTS_PER_GROUP = N_EXPERTS // N_ROUTE_GROUPS
D_EXPERT = 256
ROUTED_SCALE = 2.5
EXPERT_BLOCK = 128
BLOCKS_PER_STEP = 16
ALPHA = (2 * DEPTH) ** 0.25
BETA = (8 * DEPTH) ** -0.25
LN_EPS = 1e-5
NEG_INF = -1e30
Q_W = N_Q_HEADS * HEAD_DIM
KV_W = N_KV_HEADS * HEAD_DIM
IN_COLS = Q_W + 2 * KV_W + D_SSM + 2 * D_MODEL

kernel_name = 'hybrid_stream_swa_s5_moe_step'


def _layernorm(x, g, b):
    xf = x.astype(jnp.float32)
    mu = jnp.mean(xf, axis=-1, keepdims=True)
    xc = xf - mu
    var = jnp.mean(xc * xc, axis=-1, keepdims=True)
    y = xc * lax.rsqrt(var + LN_EPS) * g.astype(jnp.float32) + b.astype(jnp.float32)
    return y.astype(x.dtype)


def _in_proj(x, w_in):
    B, T, _ = x.shape
    proj = x @ w_in
    q, k, v, u, gl = jnp.split(proj, [Q_W, Q_W + KV_W, Q_W + 2 * KV_W, Q_W + 2 * KV_W + D_SSM], axis=-1)
    q = q.reshape(B, T, N_KV_HEADS, GQ, HEAD_DIM)
    k = k.reshape(B, T, N_KV_HEADS, HEAD_DIM)
    v = v.reshape(B, T, N_KV_HEADS, HEAD_DIM)
    return q, k, v, u, gl


def _rel_bias(rel, table):
    nb = N_BUCKETS // 2
    max_exact = nb // 2
    n = jnp.abs(rel)
    nf = jnp.maximum(n, 1).astype(jnp.float32)
    large = max_exact + (jnp.log(nf / max_exact) / math.log(MAX_DISTANCE / max_exact) * (nb - max_exact)).astype(jnp.int32)
    large = jnp.minimum(large, nb - 1)
    bucket = jnp.where(rel > 0, nb, 0) + jnp.where(n < max_exact, n, large)
    tq, tk = rel.shape
    bias = table[bucket].astype(jnp.float32)
    return bias.reshape(tq, tk, N_KV_HEADS, GQ).transpose(2, 3, 0, 1)


def _sink_attend(q, k, v, bias, mask, sinks):
    s = jnp.einsum('...qhgd,...khd->...hgqk', q, k).astype(jnp.float32) * (HEAD_DIM ** -0.5) + bias
    s = jnp.where(mask, s, NEG_INF)
    sink = sinks.astype(jnp.float32).reshape(N_KV_HEADS, GQ, 1, 1)
    m = jnp.maximum(jnp.max(s, axis=-1, keepdims=True), sink)
    p = jnp.exp(s - m)
    p = p / (jnp.sum(p, axis=-1, keepdims=True) + jnp.exp(sink - m))
    return jnp.einsum('...hgqk,...khd->...qhgd', p.astype(v.dtype), v)


def _prompt_attention(q, k, v, sinks, rel_table):
    B, S = q.shape[:2]
    nc = S // CHUNK
    qc = q.reshape(B, nc, CHUNK, N_KV_HEADS, GQ, HEAD_DIM)

    def band(t):
        tc = t.reshape(B, nc, CHUNK, N_KV_HEADS, HEAD_DIM)
        tp = jnp.pad(tc, ((0, 0), (BAND_PREV, 0), (0, 0), (0, 0), (0, 0)))
        return jnp.concatenate([tp[:, i:i + nc] for i in range(BAND_PREV + 1)], axis=2)

    kb, vb = band(k), band(v)
    rel = (jnp.arange(BAND) - BAND_PREV * CHUNK)[None, :] - jnp.arange(CHUNK)[:, None]
    bias = _rel_bias(rel, rel_table)
    key_chunk = jnp.arange(nc)[:, None] - BAND_PREV + (jnp.arange(BAND) // CHUNK)[None, :]
    mask = (key_chunk >= 0)[None, :, None, None, None, :]
    out = _sink_attend(qc, kb, vb, bias, mask, sinks)
    return out.reshape(B, S, Q_W)


def _sample_attention(q, k, v, cache_k, cache_v, sinks, rel_table):
    B, T = q.shape[:2]
    W = cache_k.shape[1]
    k_all = jnp.concatenate([cache_k.astype(k.dtype), k], axis=1)
    v_all = jnp.concatenate([cache_v.astype(v.dtype), v], axis=1)
    qpos = PAST_LEN + jnp.arange(T)
    kpos = jnp.concatenate([PAST_LEN - W + jnp.arange(W), qpos])
    rel = kpos[None, :] - qpos[:, None]
    qch = (qpos // CHUNK)[:, None]
    kch = (kpos // CHUNK)[None, :]
    mask = (kch >= qch - BAND_PREV) & (kch <= qch)
    out = _sink_attend(q, k_all, v_all, _rel_bias(rel, rel_table), mask, sinks)
    return out.reshape(B, T, Q_W), k_all[:, -W:], v_all[:, -W:]


def _discretize(lam_re, lam_im, log_dt, b_re, b_im):
    lr = lam_re.astype(jnp.float32)
    li = lam_im.astype(jnp.float32)
    dt = jnp.exp(log_dt.astype(jnp.float32))[:, None]
    mag = jnp.exp(lr * dt)
    a_re = mag * jnp.cos(li * dt)
    a_im = mag * jnp.sin(li * dt)
    den = lr * lr + li * li
    f_re = (((a_re - 1.0) * lr + a_im * li) / den)[..., None]
    f_im = ((a_im * lr - (a_re - 1.0) * li) / den)[..., None]
    br = b_re.astype(jnp.float32)
    bi = b_im.astype(jnp.float32)
    return a_re, a_im, f_re * br - f_im * bi, f_re * bi + f_im * br


def _complex_affine_combine(e1, e2):
    a1r, a1i, b1r, b1i = e1
    a2r, a2i, b2r, b2i = e2
    return (a2r * a1r - a2i * a1i, a2r * a1i + a2i * a1r,
            a2r * b1r - a2i * b1i + b2r, a2r * b1i + a2i * b1r + b2i)


def _ssm_block(h_re, h_im, u, disc, c_re, c_im):
    a_re, a_im, bb_re, bb_im = disc
    B, L = u.shape[:2]
    bu_re = jnp.einsum('blgc,gpc->blgp', u, bb_re)
    bu_im = jnp.einsum('blgc,gpc->blgp', u, bb_im)
    bu_re = bu_re.at[:, 0].add(a_re * h_re - a_im * h_im)
    bu_im = bu_im.at[:, 0].add(a_re * h_im + a_im * h_re)
    ar = jnp.broadcast_to(a_re, bu_re.shape)
    ai = jnp.broadcast_to(a_im, bu_re.shape)
    _, _, hr, hi = lax.associative_scan(_complex_affine_combine, (ar, ai, bu_re, bu_im), axis=1)
    y = (jnp.einsum('blgp,gcp->blgc', hr, c_re.astype(jnp.float32))
         - jnp.einsum('blgp,gcp->blgc', hi, c_im.astype(jnp.float32)))
    return y.reshape(B, L, D_SSM), hr[:, -1], hi[:, -1]


def _ssm_prompt(u, disc, c_re, c_im):
    B, S, _ = u.shape
    nb = S // CHUNK
    ub = u.astype(jnp.float32).reshape(B, nb, CHUNK, N_SSM_GROUPS, CH_PER_GROUP).transpose(1, 0, 2, 3, 4)
    h0 = jnp.zeros((B, N_SSM_GROUPS, STATE_DIM), jnp.float32)

    def step(carry, ublk):
        y, hr, hi = _ssm_block(carry[0], carry[1], ublk, disc, c_re, c_im)
        return (hr, hi), y

    (hr, hi), y = lax.scan(step, (h0, h0), ub)
    return y.transpose(1, 0, 2, 3).reshape(B, S, D_SSM), hr, hi


def _mix_and_norm(x, attn, ssm_y, u, gl, d_skip, w_attn_proj, w_glu, w_out, ln_g, ln_b):
    br_a = attn @ w_attn_proj
    z = jax.nn.gelu(ssm_y.astype(x.dtype) + d_skip * u)
    val, gate = jnp.split(z @ w_glu, 2, axis=-1)
    br_b = val * jax.nn.sigmoid(gate)
    g_a, g_b = jnp.split(jax.nn.sigmoid(gl), 2, axis=-1)
    mix = (g_a * br_a + g_b * br_b) @ w_out
    return _layernorm(ALPHA * x + mix, ln_g, ln_b)


def _route(x, w_router, b_router):
    n = x.shape[0]
    scores = jax.nn.sigmoid((x @ w_router).astype(jnp.float32))
    biased = scores + b_router.astype(jnp.float32)
    grp = biased.reshape(n, N_ROUTE_GROUPS, EXPERTS_PER_GROUP)
    grp_score = jnp.sum(lax.top_k(grp, 2)[0], axis=-1)
    _, gidx = lax.top_k(grp_score, TOPK_GROUPS)
    keep = jnp.any(gidx[:, :, None] == jnp.arange(N_ROUTE_GROUPS)[None, None, :], axis=1)
    keep = jnp.repeat(keep, EXPERTS_PER_GROUP, axis=1)
    _, idx = lax.top_k(jnp.where(keep, biased, NEG_INF), TOP_K)
    wts = jnp.take_along_axis(scores, idx, axis=1)
    wts = wts / jnp.sum(wts, axis=-1, keepdims=True) * ROUTED_SCALE
    return idx, wts


def _routed_experts(x, idx, wts, w_gate, w_up, w_down):
    n, d = x.shape
    a = n * TOP_K
    e_flat = idx.reshape(-1)
    order = jnp.argsort(e_flat)
    e_sorted = e_flat[order]
    tok_sorted = (order // TOP_K).astype(jnp.int32)
    w_sorted = wts.reshape(-1)[order]
    counts = jnp.bincount(e_flat, length=N_EXPERTS)
    padded = (counts + EXPERT_BLOCK - 1) // EXPERT_BLOCK * EXPERT_BLOCK
    start = jnp.cumsum(counts) - counts
    pend = jnp.cumsum(padded)
    pstart = pend - padded
    dest = pstart[e_sorted] + jnp.arange(a) - start[e_sorted]
    n_blocks = -(-(a + N_EXPERTS * (EXPERT_BLOCK - 1)) // EXPERT_BLOCK)
    n_steps = -(-n_blocks // BLOCKS_PER_STEP)
    n_slots = n_steps * BLOCKS_PER_STEP * EXPERT_BLOCK
    slot_tok = jnp.full((n_slots,), n, jnp.int32).at[dest].set(tok_sorted)
    slot_w = jnp.zeros((n_slots,), jnp.float32).at[dest].set(w_sorted)
    block_start = jnp.arange(n_steps * BLOCKS_PER_STEP) * EXPERT_BLOCK
    block_exp = jnp.minimum(jnp.searchsorted(pend, block_start, side='right'), N_EXPERTS - 1)
    x_pad = jnp.concatenate([x, jnp.zeros((1, d), x.dtype)], axis=0)

    def step(y, inp):
        toks, ws, exps = inp
        xb = x_pad[toks]
        h = (jax.nn.silu(jnp.einsum('nbd,ndf->nbf', xb, w_gate[exps]))
             * jnp.einsum('nbd,ndf->nbf', xb, w_up[exps]))
        out = jnp.einsum('nbf,nfd->nbd', h, w_down[exps]) * ws[..., None].astype(xb.dtype)
        y = y.at[toks.reshape(-1)].add(out.reshape(-1, d).astype(y.dtype))
        return y, None

    y, _ = lax.scan(step, jnp.zeros((n + 1, d), x.dtype),
                    (slot_tok.reshape(n_steps, BLOCKS_PER_STEP, EXPERT_BLOCK),
                     slot_w.reshape(n_steps, BLOCKS_PER_STEP, EXPERT_BLOCK),
                     block_exp.reshape(n_steps, BLOCKS_PER_STEP)))
    return y[:n]


def _moe_and_norm(x, w_router, b_router, w_gate, w_up, w_down, ws_gate, ws_up, ws_down, ln_g, ln_b):
    B, T, d = x.shape
    xf = x.reshape(B * T, d)
    idx, wts = _route(xf, w_router, b_router)
    routed = _routed_experts(xf, idx, wts, w_gate, w_up, w_down)
    shared = (jax.nn.silu(xf @ ws_gate) * (xf @ ws_up)) @ ws_down
    return _layernorm(ALPHA * x + (routed + shared).reshape(B, T, d), ln_g, ln_b)


def setup_inputs(seed: int = 0) -> dict:
    key = jax.random.key(seed)
    ks = jax.random.split(key, 32)
    f32 = jnp.float32
    win_rows = min(WINDOW, PAST_LEN)

    def nrm(k, shape, scale):
        return jax.random.normal(k, shape, f32) * scale

    n_idx = jnp.arange(STATE_DIM, dtype=f32)
    ssm_shape = (DEPTH, N_SSM_GROUPS, STATE_DIM)
    return {
        'x_prompt': nrm(ks[0], (BATCH, SEQ, D_MODEL), 1.0),
        'x_sample': nrm(ks[1], (DEC_BATCH, DEC_SEQ, D_MODEL), 1.0),
        'cache_k': nrm(ks[2], (DEPTH, DEC_BATCH, win_rows, N_KV_HEADS, HEAD_DIM), 1.0),
        'cache_v': nrm(ks[3], (DEPTH, DEC_BATCH, win_rows, N_KV_HEADS, HEAD_DIM), 1.0),
        'state_ssm_re': nrm(ks[4], (DEPTH, DEC_BATCH, N_SSM_GROUPS, STATE_DIM), 0.1),
        'state_ssm_im': nrm(ks[5], (DEPTH, DEC_BATCH, N_SSM_GROUPS, STATE_DIM), 0.1),
        'rel_bias': nrm(ks[6], (N_BUCKETS, N_Q_HEADS), 0.1),
        'w_in': nrm(ks[7], (DEPTH, D_MODEL, IN_COLS), D_MODEL ** -0.5),
        'attn_sinks': nrm(ks[8], (DEPTH, N_Q_HEADS), 1.0),
        'w_attn_proj': nrm(ks[9], (DEPTH, Q_W, D_MODEL), Q_W ** -0.5),
        'lam_re': -0.5 + nrm(ks[10], ssm_shape, 0.01),
        'lam_im': math.pi * n_idx + nrm(ks[11], ssm_shape, 0.01),
        'log_dt': jax.random.uniform(ks[12], (DEPTH, N_SSM_GROUPS), f32, math.log(1e-3), math.log(1e-1)),
        'b_re': nrm(ks[13], (DEPTH, N_SSM_GROUPS, STATE_DIM, CH_PER_GROUP), (2 * CH_PER_GROUP) ** -0.5),
        'b_im': nrm(ks[14], (DEPTH, N_SSM_GROUPS, STATE_DIM, CH_PER_GROUP), (2 * CH_PER_GROUP) ** -0.5),
        'c_re': nrm(ks[15], (DEPTH, N_SSM_GROUPS, CH_PER_GROUP, STATE_DIM), STATE_DIM ** -0.5),
        'c_im': nrm(ks[16], (DEPTH, N_SSM_GROUPS, CH_PER_GROUP, STATE_DIM), STATE_DIM ** -0.5),
        'd_skip': nrm(ks[17], (DEPTH, D_SSM), 1.0),
        'w_glu': nrm(ks[18], (DEPTH, D_SSM, 2 * D_MODEL), D_SSM ** -0.5),
        'w_out': nrm(ks[19], (DEPTH, D_MODEL, D_MODEL), BETA * D_MODEL ** -0.5),
        'ln1_g': 1.0 + nrm(ks[20], (DEPTH, D_MODEL), 0.01),
        'ln1_b': nrm(ks[21], (DEPTH, D_MODEL), 0.01),
        'w_router': nrm(ks[22], (DEPTH, D_MODEL, N_EXPERTS), D_MODEL ** -0.5),
        'b_router': nrm(ks[23], (DEPTH, N_EXPERTS), 0.01),
        'w_gate': nrm(ks[24], (DEPTH, N_EXPERTS, D_MODEL, D_EXPERT), D_MODEL ** -0.5),
        'w_up': nrm(ks[25], (DEPTH, N_EXPERTS, D_MODEL, D_EXPERT), D_MODEL ** -0.5),
        'w_down': nrm(ks[26], (DEPTH, N_EXPERTS, D_EXPERT, D_MODEL), BETA * D_EXPERT ** -0.5),
        'ws_gate': nrm(ks[27], (DEPTH, D_MODEL, D_EXPERT), D_MODEL ** -0.5),
        'ws_up': nrm(ks[28], (DEPTH, D_MODEL, D_EXPERT), D_MODEL ** -0.5),
        'ws_down': nrm(ks[29], (DEPTH, D_EXPERT, D_MODEL), BETA * D_EXPERT ** -0.5),
        'ln2_g': 1.0 + nrm(ks[30], (DEPTH, D_MODEL), 0.01),
        'ln2_b': nrm(ks[31], (DEPTH, D_MODEL), 0.01),
    }


def reference(x_prompt, x_sample, cache_k, cache_v, state_ssm_re, state_ssm_im, rel_bias, w_in,
              attn_sinks, w_attn_proj, lam_re, lam_im, log_dt, b_re, b_im, c_re, c_im, d_skip, w_glu,
              w_out, ln1_g, ln1_b, w_router, b_router, w_gate, w_up, w_down, ws_gate, ws_up, ws_down,
              ln2_g, ln2_b):
    W = cache_k.shape[2]
    xp = x_prompt
    xs = x_sample
    k_p, v_p, hr_p, hi_p = [], [], [], []
    k_s, v_s, hr_s, hi_s = [], [], [], []
    for l in range(DEPTH):
        disc = _discretize(lam_re[l], lam_im[l], log_dt[l], b_re[l], b_im[l])
        q, k, v, u, gl = _in_proj(xp, w_in[l])
        attn = _prompt_attention(q, k, v, attn_sinks[l], rel_bias)
        ssm_y, hr, hi = _ssm_prompt(u, disc, c_re[l], c_im[l])
        xp = _mix_and_norm(xp, attn, ssm_y, u, gl, d_skip[l], w_attn_proj[l], w_glu[l], w_out[l], ln1_g[l], ln1_b[l])
        xp = _moe_and_norm(xp, w_router[l], b_router[l], w_gate[l], w_up[l], w_down[l],
                           ws_gate[l], ws_up[l], ws_down[l], ln2_g[l], ln2_b[l])
        k_p.append(k[:, -W:])
        v_p.append(v[:, -W:])
        hr_p.append(hr)
        hi_p.append(hi)
        Bs, Ts = xs.shape[:2]
        q, k, v, u, gl = _in_proj(xs, w_in[l])
        attn, k_new, v_new = _sample_attention(q, k, v, cache_k[l], cache_v[l], attn_sinks[l], rel_bias)
        u_g = u.astype(jnp.float32).reshape(Bs, Ts, N_SSM_GROUPS, CH_PER_GROUP)
        ssm_y, hr, hi = _ssm_block(state_ssm_re[l].astype(jnp.float32), state_ssm_im[l].astype(jnp.float32),
                                   u_g, disc, c_re[l], c_im[l])
        xs = _mix_and_norm(xs, attn, ssm_y, u, gl, d_skip[l], w_attn_proj[l], w_glu[l], w_out[l], ln1_g[l], ln1_b[l])
        xs = _moe_and_norm(xs, w_router[l], b_router[l], w_gate[l], w_up[l], w_down[l],
                           ws_gate[l], ws_up[l], ws_down[l], ln2_g[l], ln2_b[l])
        k_s.append(k_new)
        v_s.append(v_new)
        hr_s.append(hr)
        hi_s.append(hi)
    return (xp, xs, jnp.stack(k_p), jnp.stack(v_p), jnp.stack(hr_p), jnp.stack(hi_p),
            jnp.stack(k_s), jnp.stack(v_s), jnp.stack(hr_s), jnp.stack(hi_s))
```

```python
import functools
import math

import jax
import jax.numpy as jnp
import numpy as np
from jax import lax
from jax.experimental import pallas as pl
from jax.experimental.pallas import tpu as pltpu

CHUNK = 64
WINDOW = 128
HEAD_DIM = 64
N_KV_HEADS = 2
MAX_DISTANCE = 128
CH_PER_GROUP = 16
STATE_DIM = 64
TOP_K = 8
N_ROUTE_GROUPS = 8
TOPK_GROUPS = 4
ROUTED_SCALE = 2.5
LN_EPS = 1e-5
NEG_INF = -1e30
PAST_LEN = 1024

SUBLANES = 8
LANES = 128
SEQS_PER_GROUP = SUBLANES
SSM_GROUPS_PER_BATCH = LANES // CH_PER_GROUP
VMEM_LIMIT_BYTES = 60 * 1024 * 1024

BF16 = jnp.bfloat16
F32 = jnp.float32


def _cparams(sem):
    return pltpu.CompilerParams(dimension_semantics=sem, vmem_limit_bytes=VMEM_LIMIT_BYTES)


def _const_spec(shape):
    nd = len(shape)
    return pl.BlockSpec(shape, lambda *_: (0,) * nd, pipeline_mode=pl.Buffered(1))


def _bias_kernel(bucket_ref, tbl_ref, out_ref):
    n_buckets, n_heads = tbl_ref.shape
    bk = bucket_ref[...]
    for h in range(n_heads):
        acc = jnp.zeros(bk.shape, F32)
        for b in range(n_buckets):
            acc = jnp.where(bk == b, tbl_ref[b, h], acc)
        out_ref[h] = acc


def _rel_bias_table(rel_bias):
    n_buckets, n_heads = rel_bias.shape
    band = WINDOW + CHUNK
    rel = (jnp.arange(band) - WINDOW)[None, :] - jnp.arange(CHUNK)[:, None]
    nb = n_buckets // 2
    max_exact = nb // 2
    n = jnp.abs(rel)
    nf = jnp.maximum(n, 1).astype(F32)
    large = max_exact + (jnp.log(nf / max_exact) / math.log(MAX_DISTANCE / max_exact) * (nb - max_exact)).astype(jnp.int32)
    large = jnp.minimum(large, nb - 1)
    bucket = (jnp.where(rel > 0, nb, 0) + jnp.where(n < max_exact, n, large)).astype(jnp.int32)
    return pl.pallas_call(
        _bias_kernel,
        out_shape=jax.ShapeDtypeStruct((n_heads, CHUNK, band), F32),
        in_specs=[pl.BlockSpec(memory_space=pltpu.VMEM), pl.BlockSpec(memory_space=pltpu.SMEM)],
        out_specs=pl.BlockSpec(memory_space=pltpu.VMEM),
        name="rel_bias_table",
    )(bucket, rel_bias.astype(F32))


def _disc_kernel(lr_ref, li_ref, ldt_ref, br_ref, bi_ref, are_ref, aim_ref, bbr_ref, bbi_ref):
    lr = lr_ref[0]
    li = li_ref[0]
    dt = jnp.exp(ldt_ref[0])
    mag = jnp.exp(lr * dt)
    a_re = mag * jnp.cos(li * dt)
    a_im = mag * jnp.sin(li * dt)
    den = lr * lr + li * li
    f_re = ((a_re - 1.0) * lr + a_im * li) / den
    f_im = (a_im * lr - (a_re - 1.0) * li) / den
    br = br_ref[0]
    bi = bi_ref[0]
    are_ref[0] = a_re
    aim_ref[0] = a_im
    bbr_ref[0] = f_re * br - f_im * bi
    bbi_ref[0] = f_re * bi + f_im * br


def _discretize(lam_re, lam_im, log_dt, b_re, b_im):
    depth, g, p = lam_re.shape
    c = b_re.shape[-1]
    spec_gp = pl.BlockSpec((1, g, 1, p), lambda l: (l, 0, 0, 0))
    spec_g1 = pl.BlockSpec((1, g, 1, 1), lambda l: (l, 0, 0, 0))
    spec_gcp = pl.BlockSpec((1, g, c, p), lambda l: (l, 0, 0, 0))
    return pl.pallas_call(
        _disc_kernel,
        grid=(depth,),
        out_shape=(jax.ShapeDtypeStruct((depth, g, 1, p), F32), jax.ShapeDtypeStruct((depth, g, 1, p), F32),
                   jax.ShapeDtypeStruct((depth, g, c, p), F32), jax.ShapeDtypeStruct((depth, g, c, p), F32)),
        in_specs=[spec_gp, spec_gp, spec_g1, spec_gcp, spec_gcp],
        out_specs=(spec_gp, spec_gp, spec_gcp, spec_gcp),
        name="ssm_discretize",
    )(lam_re.astype(F32).reshape(depth, g, 1, p), lam_im.astype(F32).reshape(depth, g, 1, p),
      log_dt.astype(F32).reshape(depth, g, 1, 1),
      jnp.swapaxes(b_re.astype(F32), -1, -2), jnp.swapaxes(b_im.astype(F32), -1, -2))


def _mixer_kernel(x_ref, kinit_ref, vinit_ref, hinit_ref, wqkv_ref, wu_ref, wgl_ref, bias_ref, sink_ref,
                  wap_ref, a_ref, wb_ref, wc_ref, dskip_ref, wglu_ref, wout_ref, lng_ref, lnb_ref,
                  x1_ref, kout_ref, vout_ref, hout_ref,
                  qkv_scr, attn_scr, kwin, vwin, h_scr, bu_scr, u_scr,
                  *, t_len, alpha, mask_missing_chunks, row_block):
    c = pl.program_id(1)
    n_c = pl.num_programs(1)
    rows, d_model = x_ref.shape
    win = kinit_ref.shape[1]
    q_slabs = attn_scr.shape[0]
    gq2 = q_slabs // N_KV_HEADS
    gp = a_ref.shape[1]
    d_ssm = u_scr.shape[1]
    n_gb = wb_ref.shape[0]
    gb_state = gp // n_gb
    scale = HEAD_DIM ** -0.5

    @pl.when(c == 0)
    def _():
        kwin[:, 0:win, :] = kinit_ref[...]
        vwin[:, 0:win, :] = vinit_ref[...]
        h_scr[...] = hinit_ref[...]

    xb = x_ref[...].astype(BF16)

    qkv = jnp.dot(xb, wqkv_ref[...], preferred_element_type=F32)
    for j in range(q_slabs + 2):
        qkv_scr[j] = qkv[:, j * LANES:(j + 1) * LANES]

    def attend(b, carry):
        seq_rows = pl.ds(b, t_len, stride=SEQS_PER_GROUP)
        kwin[b, win:win + t_len, :] = qkv_scr[q_slabs, seq_rows, :]
        vwin[b, win:win + t_len, :] = qkv_scr[q_slabs + 1, seq_rows, :]
        kfull = kwin[b]
        vfull = vwin[b]
        for hk in range(N_KV_HEADS):
            parts = []
            for s in range(gq2):
                qs = qkv_scr[hk * gq2 + s, seq_rows, :]
                parts += [qs[:, :HEAD_DIM], qs[:, HEAD_DIM:]]
            q4 = jnp.concatenate(parts, axis=0).astype(BF16)
            kh = kfull[:, hk * HEAD_DIM:(hk + 1) * HEAD_DIM].astype(BF16)
            vh = vfull[:, hk * HEAD_DIM:(hk + 1) * HEAD_DIM].astype(BF16)
            s_ = lax.dot_general(q4, kh, (((1,), (1,)), ((), ())), preferred_element_type=F32)
            s_ = s_ * scale + bias_ref[hk]
            if mask_missing_chunks:
                key = lax.broadcasted_iota(jnp.int32, s_.shape, 1)
                first_valid = (WINDOW // CHUNK - jnp.minimum(c, WINDOW // CHUNK)) * CHUNK
                s_ = jnp.where(key >= first_valid, s_, NEG_INF)
            sink = sink_ref[hk]
            m = jnp.maximum(jnp.max(s_, axis=-1, keepdims=True), sink)
            p = jnp.exp(s_ - m)
            p = p / (jnp.sum(p, axis=-1, keepdims=True) + jnp.exp(sink - m))
            o = jnp.dot(p.astype(BF16), vh, preferred_element_type=F32)
            for s in range(gq2):
                pair = jnp.concatenate([o[(2 * s) * t_len:(2 * s + 1) * t_len],
                                        o[(2 * s + 1) * t_len:(2 * s + 2) * t_len]], axis=1)
                attn_scr[hk * gq2 + s, seq_rows, :] = pair
        knext = kwin[b, t_len:t_len + win, :]
        vnext = vwin[b, t_len:t_len + win, :]
        kwin[b, 0:win, :] = knext
        vwin[b, 0:win, :] = vnext
        return carry

    lax.fori_loop(0, SEQS_PER_GROUP, attend, 0)

    @pl.when(c == n_c - 1)
    def _():
        kout_ref[...] = kwin[:, 0:win, :]
        vout_ref[...] = vwin[:, 0:win, :]

    u = jnp.dot(xb, wu_ref[...], preferred_element_type=F32)
    u_scr[...] = u
    ch_b = d_ssm // n_gb
    for j in range(n_gb):
        bu = jnp.dot(u[:, j * ch_b:(j + 1) * ch_b].astype(BF16), wb_ref[j], preferred_element_type=F32)
        bu_scr[:, j * gb_state:(j + 1) * gb_state] = bu[:, :gb_state]
        bu_scr[:, gp + j * gb_state:gp + (j + 1) * gb_state] = bu[:, gb_state:]

    scan_lanes = 4 * LANES
    for lc in range(gp // scan_lanes):
        re_l = slice(lc * scan_lanes, (lc + 1) * scan_lanes)
        im_l = slice(gp + lc * scan_lanes, gp + (lc + 1) * scan_lanes)
        a_re = jnp.broadcast_to(a_ref[0:1, re_l], (SEQS_PER_GROUP, scan_lanes))
        a_im = jnp.broadcast_to(a_ref[1:2, re_l], (SEQS_PER_GROUP, scan_lanes))

        def step(t, h, re_l=re_l, im_l=im_l, a_re=a_re, a_im=a_im):
            h_re, h_im = h
            r0 = pl.multiple_of(t * SEQS_PER_GROUP, SEQS_PER_GROUP)
            n_re = a_re * h_re - a_im * h_im + bu_scr[pl.ds(r0, SEQS_PER_GROUP), re_l]
            n_im = a_re * h_im + a_im * h_re + bu_scr[pl.ds(r0, SEQS_PER_GROUP), im_l]
            bu_scr[pl.ds(r0, SEQS_PER_GROUP), re_l] = n_re
            bu_scr[pl.ds(r0, SEQS_PER_GROUP), im_l] = n_im
            return n_re, n_im

        h_re, h_im = lax.fori_loop(0, t_len, step, (h_scr[:, re_l], h_scr[:, im_l]), unroll=8)
        h_scr[:, re_l] = h_re
        h_scr[:, im_l] = h_im

    @pl.when(c == n_c - 1)
    def _():
        hout_ref[...] = h_scr[...]

    for rb in range(rows // row_block):
        r = slice(rb * row_block, (rb + 1) * row_block)
        ys = []
        for j in range(n_gb):
            hcat = jnp.concatenate([bu_scr[r, j * gb_state:(j + 1) * gb_state],
                                    bu_scr[r, gp + j * gb_state:gp + (j + 1) * gb_state]], axis=1)
            ys.append(jnp.dot(hcat.astype(BF16), wc_ref[j], preferred_element_type=F32))
        y = jnp.concatenate(ys, axis=1)
        z = jax.nn.gelu(y + dskip_ref[...] * u_scr[r, :])
        glu = jnp.dot(z.astype(BF16), wglu_ref[...], preferred_element_type=F32)
        br_b = glu[:, :d_model] * jax.nn.sigmoid(glu[:, d_model:])
        attn = jnp.concatenate([attn_scr[s, r, :] for s in range(q_slabs)], axis=1)
        br_a = jnp.dot(attn.astype(BF16), wap_ref[...], preferred_element_type=F32)
        xr = x_ref[r, :]
        gl = jnp.dot(xr.astype(BF16), wgl_ref[...], preferred_element_type=F32)
        mix = jax.nn.sigmoid(gl[:, :d_model]) * br_a + jax.nn.sigmoid(gl[:, d_model:]) * br_b
        res = alpha * xr + jnp.dot(mix.astype(BF16), wout_ref[...], preferred_element_type=F32)
        mu = jnp.mean(res, axis=-1, keepdims=True)
        xc = res - mu
        var = jnp.mean(xc * xc, axis=-1, keepdims=True)
        x1_ref[r, :] = xc * lax.rsqrt(var + LN_EPS) * lng_ref[...] + lnb_ref[...]


def _mixer(x_all, row_offset_blocks, n_groups, n_steps, t_len, kinit, vinit, hinit, lw, bias, sinks, alpha,
           mask_missing_chunks, x1_alias=None):
    n_total, d_model = x_all.shape
    rows = SEQS_PER_GROUP * t_len
    win = kinit.shape[1]
    kvw = kinit.shape[2]
    gp2 = hinit.shape[1]
    gp = gp2 // 2
    q_slabs = lw["wap"].shape[0] // LANES
    d_ssm = lw["wu"].shape[1]
    band = win + t_len
    gq = bias.shape[0] // N_KV_HEADS
    bias_s = bias[:, :t_len, :band].reshape(N_KV_HEADS, gq * t_len, band)
    sink_s = jnp.broadcast_to(sinks.astype(F32).reshape(N_KV_HEADS, gq, 1, 1), (N_KV_HEADS, gq, t_len, 1))
    sink_s = sink_s.reshape(N_KV_HEADS, gq * t_len, 1)
    row_block = min(rows, 256)

    def xmap(g, c):
        return (row_offset_blocks + g * n_steps + c, 0)

    grp3 = lambda g, c: (g, 0, 0)
    grp2 = lambda g, c: (g, 0)
    in_specs = [
        pl.BlockSpec((rows, d_model), xmap),
        pl.BlockSpec((SEQS_PER_GROUP, win, kvw), grp3),
        pl.BlockSpec((SEQS_PER_GROUP, win, kvw), grp3),
        pl.BlockSpec((SEQS_PER_GROUP, gp2), grp2),
        _const_spec(lw["wqkv"].shape), _const_spec(lw["wu"].shape), _const_spec(lw["wgl"].shape),
        _const_spec(bias_s.shape), _const_spec(sink_s.shape), _const_spec(lw["wap"].shape),
        _const_spec(lw["a"].shape), _const_spec(lw["wb"].shape), _const_spec(lw["wc"].shape),
        _const_spec(lw["dskip"].shape), _const_spec(lw["wglu"].shape), _const_spec(lw["wout"].shape),
        _const_spec(lw["ln1g"].shape), _const_spec(lw["ln1b"].shape),
    ]
    out_specs = (
        pl.BlockSpec((rows, d_model), xmap),
        pl.BlockSpec((SEQS_PER_GROUP, win, kvw), grp3),
        pl.BlockSpec((SEQS_PER_GROUP, win, kvw), grp3),
        pl.BlockSpec((SEQS_PER_GROUP, gp2), grp2),
    )
    n_seq = n_groups * SEQS_PER_GROUP
    out_shape = (
        jax.ShapeDtypeStruct((n_total, d_model), F32),
        jax.ShapeDtypeStruct((n_seq, win, kvw), F32),
        jax.ShapeDtypeStruct((n_seq, win, kvw), F32),
        jax.ShapeDtypeStruct((n_seq, gp2), F32),
    )
    scratch = [
        pltpu.VMEM((q_slabs + 2, rows, LANES), F32),
        pltpu.VMEM((q_slabs, rows, LANES), F32),
        pltpu.VMEM((SEQS_PER_GROUP, band, kvw), F32),
        pltpu.VMEM((SEQS_PER_GROUP, band, kvw), F32),
        pltpu.VMEM((SEQS_PER_GROUP, gp2), F32),
        pltpu.VMEM((rows, gp2), F32),
        pltpu.VMEM((rows, d_ssm), F32),
    ]
    kern = functools.partial(_mixer_kernel, t_len=t_len, alpha=alpha,
                             mask_missing_chunks=mask_missing_chunks, row_block=row_block)
    args = [x_all, kinit, vinit, hinit, lw["wqkv"], lw["wu"], lw["wgl"], bias_s, sink_s, lw["wap"], lw["a"],
            lw["wb"], lw["wc"], lw["dskip"], lw["wglu"], lw["wout"], lw["ln1g"], lw["ln1b"]]
    aliases = {}
    if x1_alias is not None:
        args.append(x1_alias)
        in_specs.append(pl.BlockSpec(memory_space=pl.ANY))
        aliases = {len(args) - 1: 0}
        kern_inner = kern

        def kern(*refs):
            n_in = len(args)
            return kern_inner(*refs[:n_in - 1], *refs[n_in:])

    return pl.pallas_call(
        kern,
        grid=(n_groups, n_steps),
        in_specs=in_specs,
        out_specs=out_specs,
        out_shape=out_shape,
        scratch_shapes=scratch,
        input_output_aliases=aliases,
        compiler_params=_cparams(("arbitrary", "arbitrary")),
        name="mixer_t%d" % t_len,
    )(*args)


def _router_kernel(x_ref, wrt_ref, brt_ref, upper_ref, idx_ref, wts_ref, rank_ref, cnt_ref, base_scr):
    i = pl.program_id(0)
    n_exp = wrt_ref.shape[0]
    tb = x_ref.shape[0]
    per_group = n_exp // N_ROUTE_GROUPS
    lane_rep = tb // LANES

    @pl.when(i == 0)
    def _():
        base_scr[...] = jnp.zeros_like(base_scr)

    xb = x_ref[...].astype(BF16)
    logits = lax.dot_general(wrt_ref[...], xb, (((1,), (1,)), ((), ())), preferred_element_type=F32)
    scores = jax.nn.sigmoid(logits)
    biased = scores + jnp.concatenate([brt_ref[...]] * lane_rep, axis=1)
    neg = jnp.float32(-jnp.inf)
    big = jnp.float32(2 ** 20)

    gs = []
    member = lax.broadcasted_iota(jnp.int32, (per_group, tb), 0).astype(F32)
    for g in range(N_ROUTE_GROUPS):
        bg = biased[g * per_group:(g + 1) * per_group, :]
        m1 = jnp.max(bg, axis=0, keepdims=True)
        first = jnp.min(jnp.where(bg == m1, member, big), axis=0, keepdims=True)
        m2 = jnp.max(jnp.where(member == first, neg, bg), axis=0, keepdims=True)
        gs.append(m1 + m2)

    keep = [jnp.zeros((1, tb), F32) for _ in range(N_ROUTE_GROUPS)]
    for _ in range(TOPK_GROUPS):
        best = gs[0]
        for g in range(1, N_ROUTE_GROUPS):
            best = jnp.maximum(best, gs[g])
        taken = jnp.zeros((1, tb), F32)
        for g in range(N_ROUTE_GROUPS):
            pick = jnp.where(gs[g] == best, 1.0 - taken, 0.0)
            taken = taken + pick
            keep[g] = keep[g] + pick
            gs[g] = jnp.where(pick > 0.5, neg, gs[g])

    vals = jnp.concatenate(
        [jnp.where(jnp.broadcast_to(keep[g], (per_group, tb)) > 0.5,
                   biased[g * per_group:(g + 1) * per_group, :], NEG_INF) for g in range(N_ROUTE_GROUPS)],
        axis=0)
    eid = lax.broadcasted_iota(jnp.int32, (n_exp, tb), 0).astype(F32)
    onehot = jnp.zeros((n_exp, tb), F32)
    sel_idx = []
    sel_score = []
    for _ in range(TOP_K):
        m = jnp.max(vals, axis=0, keepdims=True)
        first = jnp.min(jnp.where(vals == m, eid, big), axis=0, keepdims=True)
        sel = eid == first
        sel_idx.append(first)
        sel_score.append(jnp.sum(jnp.where(sel, scores, 0.0), axis=0, keepdims=True))
        vals = jnp.where(sel, neg, vals)
        onehot = jnp.where(sel, 1.0, onehot)

    total = sel_score[0]
    for k in range(1, TOP_K):
        total = total + sel_score[k]

    prefix = jnp.dot(onehot.astype(BF16), upper_ref[...], preferred_element_type=F32)
    pos = prefix + jnp.concatenate([base_scr[...]] * lane_rep, axis=1)
    for k in range(TOP_K):
        idx_ref[k:k + 1, :] = sel_idx[k].astype(jnp.int32)
        wts_ref[k:k + 1, :] = sel_score[k] / total * ROUTED_SCALE
        rk = jnp.sum(jnp.where(eid == sel_idx[k], pos, 0.0), axis=0, keepdims=True)
        rank_ref[k:k + 1, :] = rk.astype(jnp.int32)
    base_scr[...] = base_scr[...] + jnp.broadcast_to(jnp.sum(onehot, axis=1, keepdims=True), base_scr.shape)
    cnt_ref[...] = base_scr[...]


def _router(x1, wrt, brt, tb):
    n, d_model = x1.shape
    n_exp = wrt.shape[0]
    upper = jnp.triu(jnp.ones((tb, tb), BF16), k=1)
    tok = lambda i: (0, i)
    return pl.pallas_call(
        _router_kernel,
        grid=(n // tb,),
        in_specs=[pl.BlockSpec((tb, d_model), lambda i: (i, 0)), _const_spec(wrt.shape), _const_spec(brt.shape),
                  _const_spec(upper.shape)],
        out_specs=(pl.BlockSpec((TOP_K, tb), tok), pl.BlockSpec((TOP_K, tb), tok), pl.BlockSpec((TOP_K, tb), tok),
                   pl.BlockSpec((n_exp, LANES), lambda i: (0, 0))),
        out_shape=(jax.ShapeDtypeStruct((TOP_K, n), jnp.int32), jax.ShapeDtypeStruct((TOP_K, n), F32),
                   jax.ShapeDtypeStruct((TOP_K, n), jnp.int32), jax.ShapeDtypeStruct((n_exp, LANES), F32)),
        scratch_shapes=[pltpu.VMEM((n_exp, LANES), F32)],
        compiler_params=_cparams(("arbitrary",)),
        name="router",
    )(x1, wrt, brt, upper)


def _dispatch_kernel(dest_ref, x_ref, xs_ref, sem):
    tb = x_ref.shape[0]

    def issue(t, carry):
        for k in range(TOP_K):
            pltpu.make_async_copy(x_ref.at[pl.ds(t, 1), :], xs_ref.at[pl.ds(dest_ref[k, t], 1), :], sem).start()
        return carry

    lax.fori_loop(0, tb, issue, 0)
    for k in range(TOP_K):
        pltpu.make_async_copy(x_ref, xs_ref.at[pl.ds(0, tb), :], sem).wait()


def _dispatch(dest, x1, n_slots, tb):
    n, d_model = x1.shape
    return pl.pallas_call(
        _dispatch_kernel,
        grid=(n // tb,),
        in_specs=[pl.BlockSpec((TOP_K, tb), lambda i: (0, i), memory_space=pltpu.SMEM),
                  pl.BlockSpec((tb, d_model), lambda i: (i, 0))],
        out_specs=pl.BlockSpec(memory_space=pl.ANY),
        out_shape=jax.ShapeDtypeStruct((n_slots, d_model), F32),
        scratch_shapes=[pltpu.SemaphoreType.DMA(())],
        compiler_params=_cparams(("arbitrary",)),
        name="dispatch",
    )(dest, x1)


def _expert_kernel(te_ref, nv_ref, xs_ref, wg_ref, wu_ref, wd_ref, os_ref, wgu_scr, wd_scr):
    i = pl.program_id(0)
    f = wg_ref.shape[2]
    prev = te_ref[jnp.maximum(i - 1, 0)]
    changed = jnp.logical_or(i == 0, te_ref[i] != prev)

    @pl.when(changed)
    def _():
        wgu_scr[:, :f] = wg_ref[0].astype(BF16)
        wgu_scr[:, f:] = wu_ref[0].astype(BF16)
        wd_scr[...] = wd_ref[0].astype(BF16)

    @pl.when(i < nv_ref[0])
    def _():
        gu = jnp.dot(xs_ref[...].astype(BF16), wgu_scr[...], preferred_element_type=F32)
        h = jax.nn.silu(gu[:, :f]) * gu[:, f:]
        os_ref[...] = jnp.dot(h.astype(BF16), wd_scr[...], preferred_element_type=F32)


def _experts(tile_exp, n_valid, xs, w_gate, w_up, w_down, bm):
    n_slots, d_model = xs.shape
    f = w_gate.shape[2]
    n_tiles = n_slots // bm
    row_map = lambda i, te, nv: (jnp.minimum(i, nv[0] - 1), 0)
    w_map = lambda i, te, nv: (te[i], 0, 0)
    return pl.pallas_call(
        _expert_kernel,
        grid_spec=pltpu.PrefetchScalarGridSpec(
            num_scalar_prefetch=2,
            grid=(n_tiles,),
            in_specs=[pl.BlockSpec((bm, d_model), row_map), pl.BlockSpec((1, d_model, f), w_map),
                      pl.BlockSpec((1, d_model, f), w_map), pl.BlockSpec((1, f, d_model), w_map)],
            out_specs=pl.BlockSpec((bm, d_model), row_map),
            scratch_shapes=[pltpu.VMEM((d_model, 2 * f), BF16), pltpu.VMEM((f, d_model), BF16)],
        ),
        out_shape=jax.ShapeDtypeStruct((n_slots, d_model), F32),
        compiler_params=_cparams(("arbitrary",)),
        name="experts",
    )(tile_exp, n_valid, xs, w_gate, w_up, w_down)


def _combine_kernel(dest_ref, x_ref, wts_ref, os_ref, wsgu_ref, wsd_ref, lng_ref, lnb_ref, x2_ref, gbuf, sem,
                    *, alpha):
    tb = x_ref.shape[0]
    f = wsd_ref.shape[0]

    def issue(t, carry):
        for k in range(TOP_K):
            pltpu.make_async_copy(os_ref.at[pl.ds(dest_ref[k, t], 1), :], gbuf.at[k, pl.ds(t, 1), :], sem).start()
        return carry

    lax.fori_loop(0, tb, issue, 0)

    x = x_ref[...]
    gu = jnp.dot(x.astype(BF16), wsgu_ref[...], preferred_element_type=F32)
    h = jax.nn.silu(gu[:, :f]) * gu[:, f:]
    shared = jnp.dot(h.astype(BF16), wsd_ref[...], preferred_element_type=F32)

    for k in range(TOP_K):
        pltpu.make_async_copy(os_ref.at[pl.ds(0, tb), :], gbuf.at[k], sem).wait()

    w = wts_ref[...]
    routed = gbuf[0] * w[:, 0:1]
    for k in range(1, TOP_K):
        routed = routed + gbuf[k] * w[:, k:k + 1]
    res = alpha * x + (routed + shared)
    mu = jnp.mean(res, axis=-1, keepdims=True)
    xc = res - mu
    var = jnp.mean(xc * xc, axis=-1, keepdims=True)
    x2_ref[...] = xc * lax.rsqrt(var + LN_EPS) * lng_ref[...] + lnb_ref[...]


def _combine(dest, x1, wts_nk, os_, wsgu, wsd, lng, lnb, alpha, tb):
    n, d_model = x1.shape
    return pl.pallas_call(
        functools.partial(_combine_kernel, alpha=alpha),
        grid=(n // tb,),
        in_specs=[pl.BlockSpec((TOP_K, tb), lambda i: (0, i), memory_space=pltpu.SMEM),
                  pl.BlockSpec((tb, d_model), lambda i: (i, 0)),
                  pl.BlockSpec((tb, TOP_K), lambda i: (i, 0)),
                  pl.BlockSpec(memory_space=pl.ANY),
                  _const_spec(wsgu.shape), _const_spec(wsd.shape), _const_spec(lng.shape), _const_spec(lnb.shape)],
        out_specs=pl.BlockSpec((tb, d_model), lambda i: (i, 0)),
        out_shape=jax.ShapeDtypeStruct((n, d_model), F32),
        scratch_shapes=[pltpu.VMEM((TOP_K, tb, d_model), F32), pltpu.SemaphoreType.DMA(())],
        compiler_params=_cparams(("arbitrary",)),
        name="combine",
    )(dest, x1, wts_nk, os_, wsgu, wsd, lng, lnb)


def _to_time_major(x):
    b, t, d = x.shape
    return x.reshape(b // SEQS_PER_GROUP, SEQS_PER_GROUP, t, d).transpose(0, 2, 1, 3).reshape(b * t, d)


def _from_time_major(x, b, t):
    d = x.shape[-1]
    return x.reshape(b // SEQS_PER_GROUP, t, SEQS_PER_GROUP, d).transpose(0, 2, 1, 3).reshape(b, t, d)


def _block_diag_in(bbt, n_gb):
    g, c, p = bbt.shape
    gl = g // n_gb
    eye = jnp.eye(gl, dtype=bbt.dtype)
    return jnp.einsum("jgcp,gh->jgchp", bbt.reshape(n_gb, gl, c, p), eye).reshape(n_gb, gl * c, gl * p)


def _block_diag_out(cm, n_gb):
    g, c, p = cm.shape
    gl = g // n_gb
    eye = jnp.eye(gl, dtype=cm.dtype)
    return jnp.einsum("jgcp,gh->jhpgc", cm.reshape(n_gb, gl, c, p), eye).reshape(n_gb, gl * p, gl * c)


def kernel(x_prompt, x_sample, cache_k, cache_v, state_ssm_re, state_ssm_im, rel_bias, w_in, attn_sinks, w_attn_proj, lam_re, lam_im, log_dt, b_re, b_im, c_re, c_im, d_skip, w_glu, w_out, ln1_g, ln1_b, w_router, b_router, w_gate, w_up, w_down, ws_gate, ws_up, ws_down, ln2_g, ln2_b):
    batch, seq, d_model = x_prompt.shape
    dec_batch, dec_seq, _ = x_sample.shape
    depth = w_in.shape[0]
    win = cache_k.shape[2]
    n_q_heads = attn_sinks.shape[1]
    q_w = n_q_heads * HEAD_DIM
    kv_w = N_KV_HEADS * HEAD_DIM
    d_ssm = d_skip.shape[1]
    n_groups_ssm, state_dim = lam_re.shape[1], lam_re.shape[2]
    gp = n_groups_ssm * state_dim
    n_exp = w_router.shape[2]
    alpha = (2 * depth) ** 0.25
    assert win == WINDOW and seq % CHUNK == 0 and dec_seq <= CHUNK
    assert batch % SEQS_PER_GROUP == 0 and dec_batch % SEQS_PER_GROUP == 0
    qch = (PAST_LEN + np.arange(dec_seq)) // CHUNK
    kch = np.concatenate([PAST_LEN - win + np.arange(win), PAST_LEN + np.arange(dec_seq)]) // CHUNK
    assert np.all((kch[None, :] >= qch[:, None] - WINDOW // CHUNK) & (kch[None, :] <= qch[:, None]))

    n_p = batch * seq
    n_s = dec_batch * dec_seq
    n_tok = n_p + n_s
    n_gb = n_groups_ssm // SSM_GROUPS_PER_BATCH

    bias = _rel_bias_table(rel_bias)
    a_re, a_im, bbt_re, bbt_im = _discretize(lam_re, lam_im, log_dt, b_re, b_im)

    x_all = jnp.concatenate([_to_time_major(x_prompt.astype(F32)), _to_time_major(x_sample.astype(F32))], axis=0)
    zeros_kv = jnp.zeros((batch, win, kv_w), F32)
    zeros_h = jnp.zeros((batch, 2 * gp), F32)

    tb_route = 512 if n_tok % 512 == 0 else 256
    tb_move = 256
    bm = 256
    n_tiles = -(-(n_tok * TOP_K + n_exp * (bm - 1)) // bm)
    n_slots = n_tiles * bm

    outs = {k: [] for k in ("kp", "vp", "hp", "ks", "vs", "hs")}
    for l in range(depth):
        wl = w_in[l].astype(BF16)
        lw = dict(
            wqkv=wl[:, :q_w + 2 * kv_w], wu=wl[:, q_w + 2 * kv_w:q_w + 2 * kv_w + d_ssm],
            wgl=wl[:, q_w + 2 * kv_w + d_ssm:], wap=w_attn_proj[l].astype(BF16),
            a=jnp.stack([a_re[l].reshape(gp), a_im[l].reshape(gp)]),
            wb=jnp.concatenate([_block_diag_in(bbt_re[l], n_gb), _block_diag_in(bbt_im[l], n_gb)], axis=2).astype(BF16),
            wc=jnp.concatenate([_block_diag_out(c_re[l].astype(F32), n_gb),
                                -_block_diag_out(c_im[l].astype(F32), n_gb)], axis=1).astype(BF16),
            dskip=d_skip[l].astype(F32).reshape(1, d_ssm), wglu=w_glu[l].astype(BF16), wout=w_out[l].astype(BF16),
            ln1g=ln1_g[l].astype(F32).reshape(1, d_model), ln1b=ln1_b[l].astype(F32).reshape(1, d_model),
        )
        x1, kp, vp, hp = _mixer(x_all, 0, batch // SEQS_PER_GROUP, seq // CHUNK, CHUNK, zeros_kv, zeros_kv, zeros_h,
                                lw, bias, attn_sinks[l], alpha, True)
        hinit = jnp.concatenate([state_ssm_re[l].astype(F32).reshape(dec_batch, gp),
                                 state_ssm_im[l].astype(F32).reshape(dec_batch, gp)], axis=1)
        x1, ks, vs, hs = _mixer(x_all, n_p // (SEQS_PER_GROUP * dec_seq), dec_batch // SEQS_PER_GROUP, 1, dec_seq,
                                cache_k[l].astype(F32).reshape(dec_batch, win, kv_w),
                                cache_v[l].astype(F32).reshape(dec_batch, win, kv_w), hinit,
                                lw, bias, attn_sinks[l], alpha, False, x1_alias=x1)
        for name, val in zip(("kp", "vp", "hp", "ks", "vs", "hs"), (kp, vp, hp, ks, vs, hs)):
            outs[name].append(val)

        wrt = w_router[l].astype(BF16).T
        brt = jnp.broadcast_to(b_router[l].astype(F32)[:, None], (n_exp, LANES))
        idx_t, wts_t, rank_t, cnt = _router(x1, wrt, brt, tb_route)
        counts = cnt[:, 0].astype(jnp.int32)
        padded = (counts + bm - 1) // bm * bm
        pend = jnp.cumsum(padded)
        pstart = pend - padded
        dest = (pstart[idx_t] + rank_t).astype(jnp.int32)
        n_valid = (pend[-1] // bm).astype(jnp.int32).reshape(1)
        tile_exp = jnp.minimum(jnp.searchsorted(pend, jnp.arange(n_tiles, dtype=jnp.int32) * bm, side="right"),
                               n_exp - 1).astype(jnp.int32)
        xs = _dispatch(dest, x1, n_slots, tb_move)
        os_ = _experts(tile_exp, n_valid, xs, w_gate[l], w_up[l], w_down[l], bm)
        wsgu = jnp.concatenate([ws_gate[l], ws_up[l]], axis=1).astype(BF16)
        x_all = _combine(dest, x1, wts_t.T, os_, wsgu, ws_down[l].astype(BF16),
                         ln2_g[l].astype(F32).reshape(1, d_model), ln2_b[l].astype(F32).reshape(1, d_model),
                         alpha, tb_move)

    y_prompt = _from_time_major(x_all[:n_p], batch, seq)
    y_sample = _from_time_major(x_all[n_p:], dec_batch, dec_seq)

    def kv(vals, b):
        return jnp.stack(vals).reshape(depth, b, win, N_KV_HEADS, HEAD_DIM)

    def st(vals, b, part):
        return jnp.stack(vals)[:, :, part * gp:(part + 1) * gp].reshape(depth, b, n_groups_ssm, state_dim)

    return (y_prompt, y_sample, kv(outs["kp"], batch), kv(outs["vp"], batch), st(outs["hp"], batch, 0),
            st(outs["hp"], batch, 1), kv(outs["ks"], dec_batch), kv(outs["vs"], dec_batch),
            st(outs["hs"], dec_batch, 0), st(outs["hs"], dec_batch, 1))
```

```python
import functools
import math

import jax
import jax.numpy as jnp
import numpy as np
from jax import lax
from jax.experimental import pallas as pl
from jax.experimental.pallas import tpu as pltpu

CHUNK = 64
WINDOW = 128
HEAD_DIM = 64
N_KV_HEADS = 2
MAX_DISTANCE = 128
CH_PER_GROUP = 16
STATE_DIM = 64
TOP_K = 8
N_ROUTE_GROUPS = 8
TOPK_GROUPS = 4
ROUTED_SCALE = 2.5
LN_EPS = 1e-5
NEG_INF = -1e30
PAST_LEN = 1024

SUBLANES = 8
LANES = 128
SEQS_PER_GROUP = SUBLANES
SSM_GROUPS_PER_BATCH = LANES // CH_PER_GROUP
VMEM_LIMIT_BYTES = 60 * 1024 * 1024
ATTN_SEQS_PER_ITER = 8

BF16 = jnp.bfloat16
F32 = jnp.float32


def _cparams(sem):
    return pltpu.CompilerParams(dimension_semantics=sem, vmem_limit_bytes=VMEM_LIMIT_BYTES)


def _const_spec(shape):
    nd = len(shape)
    return pl.BlockSpec(shape, lambda *_: (0,) * nd, pipeline_mode=pl.Buffered(1))


def _bias_kernel(bucket_ref, tbl_ref, out_ref):
    n_buckets, n_heads = tbl_ref.shape
    bk = bucket_ref[...]
    for h in range(n_heads):
        acc = jnp.zeros(bk.shape, F32)
        for b in range(n_buckets):
            acc = jnp.where(bk == b, tbl_ref[b, h], acc)
        out_ref[h] = acc


def _rel_bias_table(rel_bias):
    n_buckets, n_heads = rel_bias.shape
    band = WINDOW + CHUNK
    rel = (jnp.arange(band) - WINDOW)[None, :] - jnp.arange(CHUNK)[:, None]
    nb = n_buckets // 2
    max_exact = nb // 2
    n = jnp.abs(rel)
    nf = jnp.maximum(n, 1).astype(F32)
    large = max_exact + (jnp.log(nf / max_exact) / math.log(MAX_DISTANCE / max_exact) * (nb - max_exact)).astype(jnp.int32)
    large = jnp.minimum(large, nb - 1)
    bucket = (jnp.where(rel > 0, nb, 0) + jnp.where(n < max_exact, n, large)).astype(jnp.int32)
    return pl.pallas_call(
        _bias_kernel,
        out_shape=jax.ShapeDtypeStruct((n_heads, CHUNK, band), F32),
        in_specs=[pl.BlockSpec(memory_space=pltpu.VMEM), pl.BlockSpec(memory_space=pltpu.SMEM)],
        out_specs=pl.BlockSpec(memory_space=pltpu.VMEM),
        name="rel_bias_table",
    )(bucket, rel_bias.astype(F32))


def _disc_kernel(lr_ref, li_ref, ldt_ref, br_ref, bi_ref, are_ref, aim_ref, bbr_ref, bbi_ref):
    lr = lr_ref[0]
    li = li_ref[0]
    dt = jnp.exp(ldt_ref[0])
    mag = jnp.exp(lr * dt)
    a_re = mag * jnp.cos(li * dt)
    a_im = mag * jnp.sin(li * dt)
    den = lr * lr + li * li
    f_re = ((a_re - 1.0) * lr + a_im * li) / den
    f_im = (a_im * lr - (a_re - 1.0) * li) / den
    br = br_ref[0]
    bi = bi_ref[0]
    are_ref[0] = a_re
    aim_ref[0] = a_im
    bbr_ref[0] = f_re * br - f_im * bi
    bbi_ref[0] = f_re * bi + f_im * br


def _discretize(lam_re, lam_im, log_dt, b_re, b_im):
    depth, g, p = lam_re.shape
    c = b_re.shape[-1]
    spec_gp = pl.BlockSpec((1, g, 1, p), lambda l: (l, 0, 0, 0))
    spec_g1 = pl.BlockSpec((1, g, 1, 1), lambda l: (l, 0, 0, 0))
    spec_gcp = pl.BlockSpec((1, g, c, p), lambda l: (l, 0, 0, 0))
    return pl.pallas_call(
        _disc_kernel,
        grid=(depth,),
        out_shape=(jax.ShapeDtypeStruct((depth, g, 1, p), F32), jax.ShapeDtypeStruct((depth, g, 1, p), F32),
                   jax.ShapeDtypeStruct((depth, g, c, p), F32), jax.ShapeDtypeStruct((depth, g, c, p), F32)),
        in_specs=[spec_gp, spec_gp, spec_g1, spec_gcp, spec_gcp],
        out_specs=(spec_gp, spec_gp, spec_gcp, spec_gcp),
        name="ssm_discretize",
    )(lam_re.astype(F32).reshape(depth, g, 1, p), lam_im.astype(F32).reshape(depth, g, 1, p),
      log_dt.astype(F32).reshape(depth, g, 1, 1),
      jnp.swapaxes(b_re.astype(F32), -1, -2), jnp.swapaxes(b_im.astype(F32), -1, -2))


def _mixer_kernel(x_ref, kinit_ref, vinit_ref, hinit_ref, wqkv_ref, wu_ref, wgl_ref, bias_ref,
                  wap_ref, a_ref, wb_ref, wc_ref, dskip_ref, wglu_ref, wout_ref, lng_ref, lnb_ref,
                  x1_ref, kout_ref, vout_ref, hout_ref,
                  qkv_scr, attn_scr, kwin, vwin, h_scr, bu_scr, u_scr,
                  *, t_len, alpha, mask_missing_chunks, row_block, seqs_per_iter):
    c = pl.program_id(1)
    n_c = pl.num_programs(1)
    rows, d_model = x_ref.shape
    win = kinit_ref.shape[1]
    key_pad = kwin.shape[1]
    q_slabs = attn_scr.shape[0]
    gq2 = q_slabs // N_KV_HEADS
    gp = a_ref.shape[1]
    d_ssm = u_scr.shape[1]
    n_gb = wb_ref.shape[0]
    gb_state = gp // n_gb
    scale = HEAD_DIM ** -0.5

    @pl.when(c == 0)
    def _():
        kwin[:, 0:win, :] = kinit_ref[...]
        vwin[:, 0:win, :] = vinit_ref[...]
        kwin[:, win + t_len:, :] = jnp.zeros((SEQS_PER_GROUP, key_pad - win - t_len, kwin.shape[2]), F32)
        vwin[:, win + t_len:, :] = jnp.zeros((SEQS_PER_GROUP, key_pad - win - t_len, vwin.shape[2]), F32)
        h_scr[...] = hinit_ref[...]

    xb = x_ref[...].astype(BF16)

    qkv = jnp.dot(xb, wqkv_ref[...], preferred_element_type=F32)
    for j in range(q_slabs + 2):
        qkv_scr[j] = qkv[:, j * LANES:(j + 1) * LANES]

    def append_kv(b, carry):
        seq_rows = pl.ds(b, t_len, stride=SEQS_PER_GROUP)
        kwin[b, win:win + t_len, :] = qkv_scr[q_slabs, seq_rows, :]
        vwin[b, win:win + t_len, :] = qkv_scr[q_slabs + 1, seq_rows, :]
        return carry

    lax.fori_loop(0, SEQS_PER_GROUP, append_kv, 0)

    def attend_group(i, carry):
        chains = [(i * seqs_per_iter + u, hk) for u in range(seqs_per_iter) for hk in range(N_KV_HEADS)]
        scores, values = [], []
        for b, hk in chains:
            seq_rows = pl.ds(b, t_len, stride=SEQS_PER_GROUP)
            parts = []
            for s in range(gq2):
                qs = qkv_scr[hk * gq2 + s, seq_rows, :]
                parts += [qs[:, :HEAD_DIM], qs[:, HEAD_DIM:]]
            q4 = jnp.concatenate(parts, axis=0).astype(BF16)
            kh = kwin[b, :, hk * HEAD_DIM:(hk + 1) * HEAD_DIM].astype(BF16)
            values.append(vwin[b, :, hk * HEAD_DIM:(hk + 1) * HEAD_DIM].astype(BF16))
            scores.append(lax.dot_general(q4, kh, (((1,), (1,)), ((), ())), preferred_element_type=F32))
        probs = []
        for (b, hk), s_ in zip(chains, scores):
            s_ = s_ * scale + bias_ref[hk]
            if mask_missing_chunks:
                key = lax.broadcasted_iota(jnp.int32, s_.shape, 1)
                first_valid = (WINDOW // CHUNK - jnp.minimum(c, WINDOW // CHUNK)) * CHUNK
                s_ = jnp.where(key >= first_valid, s_, NEG_INF)
            m = jnp.max(s_, axis=-1, keepdims=True)
            probs.append(jnp.exp(s_ - m).astype(BF16))
        ones = jnp.ones((key_pad, HEAD_DIM), BF16)
        outs = [jnp.dot(p, vh, preferred_element_type=F32) / jnp.dot(p, ones, preferred_element_type=F32)
                for p, vh in zip(probs, values)]
        for (b, hk), o in zip(chains, outs):
            for s in range(gq2):
                pair = jnp.concatenate([o[(2 * s) * t_len:(2 * s + 1) * t_len],
                                        o[(2 * s + 1) * t_len:(2 * s + 2) * t_len]], axis=1)
                attn_scr[hk * gq2 + s, pl.ds(b, t_len, stride=SEQS_PER_GROUP), :] = pair
        return carry

    lax.fori_loop(0, SEQS_PER_GROUP // seqs_per_iter, attend_group, 0)

    def slide_window(b, carry):
        knext = kwin[b, t_len:t_len + win, :]
        vnext = vwin[b, t_len:t_len + win, :]
        kwin[b, 0:win, :] = knext
        vwin[b, 0:win, :] = vnext
        return carry

    lax.fori_loop(0, SEQS_PER_GROUP, slide_window, 0)

    @pl.when(c == n_c - 1)
    def _():
        kout_ref[...] = kwin[:, 0:win, :]
        vout_ref[...] = vwin[:, 0:win, :]

    u = jnp.dot(xb, wu_ref[...], preferred_element_type=F32)
    u_scr[...] = u
    ch_b = d_ssm // n_gb
    for j in range(n_gb):
        bu = jnp.dot(u[:, j * ch_b:(j + 1) * ch_b].astype(BF16), wb_ref[j], preferred_element_type=F32)
        bu_scr[:, j * gb_state:(j + 1) * gb_state] = bu[:, :gb_state]
        bu_scr[:, gp + j * gb_state:gp + (j + 1) * gb_state] = bu[:, gb_state:]

    scan_lanes = 4 * LANES
    for lc in range(gp // scan_lanes):
        re_l = slice(lc * scan_lanes, (lc + 1) * scan_lanes)
        im_l = slice(gp + lc * scan_lanes, gp + (lc + 1) * scan_lanes)
        a_re = jnp.broadcast_to(a_ref[0:1, re_l], (SEQS_PER_GROUP, scan_lanes))
        a_im = jnp.broadcast_to(a_ref[1:2, re_l], (SEQS_PER_GROUP, scan_lanes))

        def step(t, h, re_l=re_l, im_l=im_l, a_re=a_re, a_im=a_im):
            h_re, h_im = h
            r0 = pl.multiple_of(t * SEQS_PER_GROUP, SEQS_PER_GROUP)
            n_re = a_re * h_re - a_im * h_im + bu_scr[pl.ds(r0, SEQS_PER_GROUP), re_l]
            n_im = a_re * h_im + a_im * h_re + bu_scr[pl.ds(r0, SEQS_PER_GROUP), im_l]
            bu_scr[pl.ds(r0, SEQS_PER_GROUP), re_l] = n_re
            bu_scr[pl.ds(r0, SEQS_PER_GROUP), im_l] = n_im
            return n_re, n_im

        h_re, h_im = lax.fori_loop(0, t_len, step, (h_scr[:, re_l], h_scr[:, im_l]), unroll=8)
        h_scr[:, re_l] = h_re
        h_scr[:, im_l] = h_im

    @pl.when(c == n_c - 1)
    def _():
        hout_ref[...] = h_scr[...]

    for rb in range(rows // row_block):
        r = slice(rb * row_block, (rb + 1) * row_block)
        ys = []
        for j in range(n_gb):
            hcat = jnp.concatenate([bu_scr[r, j * gb_state:(j + 1) * gb_state],
                                    bu_scr[r, gp + j * gb_state:gp + (j + 1) * gb_state]], axis=1)
            ys.append(jnp.dot(hcat.astype(BF16), wc_ref[j], preferred_element_type=F32))
        y = jnp.concatenate(ys, axis=1)
        z = jax.nn.gelu(y + dskip_ref[...] * u_scr[r, :])
        glu = jnp.dot(z.astype(BF16), wglu_ref[...], preferred_element_type=F32)
        br_b = glu[:, :d_model] * jax.nn.sigmoid(glu[:, d_model:])
        attn = jnp.concatenate([attn_scr[s, r, :] for s in range(q_slabs)], axis=1)
        br_a = jnp.dot(attn.astype(BF16), wap_ref[...], preferred_element_type=F32)
        xr = x_ref[r, :]
        gl = jnp.dot(xr.astype(BF16), wgl_ref[...], preferred_element_type=F32)
        mix = jax.nn.sigmoid(gl[:, :d_model]) * br_a + jax.nn.sigmoid(gl[:, d_model:]) * br_b
        res = alpha * xr + jnp.dot(mix.astype(BF16), wout_ref[...], preferred_element_type=F32)
        mu = jnp.mean(res, axis=-1, keepdims=True)
        xc = res - mu
        var = jnp.mean(xc * xc, axis=-1, keepdims=True)
        x1_ref[r, :] = xc * lax.rsqrt(var + LN_EPS) * lng_ref[...] + lnb_ref[...]


def _mixer(x_all, row_offset_blocks, n_groups, n_steps, t_len, kinit, vinit, hinit, lw, bias, sinks, alpha,
           mask_missing_chunks):
    n_total, d_model = x_all.shape
    rows = SEQS_PER_GROUP * t_len
    win = kinit.shape[1]
    kvw = kinit.shape[2]
    gp2 = hinit.shape[1]
    gp = gp2 // 2
    q_slabs = lw["wap"].shape[0] // LANES
    d_ssm = lw["wu"].shape[1]
    band = win + t_len
    gq = bias.shape[0] // N_KV_HEADS
    key_pad = -(-(band + 1) // LANES) * LANES
    sink_col = jnp.broadcast_to(sinks.astype(F32).reshape(N_KV_HEADS * gq, 1, 1), (N_KV_HEADS * gq, t_len, 1))
    dead_cols = jnp.full((N_KV_HEADS * gq, t_len, key_pad - band - 1), NEG_INF, F32)
    bias_s = jnp.concatenate([bias[:, :t_len, :band], sink_col, dead_cols], axis=2)
    bias_s = bias_s.reshape(N_KV_HEADS, gq * t_len, key_pad)
    row_block = min(rows, 256)

    def xmap(g, c):
        return (row_offset_blocks + g * n_steps + c, 0)

    grp3 = lambda g, c: (g, 0, 0)
    grp2 = lambda g, c: (g, 0)
    in_specs = [
        pl.BlockSpec((rows, d_model), xmap),
        pl.BlockSpec((SEQS_PER_GROUP, win, kvw), grp3),
        pl.BlockSpec((SEQS_PER_GROUP, win, kvw), grp3),
        pl.BlockSpec((SEQS_PER_GROUP, gp2), grp2),
        _const_spec(lw["wqkv"].shape), _const_spec(lw["wu"].shape), _const_spec(lw["wgl"].shape),
        _const_spec(bias_s.shape), _const_spec(lw["wap"].shape),
        _const_spec(lw["a"].shape), _const_spec(lw["wb"].shape), _const_spec(lw["wc"].shape),
        _const_spec(lw["dskip"].shape), _const_spec(lw["wglu"].shape), _const_spec(lw["wout"].shape),
        _const_spec(lw["ln1g"].shape), _const_spec(lw["ln1b"].shape),
    ]
    out_specs = (
        pl.BlockSpec((rows, d_model), xmap),
        pl.BlockSpec((SEQS_PER_GROUP, win, kvw), grp3),
        pl.BlockSpec((SEQS_PER_GROUP, win, kvw), grp3),
        pl.BlockSpec((SEQS_PER_GROUP, gp2), grp2),
    )
    n_seq = n_groups * SEQS_PER_GROUP
    out_shape = (
        jax.ShapeDtypeStruct((n_total, d_model), F32),
        jax.ShapeDtypeStruct((n_seq, win, kvw), F32),
        jax.ShapeDtypeStruct((n_seq, win, kvw), F32),
        jax.ShapeDtypeStruct((n_seq, gp2), F32),
    )
    scratch = [
        pltpu.VMEM((q_slabs + 2, rows, LANES), F32),
        pltpu.VMEM((q_slabs, rows, LANES), F32),
        pltpu.VMEM((SEQS_PER_GROUP, key_pad, kvw), F32),
        pltpu.VMEM((SEQS_PER_GROUP, key_pad, kvw), F32),
        pltpu.VMEM((SEQS_PER_GROUP, gp2), F32),
        pltpu.VMEM((rows, gp2), F32),
        pltpu.VMEM((rows, d_ssm), F32),
    ]
    kern = functools.partial(_mixer_kernel, t_len=t_len, alpha=alpha,
                             mask_missing_chunks=mask_missing_chunks, row_block=row_block,
                             seqs_per_iter=ATTN_SEQS_PER_ITER)
    args = [x_all, kinit, vinit, hinit, lw["wqkv"], lw["wu"], lw["wgl"], bias_s, lw["wap"], lw["a"],
            lw["wb"], lw["wc"], lw["dskip"], lw["wglu"], lw["wout"], lw["ln1g"], lw["ln1b"]]
    return pl.pallas_call(
        kern,
        grid=(n_groups, n_steps),
        in_specs=in_specs,
        out_specs=out_specs,
        out_shape=out_shape,
        scratch_shapes=scratch,
        input_output_aliases={0: 0},
        compiler_params=_cparams(("arbitrary", "arbitrary")),
        name="mixer_t%d" % t_len,
    )(*args)


def _router_kernel(x_ref, wrt_ref, brt_ref, upper_ref, idx_ref, wts_ref, rank_ref, cnt_ref, base_scr):
    i = pl.program_id(0)
    n_exp = wrt_ref.shape[0]
    tb = x_ref.shape[0]
    per_group = n_exp // N_ROUTE_GROUPS
    lane_rep = tb // LANES

    @pl.when(i == 0)
    def _():
        base_scr[...] = jnp.zeros_like(base_scr)

    xb = x_ref[...].astype(BF16)
    logits = lax.dot_general(wrt_ref[...], xb, (((1,), (1,)), ((), ())), preferred_element_type=F32)
    scores = jax.nn.sigmoid(logits)
    biased = scores + jnp.concatenate([brt_ref[...]] * lane_rep, axis=1)
    neg = jnp.float32(-jnp.inf)
    big = jnp.float32(2 ** 20)

    gs = []
    member = lax.broadcasted_iota(jnp.int32, (per_group, tb), 0).astype(F32)
    for g in range(N_ROUTE_GROUPS):
        bg = biased[g * per_group:(g + 1) * per_group, :]
        m1 = jnp.max(bg, axis=0, keepdims=True)
        first = jnp.min(jnp.where(bg == m1, member, big), axis=0, keepdims=True)
        m2 = jnp.max(jnp.where(member == first, neg, bg), axis=0, keepdims=True)
        gs.append(m1 + m2)

    keep = [jnp.zeros((1, tb), F32) for _ in range(N_ROUTE_GROUPS)]
    for _ in range(TOPK_GROUPS):
        best = gs[0]
        for g in range(1, N_ROUTE_GROUPS):
            best = jnp.maximum(best, gs[g])
        taken = jnp.zeros((1, tb), F32)
        for g in range(N_ROUTE_GROUPS):
            pick = jnp.where(gs[g] == best, 1.0 - taken, 0.0)
            taken = taken + pick
            keep[g] = keep[g] + pick
            gs[g] = jnp.where(pick > 0.5, neg, gs[g])

    vals = jnp.concatenate(
        [jnp.where(jnp.broadcast_to(keep[g], (per_group, tb)) > 0.5,
                   biased[g * per_group:(g + 1) * per_group, :], NEG_INF) for g in range(N_ROUTE_GROUPS)],
        axis=0)
    eid = lax.broadcasted_iota(jnp.int32, (n_exp, tb), 0).astype(F32)
    onehot = jnp.zeros((n_exp, tb), F32)
    sel_idx = []
    sel_score = []
    for _ in range(TOP_K):
        m = jnp.max(vals, axis=0, keepdims=True)
        first = jnp.min(jnp.where(vals == m, eid, big), axis=0, keepdims=True)
        sel = eid == first
        sel_idx.append(first)
        sel_score.append(jnp.sum(jnp.where(sel, scores, 0.0), axis=0, keepdims=True))
        vals = jnp.where(sel, neg, vals)
        onehot = jnp.where(sel, 1.0, onehot)

    total = sel_score[0]
    for k in range(1, TOP_K):
        total = total + sel_score[k]

    prefix = jnp.dot(onehot.astype(BF16), upper_ref[...], preferred_element_type=F32)
    pos = prefix + jnp.concatenate([base_scr[...]] * lane_rep, axis=1)
    for k in range(TOP_K):
        idx_ref[k:k + 1, :] = sel_idx[k].astype(jnp.int32)
        wts_ref[k:k + 1, :] = sel_score[k] / total * ROUTED_SCALE
        rk = jnp.sum(jnp.where(eid == sel_idx[k], pos, 0.0), axis=0, keepdims=True)
        rank_ref[k:k + 1, :] = rk.astype(jnp.int32)
    base_scr[...] = base_scr[...] + jnp.broadcast_to(jnp.sum(onehot, axis=1, keepdims=True), base_scr.shape)
    cnt_ref[...] = base_scr[...]


def _router(x1, wrt, brt, tb):
    n, d_model = x1.shape
    n_exp = wrt.shape[0]
    upper = jnp.triu(jnp.ones((tb, tb), BF16), k=1)
    tok = lambda i: (0, i)
    return pl.pallas_call(
        _router_kernel,
        grid=(n // tb,),
        in_specs=[pl.BlockSpec((tb, d_model), lambda i: (i, 0)), _const_spec(wrt.shape), _const_spec(brt.shape),
                  _const_spec(upper.shape)],
        out_specs=(pl.BlockSpec((TOP_K, tb), tok), pl.BlockSpec((TOP_K, tb), tok), pl.BlockSpec((TOP_K, tb), tok),
                   pl.BlockSpec((n_exp, LANES), lambda i: (0, 0))),
        out_shape=(jax.ShapeDtypeStruct((TOP_K, n), jnp.int32), jax.ShapeDtypeStruct((TOP_K, n), F32),
                   jax.ShapeDtypeStruct((TOP_K, n), jnp.int32), jax.ShapeDtypeStruct((n_exp, LANES), F32)),
        scratch_shapes=[pltpu.VMEM((n_exp, LANES), F32)],
        compiler_params=_cparams(("arbitrary",)),
        name="router",
    )(x1, wrt, brt, upper)


def _pack_bf16_pairs(x):
    h = x.shape[1] // 2
    xb = x.astype(BF16).astype(F32)
    lo = lax.bitcast_convert_type(xb[:, :h], jnp.uint32) >> 16
    return lo | (lax.bitcast_convert_type(xb[:, h:], jnp.uint32) & jnp.uint32(0xFFFF0000))


def _unpack_lo(w):
    return lax.bitcast_convert_type(w << 16, F32)


def _unpack_hi(w):
    return lax.bitcast_convert_type(w & jnp.uint32(0xFFFF0000), F32)


def _dest_kernel(cstart_ref, idx_ref, rank_ref, dest_ref):
    idx = idx_ref[...]
    rank = rank_ref[...]

    def body(e, acc):
        return jnp.where(idx == e, cstart_ref[e] + rank, acc)

    dest_ref[...] = lax.fori_loop(0, cstart_ref.shape[0], body, jnp.zeros_like(rank), unroll=8)


def _dest(cstart, idx_t, rank_t, tb):
    n = idx_t.shape[1]
    tok = lambda i: (0, i)
    return pl.pallas_call(
        _dest_kernel,
        grid=(n // tb,),
        in_specs=[pl.BlockSpec(memory_space=pltpu.SMEM), pl.BlockSpec((TOP_K, tb), tok),
                  pl.BlockSpec((TOP_K, tb), tok)],
        out_specs=pl.BlockSpec((TOP_K, tb), tok),
        out_shape=jax.ShapeDtypeStruct((TOP_K, n), jnp.int32),
        compiler_params=_cparams(("arbitrary",)),
        name="slot_index",
    )(cstart, idx_t, rank_t)


def _dispatch_kernel(dest_ref, x_ref, xs_ref, buf, sem):
    tb = x_ref.shape[0]
    words = _pack_bf16_pairs(x_ref[...])
    for j in range(buf.shape[1]):
        buf[:, j, :] = words[:, j * LANES:(j + 1) * LANES]

    def issue(t, carry):
        for k in range(TOP_K):
            pltpu.make_async_copy(buf.at[pl.ds(t, 1)], xs_ref.at[pl.ds(dest_ref[k, t], 1)], sem).start(priority=k % 2)
        return carry

    lax.fori_loop(0, tb, issue, 0, unroll=2)
    for k in range(TOP_K):
        pltpu.make_async_copy(buf, xs_ref.at[pl.ds(0, tb)], sem).wait()


def _dispatch(dest, x1, n_slots, tb):
    n, d_model = x1.shape
    row_tiles = d_model // 2 // LANES
    return pl.pallas_call(
        _dispatch_kernel,
        grid=(n // tb,),
        in_specs=[pl.BlockSpec((TOP_K, tb), lambda i: (0, i), memory_space=pltpu.SMEM),
                  pl.BlockSpec((tb, d_model), lambda i: (i, 0))],
        out_specs=pl.BlockSpec(memory_space=pl.ANY),
        out_shape=jax.ShapeDtypeStruct((n_slots, row_tiles, LANES), jnp.uint32),
        scratch_shapes=[pltpu.VMEM((tb, row_tiles, LANES), jnp.uint32), pltpu.SemaphoreType.DMA(())],
        compiler_params=_cparams(("arbitrary",)),
        name="dispatch",
    )(dest, x1)


def _expert_kernel(wt_ref, we_ref, lo_ref, hi_ref, xs_ref, wg_ref, wu_ref, wd_ref, os_ref, wgu_scr, wd_scr):
    w = pl.program_id(0)
    bm = os_ref.shape[0]
    row_tiles = xs_ref.shape[0] // bm
    f = wg_ref.shape[2]
    prev = jnp.maximum(w - 1, 0)
    lo = lo_ref[w]
    hi = hi_ref[w]
    live = hi > lo
    new_expert = jnp.logical_or(w == 0, we_ref[w] != we_ref[prev])
    new_tile = jnp.logical_or(w == 0, wt_ref[w] != wt_ref[prev])

    @pl.when(jnp.logical_and(live, new_expert))
    def _():
        wgu_scr[:, :f] = wg_ref[0].astype(BF16)
        wgu_scr[:, f:] = wu_ref[0].astype(BF16)
        wd_scr[...] = wd_ref[0].astype(BF16)

    @pl.when(live)
    def _():
        words = [xs_ref[pl.ds(j, bm, stride=row_tiles), :] for j in range(row_tiles)]
        x = jnp.concatenate([_unpack_lo(word).astype(BF16) for word in words]
                            + [_unpack_hi(word).astype(BF16) for word in words], axis=1)
        gu = jnp.dot(x, wgu_scr[...], preferred_element_type=F32)
        h = jax.nn.silu(gu[:, :f]) * gu[:, f:]
        out = jnp.dot(h.astype(BF16), wd_scr[...], preferred_element_type=F32)
        row = lax.broadcasted_iota(jnp.int32, out.shape, 0)
        mine = jnp.logical_and(row >= jnp.where(new_tile, 0, lo), row < jnp.where(new_tile, bm, hi))
        os_ref[...] = jnp.where(mine, out, os_ref[...])


def _experts(wt, we, lo, hi, xs, w_gate, w_up, w_down, bm):
    n_slots, row_tiles, _ = xs.shape
    d_model, f = w_gate.shape[1], w_gate.shape[2]
    xs2d = xs.reshape(n_slots * row_tiles, LANES)
    row_map = lambda w, wt, we, lo, hi: (wt[w], 0)
    w_map = lambda w, wt, we, lo, hi: (we[w], 0, 0)
    return pl.pallas_call(
        _expert_kernel,
        grid_spec=pltpu.PrefetchScalarGridSpec(
            num_scalar_prefetch=4,
            grid=(wt.shape[0],),
            in_specs=[pl.BlockSpec((bm * row_tiles, LANES), row_map), pl.BlockSpec((1, d_model, f), w_map),
                      pl.BlockSpec((1, d_model, f), w_map), pl.BlockSpec((1, f, d_model), w_map)],
            out_specs=pl.BlockSpec((bm, d_model), row_map),
            scratch_shapes=[pltpu.VMEM((d_model, 2 * f), BF16), pltpu.VMEM((f, d_model), BF16)],
        ),
        out_shape=jax.ShapeDtypeStruct((n_slots, d_model), F32),
        compiler_params=_cparams(("arbitrary",)),
        name="experts",
    )(wt, we, lo, hi, xs2d, w_gate, w_up, w_down)


def _work_list(counts, bm, n_tiles, n_work):
    n_exp = counts.shape[0]
    cend = jnp.cumsum(counts)
    cstart = cend - counts
    has = counts > 0
    first_tile = cstart // bm
    n_items = jnp.where(has, (cend - 1) // bm - first_tile + 1, 0)
    wend = jnp.cumsum(n_items)
    woff = wend - n_items
    total = wend[-1]
    w = jnp.arange(n_work, dtype=jnp.int32)
    valid = w < total
    w_eff = jnp.minimum(w, total - 1)
    we = jnp.minimum(jnp.sum((wend[None, :] <= w_eff[:, None]).astype(jnp.int32), axis=1), n_exp - 1)
    onehot = we[:, None] == jnp.arange(n_exp, dtype=jnp.int32)[None, :]
    pick = lambda v: jnp.sum(jnp.where(onehot, v[None, :], 0), axis=1)
    wt = pick(first_tile) + (w_eff - pick(woff))
    lo = jnp.where(valid, jnp.maximum(pick(cstart) - wt * bm, 0), 0)
    hi = jnp.where(valid, jnp.minimum(pick(cend) - wt * bm, bm), 0)
    i32 = lambda v: v.astype(jnp.int32)
    return i32(cstart), i32(wt), i32(we), i32(lo), i32(hi)


def _combine_kernel(dest_ref, x_ref, wts_ref, os_ref, wsgu_ref, wsd_ref, lng_ref, lnb_ref, x2_ref, gbuf, sem,
                    *, alpha):
    tb = x_ref.shape[0]
    f = wsd_ref.shape[0]

    def issue(t, carry):
        for k in range(TOP_K):
            pltpu.make_async_copy(os_ref.at[pl.ds(dest_ref[k, t], 1), :], gbuf.at[k, pl.ds(t, 1), :],
                                  sem).start(priority=k % 2)
        return carry

    lax.fori_loop(0, tb, issue, 0, unroll=2)

    x = x_ref[...]
    gu = jnp.dot(x.astype(BF16), wsgu_ref[...], preferred_element_type=F32)
    h = jax.nn.silu(gu[:, :f]) * gu[:, f:]
    shared = jnp.dot(h.astype(BF16), wsd_ref[...], preferred_element_type=F32)

    for k in range(TOP_K):
        pltpu.make_async_copy(os_ref.at[pl.ds(0, tb), :], gbuf.at[k], sem).wait()

    w = wts_ref[...]
    routed = gbuf[0] * w[:, 0:1]
    for k in range(1, TOP_K):
        routed = routed + gbuf[k] * w[:, k:k + 1]
    res = alpha * x + (routed + shared)
    mu = jnp.mean(res, axis=-1, keepdims=True)
    xc = res - mu
    var = jnp.mean(xc * xc, axis=-1, keepdims=True)
    x2_ref[...] = xc * lax.rsqrt(var + LN_EPS) * lng_ref[...] + lnb_ref[...]


def _combine(dest, x1, wts_nk, os_, wsgu, wsd, lng, lnb, alpha, tb):
    n, d_model = x1.shape
    return pl.pallas_call(
        functools.partial(_combine_kernel, alpha=alpha),
        grid=(n // tb,),
        in_specs=[pl.BlockSpec((TOP_K, tb), lambda i: (0, i), memory_space=pltpu.SMEM),
                  pl.BlockSpec((tb, d_model), lambda i: (i, 0)),
                  pl.BlockSpec((tb, TOP_K), lambda i: (i, 0)),
                  pl.BlockSpec(memory_space=pl.ANY),
                  _const_spec(wsgu.shape), _const_spec(wsd.shape), _const_spec(lng.shape), _const_spec(lnb.shape)],
        out_specs=pl.BlockSpec((tb, d_model), lambda i: (i, 0)),
        out_shape=jax.ShapeDtypeStruct((n, d_model), F32),
        scratch_shapes=[pltpu.VMEM((TOP_K, tb, d_model), F32), pltpu.SemaphoreType.DMA(())],
        compiler_params=_cparams(("arbitrary",)),
        name="combine",
    )(dest, x1, wts_nk, os_, wsgu, wsd, lng, lnb)


def _to_time_major(x):
    b, t, d = x.shape
    return x.reshape(b // SEQS_PER_GROUP, SEQS_PER_GROUP, t, d).transpose(0, 2, 1, 3).reshape(b * t, d)


def _from_time_major(x, b, t):
    d = x.shape[-1]
    return x.reshape(b // SEQS_PER_GROUP, t, SEQS_PER_GROUP, d).transpose(0, 2, 1, 3).reshape(b, t, d)


def _block_diag_in(bbt, n_gb):
    g, c, p = bbt.shape
    gl = g // n_gb
    eye = jnp.eye(gl, dtype=bbt.dtype)
    return jnp.einsum("jgcp,gh->jgchp", bbt.reshape(n_gb, gl, c, p), eye).reshape(n_gb, gl * c, gl * p)


def _block_diag_out(cm, n_gb):
    g, c, p = cm.shape
    gl = g // n_gb
    eye = jnp.eye(gl, dtype=cm.dtype)
    return jnp.einsum("jgcp,gh->jhpgc", cm.reshape(n_gb, gl, c, p), eye).reshape(n_gb, gl * p, gl * c)


def kernel(x_prompt, x_sample, cache_k, cache_v, state_ssm_re, state_ssm_im, rel_bias, w_in, attn_sinks, w_attn_proj, lam_re, lam_im, log_dt, b_re, b_im, c_re, c_im, d_skip, w_glu, w_out, ln1_g, ln1_b, w_router, b_router, w_gate, w_up, w_down, ws_gate, ws_up, ws_down, ln2_g, ln2_b):
    batch, seq, d_model = x_prompt.shape
    dec_batch, dec_seq, _ = x_sample.shape
    depth = w_in.shape[0]
    win = cache_k.shape[2]
    n_q_heads = attn_sinks.shape[1]
    q_w = n_q_heads * HEAD_DIM
    kv_w = N_KV_HEADS * HEAD_DIM
    d_ssm = d_skip.shape[1]
    n_groups_ssm, state_dim = lam_re.shape[1], lam_re.shape[2]
    gp = n_groups_ssm * state_dim
    n_exp = w_router.shape[2]
    alpha = (2 * depth) ** 0.25
    assert win == WINDOW and seq % CHUNK == 0 and dec_seq <= CHUNK
    assert batch % SEQS_PER_GROUP == 0 and dec_batch % SEQS_PER_GROUP == 0
    qch = (PAST_LEN + np.arange(dec_seq)) // CHUNK
    kch = np.concatenate([PAST_LEN - win + np.arange(win), PAST_LEN + np.arange(dec_seq)]) // CHUNK
    assert np.all((kch[None, :] >= qch[:, None] - WINDOW // CHUNK) & (kch[None, :] <= qch[:, None]))

    n_p = batch * seq
    n_s = dec_batch * dec_seq
    n_tok = n_p + n_s
    n_gb = n_groups_ssm // SSM_GROUPS_PER_BATCH

    bias = _rel_bias_table(rel_bias)
    a_re, a_im, bbt_re, bbt_im = _discretize(lam_re, lam_im, log_dt, b_re, b_im)

    x_all = jnp.concatenate([_to_time_major(x_prompt.astype(F32)), _to_time_major(x_sample.astype(F32))], axis=0)
    zeros_kv = jnp.zeros((batch, win, kv_w), F32)
    zeros_h = jnp.zeros((batch, 2 * gp), F32)

    tb_route = 512 if n_tok % 512 == 0 else 256
    tb_move = 256
    bm = 256
    n_slots = n_tok * TOP_K
    assert n_slots % bm == 0
    n_tiles = n_slots // bm
    n_work = n_tiles + n_exp - 1

    outs = {k: [] for k in ("kp", "vp", "hp", "ks", "vs", "hs")}
    for l in range(depth):
        wl = w_in[l].astype(BF16)
        lw = dict(
            wqkv=wl[:, :q_w + 2 * kv_w], wu=wl[:, q_w + 2 * kv_w:q_w + 2 * kv_w + d_ssm],
            wgl=wl[:, q_w + 2 * kv_w + d_ssm:], wap=w_attn_proj[l].astype(BF16),
            a=jnp.stack([a_re[l].reshape(gp), a_im[l].reshape(gp)]),
            wb=jnp.concatenate([_block_diag_in(bbt_re[l], n_gb), _block_diag_in(bbt_im[l], n_gb)], axis=2).astype(BF16),
            wc=jnp.concatenate([_block_diag_out(c_re[l].astype(F32), n_gb),
                                -_block_diag_out(c_im[l].astype(F32), n_gb)], axis=1).astype(BF16),
            dskip=d_skip[l].astype(F32).reshape(1, d_ssm), wglu=w_glu[l].astype(BF16), wout=w_out[l].astype(BF16),
            ln1g=ln1_g[l].astype(F32).reshape(1, d_model), ln1b=ln1_b[l].astype(F32).reshape(1, d_model),
        )
        x1, kp, vp, hp = _mixer(x_all, 0, batch // SEQS_PER_GROUP, seq // CHUNK, CHUNK, zeros_kv, zeros_kv, zeros_h,
                                lw, bias, attn_sinks[l], alpha, True)
        hinit = jnp.concatenate([state_ssm_re[l].astype(F32).reshape(dec_batch, gp),
                                 state_ssm_im[l].astype(F32).reshape(dec_batch, gp)], axis=1)
        x1, ks, vs, hs = _mixer(x1, n_p // (SEQS_PER_GROUP * dec_seq), dec_batch // SEQS_PER_GROUP, 1, dec_seq,
                                cache_k[l].astype(F32).reshape(dec_batch, win, kv_w),
                                cache_v[l].astype(F32).reshape(dec_batch, win, kv_w), hinit,
                                lw, bias, attn_sinks[l], alpha, False)
        for name, val in zip(("kp", "vp", "hp", "ks", "vs", "hs"), (kp, vp, hp, ks, vs, hs)):
            outs[name].append(val)

        wrt = w_router[l].astype(BF16).T
        brt = jnp.broadcast_to(b_router[l].astype(F32)[:, None], (n_exp, LANES))
        idx_t, wts_t, rank_t, cnt = _router(x1, wrt, brt, tb_route)
        cstart, wt, we, lo, hi = _work_list(cnt[:, 0].astype(jnp.int32), bm, n_tiles, n_work)
        dest = _dest(cstart, idx_t, rank_t, tb_route)
        xs = _dispatch(dest, x1, n_slots, tb_move)
        os_ = _experts(wt, we, lo, hi, xs, w_gate[l], w_up[l], w_down[l], bm)
        wsgu = jnp.concatenate([ws_gate[l], ws_up[l]], axis=1).astype(BF16)
        x_all = _combine(dest, x1, wts_t.T, os_, wsgu, ws_down[l].astype(BF16),
                         ln2_g[l].astype(F32).reshape(1, d_model), ln2_b[l].astype(F32).reshape(1, d_model),
                         alpha, tb_move)

    y_prompt = _from_time_major(x_all[:n_p], batch, seq)
    y_sample = _from_time_major(x_all[n_p:], dec_batch, dec_seq)

    def kv(vals, b):
        return jnp.stack(vals).reshape(depth, b, win, N_KV_HEADS, HEAD_DIM)

    def st(vals, b, part):
        return jnp.stack(vals)[:, :, part * gp:(part + 1) * gp].reshape(depth, b, n_groups_ssm, state_dim)

    return (y_prompt, y_sample, kv(outs["kp"], batch), kv(outs["vp"], batch), st(outs["hp"], batch, 0),
            st(outs["hp"], batch, 1), kv(outs["ks"], dec_batch), kv(outs["vs"], dec_batch),
            st(outs["hs"], dec_batch, 0), st(outs["hs"], dec_batch, 1))
```

```python
import functools
import math

import jax
import jax.numpy as jnp
import numpy as np
from jax import lax
from jax.experimental import pallas as pl
from jax.experimental.pallas import tpu as pltpu

CHUNK = 64
WINDOW = 128
HEAD_DIM = 64
N_KV_HEADS = 2
MAX_DISTANCE = 128
CH_PER_GROUP = 16
STATE_DIM = 64
TOP_K = 8
N_ROUTE_GROUPS = 8
TOPK_GROUPS = 4
ROUTED_SCALE = 2.5
LN_EPS = 1e-5
NEG_INF = -1e30
PAST_LEN = 1024

SUBLANES = 8
LANES = 128
SEQS_PER_GROUP = SUBLANES
SSM_GROUPS_PER_BATCH = LANES // CH_PER_GROUP
VMEM_LIMIT_BYTES = 60 * 1024 * 1024
ATTN_SEQS_PER_ITER = 8

BF16 = jnp.bfloat16
F32 = jnp.float32


def _cparams(sem):
    return pltpu.CompilerParams(dimension_semantics=sem, vmem_limit_bytes=VMEM_LIMIT_BYTES)


def _const_spec(shape):
    nd = len(shape)
    return pl.BlockSpec(shape, lambda *_: (0,) * nd, pipeline_mode=pl.Buffered(1))


def _bias_kernel(bucket_ref, tbl_ref, out_ref):
    n_buckets, n_heads = tbl_ref.shape
    bk = bucket_ref[...]
    for h in range(n_heads):
        acc = jnp.zeros(bk.shape, F32)
        for b in range(n_buckets):
            acc = jnp.where(bk == b, tbl_ref[b, h], acc)
        out_ref[h] = acc


def _rel_bias_table(rel_bias):
    n_buckets, n_heads = rel_bias.shape
    band = WINDOW + CHUNK
    rel = (jnp.arange(band) - WINDOW)[None, :] - jnp.arange(CHUNK)[:, None]
    nb = n_buckets // 2
    max_exact = nb // 2
    n = jnp.abs(rel)
    nf = jnp.maximum(n, 1).astype(F32)
    large = max_exact + (jnp.log(nf / max_exact) / math.log(MAX_DISTANCE / max_exact) * (nb - max_exact)).astype(jnp.int32)
    large = jnp.minimum(large, nb - 1)
    bucket = (jnp.where(rel > 0, nb, 0) + jnp.where(n < max_exact, n, large)).astype(jnp.int32)
    return pl.pallas_call(
        _bias_kernel,
        out_shape=jax.ShapeDtypeStruct((n_heads, CHUNK, band), F32),
        in_specs=[pl.BlockSpec(memory_space=pltpu.VMEM), pl.BlockSpec(memory_space=pltpu.SMEM)],
        out_specs=pl.BlockSpec(memory_space=pltpu.VMEM),
        name="rel_bias_table",
    )(bucket, rel_bias.astype(F32))


def _disc_kernel(lr_ref, li_ref, ldt_ref, br_ref, bi_ref, are_ref, aim_ref, bbr_ref, bbi_ref):
    lr = lr_ref[0]
    li = li_ref[0]
    dt = jnp.exp(ldt_ref[0])
    mag = jnp.exp(lr * dt)
    a_re = mag * jnp.cos(li * dt)
    a_im = mag * jnp.sin(li * dt)
    den = lr * lr + li * li
    f_re = ((a_re - 1.0) * lr + a_im * li) / den
    f_im = (a_im * lr - (a_re - 1.0) * li) / den
    br = br_ref[0]
    bi = bi_ref[0]
    are_ref[0] = a_re
    aim_ref[0] = a_im
    bbr_ref[0] = f_re * br - f_im * bi
    bbi_ref[0] = f_re * bi + f_im * br


def _discretize(lam_re, lam_im, log_dt, b_re, b_im):
    depth, g, p = lam_re.shape
    c = b_re.shape[-1]
    spec_gp = pl.BlockSpec((1, g, 1, p), lambda l: (l, 0, 0, 0))
    spec_g1 = pl.BlockSpec((1, g, 1, 1), lambda l: (l, 0, 0, 0))
    spec_gcp = pl.BlockSpec((1, g, c, p), lambda l: (l, 0, 0, 0))
    return pl.pallas_call(
        _disc_kernel,
        grid=(depth,),
        out_shape=(jax.ShapeDtypeStruct((depth, g, 1, p), F32), jax.ShapeDtypeStruct((depth, g, 1, p), F32),
                   jax.ShapeDtypeStruct((depth, g, c, p), F32), jax.ShapeDtypeStruct((depth, g, c, p), F32)),
        in_specs=[spec_gp, spec_gp, spec_g1, spec_gcp, spec_gcp],
        out_specs=(spec_gp, spec_gp, spec_gcp, spec_gcp),
        name="ssm_discretize",
    )(lam_re.astype(F32).reshape(depth, g, 1, p), lam_im.astype(F32).reshape(depth, g, 1, p),
      log_dt.astype(F32).reshape(depth, g, 1, 1),
      jnp.swapaxes(b_re.astype(F32), -1, -2), jnp.swapaxes(b_im.astype(F32), -1, -2))


def _mixer_kernel(x_ref, kinit_ref, vinit_ref, hinit_ref, wqkv_ref, wu_ref, wgl_ref, bias_ref,
                  wap_ref, a_ref, wb_ref, wc_ref, dskip_ref, wglu_ref, wout_ref, lng_ref, lnb_ref,
                  x1_ref, kout_ref, vout_ref, hout_ref,
                  qkv_scr, attn_scr, kwin, vwin, h_scr, bu_scr, u_scr,
                  *, t_len, alpha, mask_missing_chunks, row_block, seqs_per_iter):
    c = pl.program_id(1)
    n_c = pl.num_programs(1)
    rows, d_model = x_ref.shape
    win = kinit_ref.shape[1]
    key_pad = kwin.shape[1]
    q_slabs = attn_scr.shape[0]
    gq2 = q_slabs // N_KV_HEADS
    gp = a_ref.shape[1]
    d_ssm = u_scr.shape[1]
    n_gb = wb_ref.shape[0]
    gb_state = gp // n_gb
    scale = HEAD_DIM ** -0.5

    @pl.when(c == 0)
    def _():
        kwin[:, 0:win, :] = kinit_ref[...]
        vwin[:, 0:win, :] = vinit_ref[...]
        kwin[:, win + t_len:, :] = jnp.zeros((SEQS_PER_GROUP, key_pad - win - t_len, kwin.shape[2]), F32)
        vwin[:, win + t_len:, :] = jnp.zeros((SEQS_PER_GROUP, key_pad - win - t_len, vwin.shape[2]), F32)
        h_scr[...] = hinit_ref[...]

    xb = x_ref[...].astype(BF16)

    qkv = jnp.dot(xb, wqkv_ref[...], preferred_element_type=F32)
    for j in range(q_slabs + 2):
        qkv_scr[j] = qkv[:, j * LANES:(j + 1) * LANES]

    def append_kv(b, carry):
        seq_rows = pl.ds(b, t_len, stride=SEQS_PER_GROUP)
        kwin[b, win:win + t_len, :] = qkv_scr[q_slabs, seq_rows, :]
        vwin[b, win:win + t_len, :] = qkv_scr[q_slabs + 1, seq_rows, :]
        return carry

    lax.fori_loop(0, SEQS_PER_GROUP, append_kv, 0)

    def attend_group(i, carry):
        chains = [(i * seqs_per_iter + u, hk) for u in range(seqs_per_iter) for hk in range(N_KV_HEADS)]
        scores, values = [], []
        for b, hk in chains:
            seq_rows = pl.ds(b, t_len, stride=SEQS_PER_GROUP)
            parts = []
            for s in range(gq2):
                qs = qkv_scr[hk * gq2 + s, seq_rows, :]
                parts += [qs[:, :HEAD_DIM], qs[:, HEAD_DIM:]]
            q4 = jnp.concatenate(parts, axis=0).astype(BF16)
            kh = kwin[b, :, hk * HEAD_DIM:(hk + 1) * HEAD_DIM].astype(BF16)
            values.append(vwin[b, :, hk * HEAD_DIM:(hk + 1) * HEAD_DIM].astype(BF16))
            scores.append(lax.dot_general(q4, kh, (((1,), (1,)), ((), ())), preferred_element_type=F32))
        probs = []
        for (b, hk), s_ in zip(chains, scores):
            s_ = s_ * scale + bias_ref[hk]
            if mask_missing_chunks:
                key = lax.broadcasted_iota(jnp.int32, s_.shape, 1)
                first_valid = (WINDOW // CHUNK - jnp.minimum(c, WINDOW // CHUNK)) * CHUNK
                s_ = jnp.where(key >= first_valid, s_, NEG_INF)
            m = jnp.max(s_, axis=-1, keepdims=True)
            probs.append(jnp.exp(s_ - m).astype(BF16))
        ones = jnp.ones((key_pad, HEAD_DIM), BF16)
        outs = [jnp.dot(p, vh, preferred_element_type=F32) / jnp.dot(p, ones, preferred_element_type=F32)
                for p, vh in zip(probs, values)]
        for (b, hk), o in zip(chains, outs):
            for s in range(gq2):
                pair = jnp.concatenate([o[(2 * s) * t_len:(2 * s + 1) * t_len],
                                        o[(2 * s + 1) * t_len:(2 * s + 2) * t_len]], axis=1)
                attn_scr[hk * gq2 + s, pl.ds(b, t_len, stride=SEQS_PER_GROUP), :] = pair
        return carry

    lax.fori_loop(0, SEQS_PER_GROUP // seqs_per_iter, attend_group, 0)

    def slide_window(b, carry):
        knext = kwin[b, t_len:t_len + win, :]
        vnext = vwin[b, t_len:t_len + win, :]
        kwin[b, 0:win, :] = knext
        vwin[b, 0:win, :] = vnext
        return carry

    lax.fori_loop(0, SEQS_PER_GROUP, slide_window, 0)

    @pl.when(c == n_c - 1)
    def _():
        kout_ref[...] = kwin[:, 0:win, :]
        vout_ref[...] = vwin[:, 0:win, :]

    u = jnp.dot(xb, wu_ref[...], preferred_element_type=F32)
    u_scr[...] = u
    ch_b = d_ssm // n_gb
    for j in range(n_gb):
        bu = jnp.dot(u[:, j * ch_b:(j + 1) * ch_b].astype(BF16), wb_ref[j], preferred_element_type=F32)
        bu_scr[:, j * gb_state:(j + 1) * gb_state] = bu[:, :gb_state]
        bu_scr[:, gp + j * gb_state:gp + (j + 1) * gb_state] = bu[:, gb_state:]

    scan_lanes = 4 * LANES
    for lc in range(gp // scan_lanes):
        re_l = slice(lc * scan_lanes, (lc + 1) * scan_lanes)
        im_l = slice(gp + lc * scan_lanes, gp + (lc + 1) * scan_lanes)
        a_re = jnp.broadcast_to(a_ref[0:1, re_l], (SEQS_PER_GROUP, scan_lanes))
        a_im = jnp.broadcast_to(a_ref[1:2, re_l], (SEQS_PER_GROUP, scan_lanes))

        def step(t, h, re_l=re_l, im_l=im_l, a_re=a_re, a_im=a_im):
            h_re, h_im = h
            r0 = pl.multiple_of(t * SEQS_PER_GROUP, SEQS_PER_GROUP)
            n_re = a_re * h_re - a_im * h_im + bu_scr[pl.ds(r0, SEQS_PER_GROUP), re_l]
            n_im = a_re * h_im + a_im * h_re + bu_scr[pl.ds(r0, SEQS_PER_GROUP), im_l]
            bu_scr[pl.ds(r0, SEQS_PER_GROUP), re_l] = n_re
            bu_scr[pl.ds(r0, SEQS_PER_GROUP), im_l] = n_im
            return n_re, n_im

        h_re, h_im = lax.fori_loop(0, t_len, step, (h_scr[:, re_l], h_scr[:, im_l]), unroll=8)
        h_scr[:, re_l] = h_re
        h_scr[:, im_l] = h_im

    @pl.when(c == n_c - 1)
    def _():
        hout_ref[...] = h_scr[...]

    for rb in range(rows // row_block):
        r = slice(rb * row_block, (rb + 1) * row_block)
        ys = []
        for j in range(n_gb):
            hcat = jnp.concatenate([bu_scr[r, j * gb_state:(j + 1) * gb_state],
                                    bu_scr[r, gp + j * gb_state:gp + (j + 1) * gb_state]], axis=1)
            ys.append(jnp.dot(hcat.astype(BF16), wc_ref[j], preferred_element_type=F32))
        y = jnp.concatenate(ys, axis=1)
        z = jax.nn.gelu(y + dskip_ref[...] * u_scr[r, :])
        glu = jnp.dot(z.astype(BF16), wglu_ref[...], preferred_element_type=F32)
        br_b = glu[:, :d_model] * jax.nn.sigmoid(glu[:, d_model:])
        attn = jnp.concatenate([attn_scr[s, r, :] for s in range(q_slabs)], axis=1)
        br_a = jnp.dot(attn.astype(BF16), wap_ref[...], preferred_element_type=F32)
        xr = x_ref[r, :]
        gl = jnp.dot(xr.astype(BF16), wgl_ref[...], preferred_element_type=F32)
        mix = jax.nn.sigmoid(gl[:, :d_model]) * br_a + jax.nn.sigmoid(gl[:, d_model:]) * br_b
        res = alpha * xr + jnp.dot(mix.astype(BF16), wout_ref[...], preferred_element_type=F32)
        mu = jnp.mean(res, axis=-1, keepdims=True)
        xc = res - mu
        var = jnp.mean(xc * xc, axis=-1, keepdims=True)
        x1_ref[r, :] = xc * lax.rsqrt(var + LN_EPS) * lng_ref[...] + lnb_ref[...]


def _mixer(x_all, row_offset_blocks, n_groups, n_steps, t_len, kinit, vinit, hinit, lw, bias, sinks, alpha,
           mask_missing_chunks):
    n_total, d_model = x_all.shape
    rows = SEQS_PER_GROUP * t_len
    win = kinit.shape[1]
    kvw = kinit.shape[2]
    gp2 = hinit.shape[1]
    gp = gp2 // 2
    q_slabs = lw["wap"].shape[0] // LANES
    d_ssm = lw["wu"].shape[1]
    band = win + t_len
    gq = bias.shape[0] // N_KV_HEADS
    key_pad = -(-(band + 1) // LANES) * LANES
    sink_col = jnp.broadcast_to(sinks.astype(F32).reshape(N_KV_HEADS * gq, 1, 1), (N_KV_HEADS * gq, t_len, 1))
    dead_cols = jnp.full((N_KV_HEADS * gq, t_len, key_pad - band - 1), NEG_INF, F32)
    bias_s = jnp.concatenate([bias[:, :t_len, :band], sink_col, dead_cols], axis=2)
    bias_s = bias_s.reshape(N_KV_HEADS, gq * t_len, key_pad)
    row_block = min(rows, 256)

    def xmap(g, c):
        return (row_offset_blocks + g * n_steps + c, 0)

    grp3 = lambda g, c: (g, 0, 0)
    grp2 = lambda g, c: (g, 0)
    in_specs = [
        pl.BlockSpec((rows, d_model), xmap),
        pl.BlockSpec((SEQS_PER_GROUP, win, kvw), grp3),
        pl.BlockSpec((SEQS_PER_GROUP, win, kvw), grp3),
        pl.BlockSpec((SEQS_PER_GROUP, gp2), grp2),
        _const_spec(lw["wqkv"].shape), _const_spec(lw["wu"].shape), _const_spec(lw["wgl"].shape),
        _const_spec(bias_s.shape), _const_spec(lw["wap"].shape),
        _const_spec(lw["a"].shape), _const_spec(lw["wb"].shape), _const_spec(lw["wc"].shape),
        _const_spec(lw["dskip"].shape), _const_spec(lw["wglu"].shape), _const_spec(lw["wout"].shape),
        _const_spec(lw["ln1g"].shape), _const_spec(lw["ln1b"].shape),
    ]
    out_specs = (
        pl.BlockSpec((rows, d_model), xmap),
        pl.BlockSpec((SEQS_PER_GROUP, win, kvw), grp3),
        pl.BlockSpec((SEQS_PER_GROUP, win, kvw), grp3),
        pl.BlockSpec((SEQS_PER_GROUP, gp2), grp2),
    )
    n_seq = n_groups * SEQS_PER_GROUP
    out_shape = (
        jax.ShapeDtypeStruct((n_total, d_model), F32),
        jax.ShapeDtypeStruct((n_seq, win, kvw), F32),
        jax.ShapeDtypeStruct((n_seq, win, kvw), F32),
        jax.ShapeDtypeStruct((n_seq, gp2), F32),
    )
    scratch = [
        pltpu.VMEM((q_slabs + 2, rows, LANES), F32),
        pltpu.VMEM((q_slabs, rows, LANES), F32),
        pltpu.VMEM((SEQS_PER_GROUP, key_pad, kvw), F32),
        pltpu.VMEM((SEQS_PER_GROUP, key_pad, kvw), F32),
        pltpu.VMEM((SEQS_PER_GROUP, gp2), F32),
        pltpu.VMEM((rows, gp2), F32),
        pltpu.VMEM((rows, d_ssm), F32),
    ]
    kern = functools.partial(_mixer_kernel, t_len=t_len, alpha=alpha,
                             mask_missing_chunks=mask_missing_chunks, row_block=row_block,
                             seqs_per_iter=ATTN_SEQS_PER_ITER)
    args = [x_all, kinit, vinit, hinit, lw["wqkv"], lw["wu"], lw["wgl"], bias_s, lw["wap"], lw["a"],
            lw["wb"], lw["wc"], lw["dskip"], lw["wglu"], lw["wout"], lw["ln1g"], lw["ln1b"]]
    return pl.pallas_call(
        kern,
        grid=(n_groups, n_steps),
        in_specs=in_specs,
        out_specs=out_specs,
        out_shape=out_shape,
        scratch_shapes=scratch,
        input_output_aliases={0: 0},
        compiler_params=_cparams(("arbitrary", "arbitrary")),
        name="mixer_t%d" % t_len,
    )(*args)


def _router_kernel(x_ref, wrt_ref, brt_ref, upper_ref, idx_ref, wts_ref, rank_ref, cnt_ref, base_scr):
    i = pl.program_id(0)
    n_exp = wrt_ref.shape[0]
    tb = x_ref.shape[0]
    per_group = n_exp // N_ROUTE_GROUPS
    lane_rep = tb // LANES

    @pl.when(i == 0)
    def _():
        base_scr[...] = jnp.zeros_like(base_scr)

    xb = x_ref[...].astype(BF16)
    logits = lax.dot_general(wrt_ref[...], xb, (((1,), (1,)), ((), ())), preferred_element_type=F32)
    scores = jax.nn.sigmoid(logits)
    biased = scores + jnp.concatenate([brt_ref[...]] * lane_rep, axis=1)
    neg = jnp.float32(-jnp.inf)
    big = jnp.float32(2 ** 20)

    gs = []
    member = lax.broadcasted_iota(jnp.int32, (per_group, tb), 0).astype(F32)
    for g in range(N_ROUTE_GROUPS):
        bg = biased[g * per_group:(g + 1) * per_group, :]
        m1 = jnp.max(bg, axis=0, keepdims=True)
        first = jnp.min(jnp.where(bg == m1, member, big), axis=0, keepdims=True)
        m2 = jnp.max(jnp.where(member == first, neg, bg), axis=0, keepdims=True)
        gs.append(m1 + m2)

    keep = [jnp.zeros((1, tb), F32) for _ in range(N_ROUTE_GROUPS)]
    for _ in range(TOPK_GROUPS):
        best = gs[0]
        for g in range(1, N_ROUTE_GROUPS):
            best = jnp.maximum(best, gs[g])
        taken = jnp.zeros((1, tb), F32)
        for g in range(N_ROUTE_GROUPS):
            pick = jnp.where(gs[g] == best, 1.0 - taken, 0.0)
            taken = taken + pick
            keep[g] = keep[g] + pick
            gs[g] = jnp.where(pick > 0.5, neg, gs[g])

    vals = jnp.concatenate(
        [jnp.where(jnp.broadcast_to(keep[g], (per_group, tb)) > 0.5,
                   biased[g * per_group:(g + 1) * per_group, :], NEG_INF) for g in range(N_ROUTE_GROUPS)],
        axis=0)
    eid = lax.broadcasted_iota(jnp.int32, (n_exp, tb), 0).astype(F32)
    onehot = jnp.zeros((n_exp, tb), F32)
    sel_idx = []
    sel_score = []
    for _ in range(TOP_K):
        m = jnp.max(vals, axis=0, keepdims=True)
        first = jnp.min(jnp.where(vals == m, eid, big), axis=0, keepdims=True)
        sel = eid == first
        sel_idx.append(first)
        sel_score.append(jnp.sum(jnp.where(sel, scores, 0.0), axis=0, keepdims=True))
        vals = jnp.where(sel, neg, vals)
        onehot = jnp.where(sel, 1.0, onehot)

    total = sel_score[0]
    for k in range(1, TOP_K):
        total = total + sel_score[k]

    prefix = jnp.dot(onehot.astype(BF16), upper_ref[...], preferred_element_type=F32)
    pos = prefix + jnp.concatenate([base_scr[...]] * lane_rep, axis=1)
    for k in range(TOP_K):
        idx_ref[k:k + 1, :] = sel_idx[k].astype(jnp.int32)
        wts_ref[k:k + 1, :] = sel_score[k] / total * ROUTED_SCALE
        rk = jnp.sum(jnp.where(eid == sel_idx[k], pos, 0.0), axis=0, keepdims=True)
        rank_ref[k:k + 1, :] = rk.astype(jnp.int32)
    base_scr[...] = base_scr[...] + jnp.broadcast_to(jnp.sum(onehot, axis=1, keepdims=True), base_scr.shape)
    cnt_ref[...] = base_scr[...]


def _router(x1, wrt, brt, tb):
    n, d_model = x1.shape
    n_exp = wrt.shape[0]
    upper = jnp.triu(jnp.ones((tb, tb), BF16), k=1)
    tok = lambda i: (0, i)
    return pl.pallas_call(
        _router_kernel,
        grid=(n // tb,),
        in_specs=[pl.BlockSpec((tb, d_model), lambda i: (i, 0)), _const_spec(wrt.shape), _const_spec(brt.shape),
                  _const_spec(upper.shape)],
        out_specs=(pl.BlockSpec((TOP_K, tb), tok), pl.BlockSpec((TOP_K, tb), tok), pl.BlockSpec((TOP_K, tb), tok),
                   pl.BlockSpec((n_exp, LANES), lambda i: (0, 0))),
        out_shape=(jax.ShapeDtypeStruct((TOP_K, n), jnp.int32), jax.ShapeDtypeStruct((TOP_K, n), F32),
                   jax.ShapeDtypeStruct((TOP_K, n), jnp.int32), jax.ShapeDtypeStruct((n_exp, LANES), F32)),
        scratch_shapes=[pltpu.VMEM((n_exp, LANES), F32)],
        compiler_params=_cparams(("arbitrary",)),
        name="router",
    )(x1, wrt, brt, upper)


def _pack_bf16_pairs(x):
    h = x.shape[1] // 2
    xb = x.astype(BF16).astype(F32)
    lo = lax.bitcast_convert_type(xb[:, :h], jnp.uint32) >> 16
    return lo | (lax.bitcast_convert_type(xb[:, h:], jnp.uint32) & jnp.uint32(0xFFFF0000))


def _unpack_lo(w):
    return lax.bitcast_convert_type(w << 16, F32)


def _unpack_hi(w):
    return lax.bitcast_convert_type(w & jnp.uint32(0xFFFF0000), F32)


def _dest_kernel(cstart_ref, idx_ref, rank_ref, dest_ref):
    idx = idx_ref[...]
    rank = rank_ref[...]

    def body(e, acc):
        return jnp.where(idx == e, cstart_ref[e] + rank, acc)

    dest_ref[...] = lax.fori_loop(0, cstart_ref.shape[0], body, jnp.zeros_like(rank), unroll=8)


def _dest(cstart, idx_t, rank_t, tb):
    n = idx_t.shape[1]
    tok = lambda i: (0, i)
    return pl.pallas_call(
        _dest_kernel,
        grid=(n // tb,),
        in_specs=[pl.BlockSpec(memory_space=pltpu.SMEM), pl.BlockSpec((TOP_K, tb), tok),
                  pl.BlockSpec((TOP_K, tb), tok)],
        out_specs=pl.BlockSpec((TOP_K, tb), tok),
        out_shape=jax.ShapeDtypeStruct((TOP_K, n), jnp.int32),
        compiler_params=_cparams(("arbitrary",)),
        name="slot_index",
    )(cstart, idx_t, rank_t)


def _dispatch_kernel(dest_ref, x_ref, xs_ref, buf, sem):
    tb = x_ref.shape[0]
    words = _pack_bf16_pairs(x_ref[...])
    for j in range(buf.shape[1]):
        buf[:, j, :] = words[:, j * LANES:(j + 1) * LANES]

    def issue(t, carry):
        for k in range(TOP_K):
            pltpu.make_async_copy(buf.at[pl.ds(t, 1)], xs_ref.at[pl.ds(dest_ref[k, t], 1)], sem).start(priority=k % 2)
        return carry

    lax.fori_loop(0, tb, issue, 0, unroll=2)
    for k in range(TOP_K):
        pltpu.make_async_copy(buf, xs_ref.at[pl.ds(0, tb)], sem).wait()


def _dispatch(dest, x1, n_slots, tb):
    n, d_model = x1.shape
    row_tiles = d_model // 2 // LANES
    return pl.pallas_call(
        _dispatch_kernel,
        grid=(n // tb,),
        in_specs=[pl.BlockSpec((TOP_K, tb), lambda i: (0, i), memory_space=pltpu.SMEM),
                  pl.BlockSpec((tb, d_model), lambda i: (i, 0))],
        out_specs=pl.BlockSpec(memory_space=pl.ANY),
        out_shape=jax.ShapeDtypeStruct((n_slots, row_tiles, LANES), jnp.uint32),
        scratch_shapes=[pltpu.VMEM((tb, row_tiles, LANES), jnp.uint32), pltpu.SemaphoreType.DMA(())],
        compiler_params=_cparams(("arbitrary",)),
        name="dispatch",
    )(dest, x1)


def _expert_kernel(wt_ref, we_ref, lo_ref, hi_ref, xs_ref, wg_ref, wu_ref, wd_ref, os_ref, wgu_scr, wd_scr, *, bm):
    w = pl.program_id(0)
    row_tiles = xs_ref.shape[0] // bm
    f = wg_ref.shape[2]
    prev = jnp.maximum(w - 1, 0)
    lo = lo_ref[w]
    hi = hi_ref[w]
    live = hi > lo
    new_expert = jnp.logical_or(w == 0, we_ref[w] != we_ref[prev])
    new_tile = jnp.logical_or(w == 0, wt_ref[w] != wt_ref[prev])

    @pl.when(jnp.logical_and(live, new_expert))
    def _():
        wgu_scr[:, :f] = wg_ref[0].astype(BF16)
        wgu_scr[:, f:] = wu_ref[0].astype(BF16)
        wd_scr[...] = wd_ref[0].astype(BF16)

    @pl.when(live)
    def _():
        words = [xs_ref[pl.ds(j, bm, stride=row_tiles), :] for j in range(row_tiles)]
        x = jnp.concatenate([_unpack_lo(word).astype(BF16) for word in words]
                            + [_unpack_hi(word).astype(BF16) for word in words], axis=1)
        gu = jnp.dot(x, wgu_scr[...], preferred_element_type=F32)
        h = jax.nn.silu(gu[:, :f]) * gu[:, f:]
        out = jnp.dot(h.astype(BF16), wd_scr[...], preferred_element_type=F32)
        out_tiles = os_ref.shape[0] // bm

        @pl.when(new_tile)
        def _():
            for j in range(out_tiles):
                os_ref[pl.ds(j, bm, stride=out_tiles), :] = out[:, j * LANES:(j + 1) * LANES]

        @pl.when(jnp.logical_not(new_tile))
        def _():
            row = lax.broadcasted_iota(jnp.int32, (bm, LANES), 0)
            mine = jnp.logical_and(row >= lo, row < hi)
            for j in range(out_tiles):
                rows_j = pl.ds(j, bm, stride=out_tiles)
                os_ref[rows_j, :] = jnp.where(mine, out[:, j * LANES:(j + 1) * LANES], os_ref[rows_j, :])


def _experts(wt, we, lo, hi, xs, w_gate, w_up, w_down, bm):
    n_slots, row_tiles, _ = xs.shape
    d_model, f = w_gate.shape[1], w_gate.shape[2]
    xs2d = xs.reshape(n_slots * row_tiles, LANES)
    row_map = lambda w, wt, we, lo, hi: (wt[w], 0)
    w_map = lambda w, wt, we, lo, hi: (we[w], 0, 0)
    return pl.pallas_call(
        functools.partial(_expert_kernel, bm=bm),
        grid_spec=pltpu.PrefetchScalarGridSpec(
            num_scalar_prefetch=4,
            grid=(wt.shape[0],),
            in_specs=[pl.BlockSpec((bm * row_tiles, LANES), row_map), pl.BlockSpec((1, d_model, f), w_map),
                      pl.BlockSpec((1, d_model, f), w_map), pl.BlockSpec((1, f, d_model), w_map)],
            out_specs=pl.BlockSpec((bm * (d_model // LANES), LANES), row_map),
            scratch_shapes=[pltpu.VMEM((d_model, 2 * f), BF16), pltpu.VMEM((f, d_model), BF16)],
        ),
        out_shape=jax.ShapeDtypeStruct((n_slots * (d_model // LANES), LANES), F32),
        compiler_params=_cparams(("arbitrary",)),
        name="experts",
    )(wt, we, lo, hi, xs2d, w_gate, w_up, w_down)


def _work_list(counts, bm, n_tiles, n_work):
    n_exp = counts.shape[0]
    cend = jnp.cumsum(counts)
    cstart = cend - counts
    has = counts > 0
    first_tile = cstart // bm
    n_items = jnp.where(has, (cend - 1) // bm - first_tile + 1, 0)
    wend = jnp.cumsum(n_items)
    woff = wend - n_items
    total = wend[-1]
    w = jnp.arange(n_work, dtype=jnp.int32)
    valid = w < total
    w_eff = jnp.minimum(w, total - 1)
    we = jnp.minimum(jnp.sum((wend[None, :] <= w_eff[:, None]).astype(jnp.int32), axis=1), n_exp - 1)
    onehot = we[:, None] == jnp.arange(n_exp, dtype=jnp.int32)[None, :]
    pick = lambda v: jnp.sum(jnp.where(onehot, v[None, :], 0), axis=1)
    wt = pick(first_tile) + (w_eff - pick(woff))
    lo = jnp.where(valid, jnp.maximum(pick(cstart) - wt * bm, 0), 0)
    hi = jnp.where(valid, jnp.minimum(pick(cend) - wt * bm, bm), 0)
    i32 = lambda v: v.astype(jnp.int32)
    return i32(cstart), i32(wt), i32(we), i32(lo), i32(hi)


def _combine_kernel(dest_ref, x_ref, wts_ref, os_ref, wsgu_ref, wsd_ref, lng_ref, lnb_ref, x2_ref, gbuf, sem,
                    *, alpha):
    tb = x_ref.shape[0]
    f = wsd_ref.shape[0]

    rt = gbuf.shape[1] // tb

    def issue(t, carry):
        dst_rows = pl.ds(pl.multiple_of(t * rt, rt), rt)
        for k in range(TOP_K):
            src_rows = pl.ds(pl.multiple_of(dest_ref[k, t] * rt, rt), rt)
            pltpu.make_async_copy(os_ref.at[src_rows, :], gbuf.at[k, dst_rows, :], sem).start(priority=k % 2)
        return carry

    lax.fori_loop(0, tb, issue, 0, unroll=2)

    x = x_ref[...]
    gu = jnp.dot(x.astype(BF16), wsgu_ref[...], preferred_element_type=F32)
    h = jax.nn.silu(gu[:, :f]) * gu[:, f:]
    shared = jnp.dot(h.astype(BF16), wsd_ref[...], preferred_element_type=F32)

    for k in range(TOP_K):
        pltpu.make_async_copy(os_ref.at[pl.ds(0, tb * rt), :], gbuf.at[k], sem).wait()

    w = wts_ref[...]
    rb_rows = 4 * SUBLANES
    blocks = []
    for rb in range(tb // rb_rows):
        acc = [None] * rt
        for k in range(TOP_K):
            wk = jnp.broadcast_to(w[rb * rb_rows:(rb + 1) * rb_rows, k:k + 1], (rb_rows, LANES))
            for j in range(rt):
                piece = gbuf[k, pl.ds(rb * rb_rows * rt + j, rb_rows, stride=rt), :] * wk
                acc[j] = piece if k == 0 else acc[j] + piece
        blocks.append(jnp.concatenate(acc, axis=1))
    routed = jnp.concatenate(blocks, axis=0)
    res = alpha * x + (routed + shared)
    mu = jnp.mean(res, axis=-1, keepdims=True)
    xc = res - mu
    var = jnp.mean(xc * xc, axis=-1, keepdims=True)
    x2_ref[...] = xc * lax.rsqrt(var + LN_EPS) * lng_ref[...] + lnb_ref[...]


def _combine(dest, x1, wts_nk, os_, wsgu, wsd, lng, lnb, alpha, tb):
    n, d_model = x1.shape
    return pl.pallas_call(
        functools.partial(_combine_kernel, alpha=alpha),
        grid=(n // tb,),
        in_specs=[pl.BlockSpec((TOP_K, tb), lambda i: (0, i), memory_space=pltpu.SMEM),
                  pl.BlockSpec((tb, d_model), lambda i: (i, 0)),
                  pl.BlockSpec((tb, TOP_K), lambda i: (i, 0)),
                  pl.BlockSpec(memory_space=pl.ANY),
                  _const_spec(wsgu.shape), _const_spec(wsd.shape), _const_spec(lng.shape), _const_spec(lnb.shape)],
        out_specs=pl.BlockSpec((tb, d_model), lambda i: (i, 0)),
        out_shape=jax.ShapeDtypeStruct((n, d_model), F32),
        scratch_shapes=[pltpu.VMEM((TOP_K, tb * (d_model // LANES), LANES), F32), pltpu.SemaphoreType.DMA(())],
        compiler_params=_cparams(("arbitrary",)),
        name="combine",
    )(dest, x1, wts_nk, os_, wsgu, wsd, lng, lnb)


def _to_time_major(x):
    b, t, d = x.shape
    return x.reshape(b // SEQS_PER_GROUP, SEQS_PER_GROUP, t, d).transpose(0, 2, 1, 3).reshape(b * t, d)


def _from_time_major(x, b, t):
    d = x.shape[-1]
    return x.reshape(b // SEQS_PER_GROUP, t, SEQS_PER_GROUP, d).transpose(0, 2, 1, 3).reshape(b, t, d)


def _block_diag_in(bbt, n_gb):
    g, c, p = bbt.shape
    gl = g // n_gb
    eye = jnp.eye(gl, dtype=bbt.dtype)
    return jnp.einsum("jgcp,gh->jgchp", bbt.reshape(n_gb, gl, c, p), eye).reshape(n_gb, gl * c, gl * p)


def _block_diag_out(cm, n_gb):
    g, c, p = cm.shape
    gl = g // n_gb
    eye = jnp.eye(gl, dtype=cm.dtype)
    return jnp.einsum("jgcp,gh->jhpgc", cm.reshape(n_gb, gl, c, p), eye).reshape(n_gb, gl * p, gl * c)


def kernel(x_prompt, x_sample, cache_k, cache_v, state_ssm_re, state_ssm_im, rel_bias, w_in, attn_sinks, w_attn_proj, lam_re, lam_im, log_dt, b_re, b_im, c_re, c_im, d_skip, w_glu, w_out, ln1_g, ln1_b, w_router, b_router, w_gate, w_up, w_down, ws_gate, ws_up, ws_down, ln2_g, ln2_b):
    batch, seq, d_model = x_prompt.shape
    dec_batch, dec_seq, _ = x_sample.shape
    depth = w_in.shape[0]
    win = cache_k.shape[2]
    n_q_heads = attn_sinks.shape[1]
    q_w = n_q_heads * HEAD_DIM
    kv_w = N_KV_HEADS * HEAD_DIM
    d_ssm = d_skip.shape[1]
    n_groups_ssm, state_dim = lam_re.shape[1], lam_re.shape[2]
    gp = n_groups_ssm * state_dim
    n_exp = w_router.shape[2]
    alpha = (2 * depth) ** 0.25
    assert win == WINDOW and seq % CHUNK == 0 and dec_seq <= CHUNK
    assert batch % SEQS_PER_GROUP == 0 and dec_batch % SEQS_PER_GROUP == 0
    qch = (PAST_LEN + np.arange(dec_seq)) // CHUNK
    kch = np.concatenate([PAST_LEN - win + np.arange(win), PAST_LEN + np.arange(dec_seq)]) // CHUNK
    assert np.all((kch[None, :] >= qch[:, None] - WINDOW // CHUNK) & (kch[None, :] <= qch[:, None]))

    n_p = batch * seq
    n_s = dec_batch * dec_seq
    n_tok = n_p + n_s
    n_gb = n_groups_ssm // SSM_GROUPS_PER_BATCH

    bias = _rel_bias_table(rel_bias)
    a_re, a_im, bbt_re, bbt_im = _discretize(lam_re, lam_im, log_dt, b_re, b_im)

    x_all = jnp.concatenate([_to_time_major(x_prompt.astype(F32)), _to_time_major(x_sample.astype(F32))], axis=0)
    zeros_kv = jnp.zeros((batch, win, kv_w), F32)
    zeros_h = jnp.zeros((batch, 2 * gp), F32)

    tb_route = 512 if n_tok % 512 == 0 else 256
    tb_move = 256
    bm = 512
    n_slots = n_tok * TOP_K
    assert n_slots % bm == 0
    n_tiles = n_slots // bm
    n_work = n_tiles + n_exp - 1

    outs = {k: [] for k in ("kp", "vp", "hp", "ks", "vs", "hs")}
    for l in range(depth):
        wl = w_in[l].astype(BF16)
        lw = dict(
            wqkv=wl[:, :q_w + 2 * kv_w], wu=wl[:, q_w + 2 * kv_w:q_w + 2 * kv_w + d_ssm],
            wgl=wl[:, q_w + 2 * kv_w + d_ssm:], wap=w_attn_proj[l].astype(BF16),
            a=jnp.stack([a_re[l].reshape(gp), a_im[l].reshape(gp)]),
            wb=jnp.concatenate([_block_diag_in(bbt_re[l], n_gb), _block_diag_in(bbt_im[l], n_gb)], axis=2).astype(BF16),
            wc=jnp.concatenate([_block_diag_out(c_re[l].astype(F32), n_gb),
                                -_block_diag_out(c_im[l].astype(F32), n_gb)], axis=1).astype(BF16),
            dskip=d_skip[l].astype(F32).reshape(1, d_ssm), wglu=w_glu[l].astype(BF16), wout=w_out[l].astype(BF16),
            ln1g=ln1_g[l].astype(F32).reshape(1, d_model), ln1b=ln1_b[l].astype(F32).reshape(1, d_model),
        )
        x1, kp, vp, hp = _mixer(x_all, 0, batch // SEQS_PER_GROUP, seq // CHUNK, CHUNK, zeros_kv, zeros_kv, zeros_h,
                                lw, bias, attn_sinks[l], alpha, True)
        hinit = jnp.concatenate([state_ssm_re[l].astype(F32).reshape(dec_batch, gp),
                                 state_ssm_im[l].astype(F32).reshape(dec_batch, gp)], axis=1)
        x1, ks, vs, hs = _mixer(x1, n_p // (SEQS_PER_GROUP * dec_seq), dec_batch // SEQS_PER_GROUP, 1, dec_seq,
                                cache_k[l].astype(F32).reshape(dec_batch, win, kv_w),
                                cache_v[l].astype(F32).reshape(dec_batch, win, kv_w), hinit,
                                lw, bias, attn_sinks[l], alpha, False)
        for name, val in zip(("kp", "vp", "hp", "ks", "vs", "hs"), (kp, vp, hp, ks, vs, hs)):
            outs[name].append(val)

        wrt = w_router[l].astype(BF16).T
        brt = jnp.broadcast_to(b_router[l].astype(F32)[:, None], (n_exp, LANES))
        idx_t, wts_t, rank_t, cnt = _router(x1, wrt, brt, tb_route)
        cstart, wt, we, lo, hi = _work_list(cnt[:, 0].astype(jnp.int32), bm, n_tiles, n_work)
        dest = _dest(cstart, idx_t, rank_t, tb_route)
        xs = _dispatch(dest, x1, n_slots, tb_move)
        os_ = _experts(wt, we, lo, hi, xs, w_gate[l], w_up[l], w_down[l], bm)
        wsgu = jnp.concatenate([ws_gate[l], ws_up[l]], axis=1).astype(BF16)
        x_all = _combine(dest, x1, wts_t.T, os_, wsgu, ws_down[l].astype(BF16),
                         ln2_g[l].astype(F32).reshape(1, d_model), ln2_b[l].astype(F32).reshape(1, d_model),
                         alpha, tb_move)

    y_prompt = _from_time_major(x_all[:n_p], batch, seq)
    y_sample = _from_time_major(x_all[n_p:], dec_batch, dec_seq)

    def kv(vals, b):
        return jnp.stack(vals).reshape(depth, b, win, N_KV_HEADS, HEAD_DIM)

    def st(vals, b, part):
        return jnp.stack(vals)[:, :, part * gp:(part + 1) * gp].reshape(depth, b, n_groups_ssm, state_dim)

    return (y_prompt, y_sample, kv(outs["kp"], batch), kv(outs["vp"], batch), st(outs["hp"], batch, 0),
            st(outs["hp"], batch, 1), kv(outs["ks"], dec_batch), kv(outs["vs"], dec_batch),
            st(outs["hs"], dec_batch, 0), st(outs["hs"], dec_batch, 1))
```

```python
import functools
import math

import jax
import jax.numpy as jnp
import numpy as np
from jax import lax
from jax.experimental import pallas as pl
from jax.experimental.pallas import tpu as pltpu

CHUNK = 64
WINDOW = 128
HEAD_DIM = 64
N_KV_HEADS = 2
MAX_DISTANCE = 128
CH_PER_GROUP = 16
STATE_DIM = 64
TOP_K = 8
N_ROUTE_GROUPS = 8
TOPK_GROUPS = 4
ROUTED_SCALE = 2.5
LN_EPS = 1e-5
NEG_INF = -1e30
PAST_LEN = 1024

SUBLANES = 8
LANES = 128
SEQS_PER_GROUP = SUBLANES
SSM_GROUPS_PER_BATCH = LANES // CH_PER_GROUP
VMEM_LIMIT_BYTES = 60 * 1024 * 1024
ATTN_SEQS_PER_ITER = 8

BF16 = jnp.bfloat16
F32 = jnp.float32


def _cparams(sem):
    return pltpu.CompilerParams(dimension_semantics=sem, vmem_limit_bytes=VMEM_LIMIT_BYTES)


def _const_spec(shape):
    nd = len(shape)
    return pl.BlockSpec(shape, lambda *_: (0,) * nd, pipeline_mode=pl.Buffered(1))


def _bias_kernel(bucket_ref, tbl_ref, out_ref):
    n_buckets, n_heads = tbl_ref.shape
    bk = bucket_ref[...]
    for h in range(n_heads):
        acc = jnp.zeros(bk.shape, F32)
        for b in range(n_buckets):
            acc = jnp.where(bk == b, tbl_ref[b, h], acc)
        out_ref[h] = acc


def _rel_bias_table(rel_bias):
    n_buckets, n_heads = rel_bias.shape
    band = WINDOW + CHUNK
    rel = (jnp.arange(band) - WINDOW)[None, :] - jnp.arange(CHUNK)[:, None]
    nb = n_buckets // 2
    max_exact = nb // 2
    n = jnp.abs(rel)
    nf = jnp.maximum(n, 1).astype(F32)
    large = max_exact + (jnp.log(nf / max_exact) / math.log(MAX_DISTANCE / max_exact) * (nb - max_exact)).astype(jnp.int32)
    large = jnp.minimum(large, nb - 1)
    bucket = (jnp.where(rel > 0, nb, 0) + jnp.where(n < max_exact, n, large)).astype(jnp.int32)
    return pl.pallas_call(
        _bias_kernel,
        out_shape=jax.ShapeDtypeStruct((n_heads, CHUNK, band), F32),
        in_specs=[pl.BlockSpec(memory_space=pltpu.VMEM), pl.BlockSpec(memory_space=pltpu.SMEM)],
        out_specs=pl.BlockSpec(memory_space=pltpu.VMEM),
        name="rel_bias_table",
    )(bucket, rel_bias.astype(F32))


def _disc_kernel(lr_ref, li_ref, ldt_ref, br_ref, bi_ref, are_ref, aim_ref, bbr_ref, bbi_ref):
    lr = lr_ref[0]
    li = li_ref[0]
    dt = jnp.exp(ldt_ref[0])
    mag = jnp.exp(lr * dt)
    a_re = mag * jnp.cos(li * dt)
    a_im = mag * jnp.sin(li * dt)
    den = lr * lr + li * li
    f_re = ((a_re - 1.0) * lr + a_im * li) / den
    f_im = (a_im * lr - (a_re - 1.0) * li) / den
    br = br_ref[0]
    bi = bi_ref[0]
    are_ref[0] = a_re
    aim_ref[0] = a_im
    bbr_ref[0] = f_re * br - f_im * bi
    bbi_ref[0] = f_re * bi + f_im * br


def _discretize(lam_re, lam_im, log_dt, b_re, b_im):
    depth, g, p = lam_re.shape
    c = b_re.shape[-1]
    spec_gp = pl.BlockSpec((1, g, 1, p), lambda l: (l, 0, 0, 0))
    spec_g1 = pl.BlockSpec((1, g, 1, 1), lambda l: (l, 0, 0, 0))
    spec_gcp = pl.BlockSpec((1, g, c, p), lambda l: (l, 0, 0, 0))
    return pl.pallas_call(
        _disc_kernel,
        grid=(depth,),
        out_shape=(jax.ShapeDtypeStruct((depth, g, 1, p), F32), jax.ShapeDtypeStruct((depth, g, 1, p), F32),
                   jax.ShapeDtypeStruct((depth, g, c, p), F32), jax.ShapeDtypeStruct((depth, g, c, p), F32)),
        in_specs=[spec_gp, spec_gp, spec_g1, spec_gcp, spec_gcp],
        out_specs=(spec_gp, spec_gp, spec_gcp, spec_gcp),
        name="ssm_discretize",
    )(lam_re.astype(F32).reshape(depth, g, 1, p), lam_im.astype(F32).reshape(depth, g, 1, p),
      log_dt.astype(F32).reshape(depth, g, 1, 1),
      jnp.swapaxes(b_re.astype(F32), -1, -2), jnp.swapaxes(b_im.astype(F32), -1, -2))


def _mixer_kernel(x_ref, kinit_ref, vinit_ref, hinit_ref, wqkv_ref, wu_ref, wgl_ref, bias_ref,
                  wap_ref, a_ref, wb_ref, wc_ref, dskip_ref, wglu_ref, wout_ref, lng_ref, lnb_ref,
                  x1_ref, kout_ref, vout_ref, hout_ref,
                  qkv_scr, attn_scr, kwin, vwin, h_scr, bu_scr, u_scr,
                  *, t_len, alpha, mask_missing_chunks, row_block, seqs_per_iter):
    c = pl.program_id(1)
    n_c = pl.num_programs(1)
    rows, d_model = x_ref.shape
    win = kinit_ref.shape[1]
    key_pad = kwin.shape[1]
    q_slabs = attn_scr.shape[0]
    gq2 = q_slabs // N_KV_HEADS
    gp = a_ref.shape[1]
    d_ssm = u_scr.shape[1]
    n_gb = wb_ref.shape[0]
    gb_state = gp // n_gb
    scale = HEAD_DIM ** -0.5

    @pl.when(c == 0)
    def _():
        kwin[:, 0:win, :] = kinit_ref[...]
        vwin[:, 0:win, :] = vinit_ref[...]
        kwin[:, win + t_len:, :] = jnp.zeros((SEQS_PER_GROUP, key_pad - win - t_len, kwin.shape[2]), F32)
        vwin[:, win + t_len:, :] = jnp.zeros((SEQS_PER_GROUP, key_pad - win - t_len, vwin.shape[2]), F32)
        h_scr[...] = hinit_ref[...]

    xb = x_ref[...].astype(BF16)

    qkv = jnp.dot(xb, wqkv_ref[...], preferred_element_type=F32)
    for j in range(q_slabs + 2):
        qkv_scr[j] = qkv[:, j * LANES:(j + 1) * LANES]

    def append_kv(b, carry):
        seq_rows = pl.ds(b, t_len, stride=SEQS_PER_GROUP)
        kwin[b, win:win + t_len, :] = qkv_scr[q_slabs, seq_rows, :]
        vwin[b, win:win + t_len, :] = qkv_scr[q_slabs + 1, seq_rows, :]
        return carry

    lax.fori_loop(0, SEQS_PER_GROUP, append_kv, 0)

    def attend_group(i, carry):
        chains = [(i * seqs_per_iter + u, hk) for u in range(seqs_per_iter) for hk in range(N_KV_HEADS)]
        scores, values = [], []
        for b, hk in chains:
            seq_rows = pl.ds(b, t_len, stride=SEQS_PER_GROUP)
            parts = []
            for s in range(gq2):
                qs = qkv_scr[hk * gq2 + s, seq_rows, :]
                parts += [qs[:, :HEAD_DIM], qs[:, HEAD_DIM:]]
            q4 = jnp.concatenate(parts, axis=0).astype(BF16)
            kh = kwin[b, :, hk * HEAD_DIM:(hk + 1) * HEAD_DIM].astype(BF16)
            values.append(vwin[b, :, hk * HEAD_DIM:(hk + 1) * HEAD_DIM].astype(BF16))
            scores.append(lax.dot_general(q4, kh, (((1,), (1,)), ((), ())), preferred_element_type=F32))
        probs = []
        for (b, hk), s_ in zip(chains, scores):
            s_ = s_ * scale + bias_ref[hk]
            if mask_missing_chunks:
                key = lax.broadcasted_iota(jnp.int32, s_.shape, 1)
                first_valid = (WINDOW // CHUNK - jnp.minimum(c, WINDOW // CHUNK)) * CHUNK
                s_ = jnp.where(key >= first_valid, s_, NEG_INF)
            m = jnp.max(s_, axis=-1, keepdims=True)
            probs.append(jnp.exp(s_ - m).astype(BF16))
        ones = jnp.ones((key_pad, HEAD_DIM), BF16)
        outs = [jnp.dot(p, vh, preferred_element_type=F32) / jnp.dot(p, ones, preferred_element_type=F32)
                for p, vh in zip(probs, values)]
        for (b, hk), o in zip(chains, outs):
            for s in range(gq2):
                pair = jnp.concatenate([o[(2 * s) * t_len:(2 * s + 1) * t_len],
                                        o[(2 * s + 1) * t_len:(2 * s + 2) * t_len]], axis=1)
                attn_scr[hk * gq2 + s, pl.ds(b, t_len, stride=SEQS_PER_GROUP), :] = pair
        return carry

    lax.fori_loop(0, SEQS_PER_GROUP // seqs_per_iter, attend_group, 0)

    def slide_window(b, carry):
        knext = kwin[b, t_len:t_len + win, :]
        vnext = vwin[b, t_len:t_len + win, :]
        kwin[b, 0:win, :] = knext
        vwin[b, 0:win, :] = vnext
        return carry

    lax.fori_loop(0, SEQS_PER_GROUP, slide_window, 0)

    @pl.when(c == n_c - 1)
    def _():
        kout_ref[...] = kwin[:, 0:win, :]
        vout_ref[...] = vwin[:, 0:win, :]

    u = jnp.dot(xb, wu_ref[...], preferred_element_type=F32)
    u_scr[...] = u
    ch_b = d_ssm // n_gb
    for j in range(n_gb):
        bu = jnp.dot(u[:, j * ch_b:(j + 1) * ch_b].astype(BF16), wb_ref[j], preferred_element_type=F32)
        bu_scr[:, j * gb_state:(j + 1) * gb_state] = bu[:, :gb_state]
        bu_scr[:, gp + j * gb_state:gp + (j + 1) * gb_state] = bu[:, gb_state:]

    scan_lanes = 4 * LANES
    for lc in range(gp // scan_lanes):
        re_l = slice(lc * scan_lanes, (lc + 1) * scan_lanes)
        im_l = slice(gp + lc * scan_lanes, gp + (lc + 1) * scan_lanes)
        a_re = jnp.broadcast_to(a_ref[0:1, re_l], (SEQS_PER_GROUP, scan_lanes))
        a_im = jnp.broadcast_to(a_ref[1:2, re_l], (SEQS_PER_GROUP, scan_lanes))

        def step(t, h, re_l=re_l, im_l=im_l, a_re=a_re, a_im=a_im):
            h_re, h_im = h
            r0 = pl.multiple_of(t * SEQS_PER_GROUP, SEQS_PER_GROUP)
            n_re = a_re * h_re - a_im * h_im + bu_scr[pl.ds(r0, SEQS_PER_GROUP), re_l]
            n_im = a_re * h_im + a_im * h_re + bu_scr[pl.ds(r0, SEQS_PER_GROUP), im_l]
            bu_scr[pl.ds(r0, SEQS_PER_GROUP), re_l] = n_re
            bu_scr[pl.ds(r0, SEQS_PER_GROUP), im_l] = n_im
            return n_re, n_im

        h_re, h_im = lax.fori_loop(0, t_len, step, (h_scr[:, re_l], h_scr[:, im_l]), unroll=8)
        h_scr[:, re_l] = h_re
        h_scr[:, im_l] = h_im

    @pl.when(c == n_c - 1)
    def _():
        hout_ref[...] = h_scr[...]

    for rb in range(rows // row_block):
        r = slice(rb * row_block, (rb + 1) * row_block)
        ys = []
        for j in range(n_gb):
            hcat = jnp.concatenate([bu_scr[r, j * gb_state:(j + 1) * gb_state],
                                    bu_scr[r, gp + j * gb_state:gp + (j + 1) * gb_state]], axis=1)
            ys.append(jnp.dot(hcat.astype(BF16), wc_ref[j], preferred_element_type=F32))
        y = jnp.concatenate(ys, axis=1)
        z = jax.nn.gelu(y + dskip_ref[...] * u_scr[r, :])
        glu = jnp.dot(z.astype(BF16), wglu_ref[...], preferred_element_type=F32)
        br_b = glu[:, :d_model] * jax.nn.sigmoid(glu[:, d_model:])
        attn = jnp.concatenate([attn_scr[s, r, :] for s in range(q_slabs)], axis=1)
        br_a = jnp.dot(attn.astype(BF16), wap_ref[...], preferred_element_type=F32)
        xr = x_ref[r, :]
        gl = jnp.dot(xr.astype(BF16), wgl_ref[...], preferred_element_type=F32)
        mix = jax.nn.sigmoid(gl[:, :d_model]) * br_a + jax.nn.sigmoid(gl[:, d_model:]) * br_b
        res = alpha * xr + jnp.dot(mix.astype(BF16), wout_ref[...], preferred_element_type=F32)
        mu = jnp.mean(res, axis=-1, keepdims=True)
        xc = res - mu
        var = jnp.mean(xc * xc, axis=-1, keepdims=True)
        x1_ref[r, :] = xc * lax.rsqrt(var + LN_EPS) * lng_ref[...] + lnb_ref[...]


def _mixer(x_all, row_offset_blocks, n_groups, n_steps, t_len, kinit, vinit, hinit, lw, bias, sinks, alpha,
           mask_missing_chunks):
    n_total, d_model = x_all.shape
    rows = SEQS_PER_GROUP * t_len
    win = kinit.shape[1]
    kvw = kinit.shape[2]
    gp2 = hinit.shape[1]
    gp = gp2 // 2
    q_slabs = lw["wap"].shape[0] // LANES
    d_ssm = lw["wu"].shape[1]
    band = win + t_len
    gq = bias.shape[0] // N_KV_HEADS
    key_pad = -(-(band + 1) // LANES) * LANES
    sink_col = jnp.broadcast_to(sinks.astype(F32).reshape(N_KV_HEADS * gq, 1, 1), (N_KV_HEADS * gq, t_len, 1))
    dead_cols = jnp.full((N_KV_HEADS * gq, t_len, key_pad - band - 1), NEG_INF, F32)
    bias_s = jnp.concatenate([bias[:, :t_len, :band], sink_col, dead_cols], axis=2)
    bias_s = bias_s.reshape(N_KV_HEADS, gq * t_len, key_pad)
    row_block = min(rows, 256)

    def xmap(g, c):
        return (row_offset_blocks + g * n_steps + c, 0)

    grp3 = lambda g, c: (g, 0, 0)
    grp2 = lambda g, c: (g, 0)
    in_specs = [
        pl.BlockSpec((rows, d_model), xmap),
        pl.BlockSpec((SEQS_PER_GROUP, win, kvw), grp3),
        pl.BlockSpec((SEQS_PER_GROUP, win, kvw), grp3),
        pl.BlockSpec((SEQS_PER_GROUP, gp2), grp2),
        _const_spec(lw["wqkv"].shape), _const_spec(lw["wu"].shape), _const_spec(lw["wgl"].shape),
        _const_spec(bias_s.shape), _const_spec(lw["wap"].shape),
        _const_spec(lw["a"].shape), _const_spec(lw["wb"].shape), _const_spec(lw["wc"].shape),
        _const_spec(lw["dskip"].shape), _const_spec(lw["wglu"].shape), _const_spec(lw["wout"].shape),
        _const_spec(lw["ln1g"].shape), _const_spec(lw["ln1b"].shape),
    ]
    out_specs = (
        pl.BlockSpec((rows, d_model), xmap),
        pl.BlockSpec((SEQS_PER_GROUP, win, kvw), grp3),
        pl.BlockSpec((SEQS_PER_GROUP, win, kvw), grp3),
        pl.BlockSpec((SEQS_PER_GROUP, gp2), grp2),
    )
    n_seq = n_groups * SEQS_PER_GROUP
    out_shape = (
        jax.ShapeDtypeStruct((n_total, d_model), F32),
        jax.ShapeDtypeStruct((n_seq, win, kvw), F32),
        jax.ShapeDtypeStruct((n_seq, win, kvw), F32),
        jax.ShapeDtypeStruct((n_seq, gp2), F32),
    )
    scratch = [
        pltpu.VMEM((q_slabs + 2, rows, LANES), F32),
        pltpu.VMEM((q_slabs, rows, LANES), F32),
        pltpu.VMEM((SEQS_PER_GROUP, key_pad, kvw), F32),
        pltpu.VMEM((SEQS_PER_GROUP, key_pad, kvw), F32),
        pltpu.VMEM((SEQS_PER_GROUP, gp2), F32),
        pltpu.VMEM((rows, gp2), F32),
        pltpu.VMEM((rows, d_ssm), F32),
    ]
    kern = functools.partial(_mixer_kernel, t_len=t_len, alpha=alpha,
                             mask_missing_chunks=mask_missing_chunks, row_block=row_block,
                             seqs_per_iter=ATTN_SEQS_PER_ITER)
    args = [x_all, kinit, vinit, hinit, lw["wqkv"], lw["wu"], lw["wgl"], bias_s, lw["wap"], lw["a"],
            lw["wb"], lw["wc"], lw["dskip"], lw["wglu"], lw["wout"], lw["ln1g"], lw["ln1b"]]
    return pl.pallas_call(
        kern,
        grid=(n_groups, n_steps),
        in_specs=in_specs,
        out_specs=out_specs,
        out_shape=out_shape,
        scratch_shapes=scratch,
        input_output_aliases={0: 0},
        compiler_params=_cparams(("arbitrary", "arbitrary")),
        name="mixer_t%d" % t_len,
    )(*args)


def _router_kernel(x_ref, wrt_ref, brt_ref, upper_ref, idx_ref, wts_ref, rank_ref, cnt_ref, base_scr):
    i = pl.program_id(0)
    n_exp = wrt_ref.shape[0]
    tb = x_ref.shape[0]
    per_group = n_exp // N_ROUTE_GROUPS
    lane_rep = tb // LANES

    @pl.when(i == 0)
    def _():
        base_scr[...] = jnp.zeros_like(base_scr)

    xb = x_ref[...].astype(BF16)
    logits = lax.dot_general(wrt_ref[...], xb, (((1,), (1,)), ((), ())), preferred_element_type=F32)
    scores = jax.nn.sigmoid(logits)
    biased = scores + jnp.concatenate([brt_ref[...]] * lane_rep, axis=1)
    neg = jnp.float32(-jnp.inf)
    big = jnp.float32(2 ** 20)

    gs = []
    member = lax.broadcasted_iota(jnp.int32, (per_group, tb), 0).astype(F32)
    for g in range(N_ROUTE_GROUPS):
        bg = biased[g * per_group:(g + 1) * per_group, :]
        m1 = jnp.max(bg, axis=0, keepdims=True)
        first = jnp.min(jnp.where(bg == m1, member, big), axis=0, keepdims=True)
        m2 = jnp.max(jnp.where(member == first, neg, bg), axis=0, keepdims=True)
        gs.append(m1 + m2)

    keep = [jnp.zeros((1, tb), F32) for _ in range(N_ROUTE_GROUPS)]
    for _ in range(TOPK_GROUPS):
        best = gs[0]
        for g in range(1, N_ROUTE_GROUPS):
            best = jnp.maximum(best, gs[g])
        taken = jnp.zeros((1, tb), F32)
        for g in range(N_ROUTE_GROUPS):
            pick = jnp.where(gs[g] == best, 1.0 - taken, 0.0)
            taken = taken + pick
            keep[g] = keep[g] + pick
            gs[g] = jnp.where(pick > 0.5, neg, gs[g])

    vals = jnp.concatenate(
        [jnp.where(jnp.broadcast_to(keep[g], (per_group, tb)) > 0.5,
                   biased[g * per_group:(g + 1) * per_group, :], NEG_INF) for g in range(N_ROUTE_GROUPS)],
        axis=0)
    eid = lax.broadcasted_iota(jnp.int32, (n_exp, tb), 0).astype(F32)
    onehot = jnp.zeros((n_exp, tb), F32)
    sel_idx = []
    sel_score = []
    for _ in range(TOP_K):
        m = jnp.max(vals, axis=0, keepdims=True)
        first = jnp.min(jnp.where(vals == m, eid, big), axis=0, keepdims=True)
        sel = eid == first
        sel_idx.append(first)
        sel_score.append(jnp.sum(jnp.where(sel, scores, 0.0), axis=0, keepdims=True))
        vals = jnp.where(sel, neg, vals)
        onehot = jnp.where(sel, 1.0, onehot)

    total = sel_score[0]
    for k in range(1, TOP_K):
        total = total + sel_score[k]

    prefix = jnp.dot(onehot.astype(BF16), upper_ref[...], preferred_element_type=F32)
    pos = prefix + jnp.concatenate([base_scr[...]] * lane_rep, axis=1)
    for k in range(TOP_K):
        idx_ref[k:k + 1, :] = sel_idx[k].astype(jnp.int32)
        wts_ref[k:k + 1, :] = sel_score[k] / total * ROUTED_SCALE
        rk = jnp.sum(jnp.where(eid == sel_idx[k], pos, 0.0), axis=0, keepdims=True)
        rank_ref[k:k + 1, :] = rk.astype(jnp.int32)
    base_scr[...] = base_scr[...] + jnp.broadcast_to(jnp.sum(onehot, axis=1, keepdims=True), base_scr.shape)
    cnt_ref[...] = base_scr[...]


def _router(x1, wrt, brt, tb):
    n, d_model = x1.shape
    n_exp = wrt.shape[0]
    upper = jnp.triu(jnp.ones((tb, tb), BF16), k=1)
    tok = lambda i: (0, i)
    return pl.pallas_call(
        _router_kernel,
        grid=(n // tb,),
        in_specs=[pl.BlockSpec((tb, d_model), lambda i: (i, 0)), _const_spec(wrt.shape), _const_spec(brt.shape),
                  _const_spec(upper.shape)],
        out_specs=(pl.BlockSpec((TOP_K, tb), tok), pl.BlockSpec((TOP_K, tb), tok), pl.BlockSpec((TOP_K, tb), tok),
                   pl.BlockSpec((n_exp, LANES), lambda i: (0, 0))),
        out_shape=(jax.ShapeDtypeStruct((TOP_K, n), jnp.int32), jax.ShapeDtypeStruct((TOP_K, n), F32),
                   jax.ShapeDtypeStruct((TOP_K, n), jnp.int32), jax.ShapeDtypeStruct((n_exp, LANES), F32)),
        scratch_shapes=[pltpu.VMEM((n_exp, LANES), F32)],
        compiler_params=_cparams(("arbitrary",)),
        name="router",
    )(x1, wrt, brt, upper)


def _pack_bf16_pairs(x):
    h = x.shape[1] // 2
    xb = x.astype(BF16).astype(F32)
    lo = lax.bitcast_convert_type(xb[:, :h], jnp.uint32) >> 16
    return lo | (lax.bitcast_convert_type(xb[:, h:], jnp.uint32) & jnp.uint32(0xFFFF0000))


def _unpack_lo(w):
    return lax.bitcast_convert_type(w << 16, F32)


def _unpack_hi(w):
    return lax.bitcast_convert_type(w & jnp.uint32(0xFFFF0000), F32)


def _dest_kernel(cstart_ref, idx_ref, rank_ref, dest_ref):
    idx = idx_ref[...]
    rank = rank_ref[...]

    def body(e, acc):
        return jnp.where(idx == e, cstart_ref[e] + rank, acc)

    dest_ref[...] = lax.fori_loop(0, cstart_ref.shape[0], body, jnp.zeros_like(rank), unroll=8)


def _dest(cstart, idx_t, rank_t, tb):
    n = idx_t.shape[1]
    tok = lambda i: (0, i)
    return pl.pallas_call(
        _dest_kernel,
        grid=(n // tb,),
        in_specs=[pl.BlockSpec(memory_space=pltpu.SMEM), pl.BlockSpec((TOP_K, tb), tok),
                  pl.BlockSpec((TOP_K, tb), tok)],
        out_specs=pl.BlockSpec((TOP_K, tb), tok),
        out_shape=jax.ShapeDtypeStruct((TOP_K, n), jnp.int32),
        compiler_params=_cparams(("arbitrary",)),
        name="slot_index",
    )(cstart, idx_t, rank_t)


def _dispatch_kernel(dest_ref, x_ref, xs_ref, buf, sem):
    tb = x_ref.shape[0]
    words = _pack_bf16_pairs(x_ref[...])
    for j in range(buf.shape[1]):
        buf[:, j, :] = words[:, j * LANES:(j + 1) * LANES]

    def issue(t, carry):
        for k in range(TOP_K):
            pltpu.make_async_copy(buf.at[pl.ds(t, 1)], xs_ref.at[pl.ds(dest_ref[k, t], 1)], sem).start(priority=k % 2)
        return carry

    lax.fori_loop(0, tb, issue, 0, unroll=2)
    for k in range(TOP_K):
        pltpu.make_async_copy(buf, xs_ref.at[pl.ds(0, tb)], sem).wait()


def _dispatch(dest, x1, n_slots, tb):
    n, d_model = x1.shape
    row_tiles = d_model // 2 // LANES
    return pl.pallas_call(
        _dispatch_kernel,
        grid=(n // tb,),
        in_specs=[pl.BlockSpec((TOP_K, tb), lambda i: (0, i), memory_space=pltpu.SMEM),
                  pl.BlockSpec((tb, d_model), lambda i: (i, 0))],
        out_specs=pl.BlockSpec(memory_space=pl.ANY),
        out_shape=jax.ShapeDtypeStruct((n_slots, row_tiles, LANES), jnp.uint32),
        scratch_shapes=[pltpu.VMEM((tb, row_tiles, LANES), jnp.uint32), pltpu.SemaphoreType.DMA(())],
        compiler_params=_cparams(("arbitrary",)),
        name="dispatch",
    )(dest, x1)


def _expert_kernel(wt_ref, we_ref, lo_ref, hi_ref, xs_ref, wg_ref, wu_ref, wd_ref, os_ref, wgu_scr, wd_scr, *, bm):
    w = pl.program_id(0)
    row_tiles = xs_ref.shape[0] // bm
    f = wg_ref.shape[2]
    prev = jnp.maximum(w - 1, 0)
    lo = lo_ref[w]
    hi = hi_ref[w]
    live = hi > lo
    new_expert = jnp.logical_or(w == 0, we_ref[w] != we_ref[prev])
    new_tile = jnp.logical_or(w == 0, wt_ref[w] != wt_ref[prev])

    @pl.when(jnp.logical_and(live, new_expert))
    def _():
        wgu_scr[:, :f] = wg_ref[0].astype(BF16)
        wgu_scr[:, f:] = wu_ref[0].astype(BF16)
        wd_scr[...] = wd_ref[0].astype(BF16)

    @pl.when(live)
    def _():
        words = [xs_ref[pl.ds(j, bm, stride=row_tiles), :] for j in range(row_tiles)]
        x = jnp.concatenate([_unpack_lo(word).astype(BF16) for word in words]
                            + [_unpack_hi(word).astype(BF16) for word in words], axis=1)
        gu = jnp.dot(x, wgu_scr[...], preferred_element_type=F32)
        h = jax.nn.silu(gu[:, :f]) * gu[:, f:]
        out = _pack_bf16_pairs(jnp.dot(h.astype(BF16), wd_scr[...], preferred_element_type=F32))
        out_tiles = os_ref.shape[0] // bm

        @pl.when(new_tile)
        def _():
            for j in range(out_tiles):
                os_ref[pl.ds(j, bm, stride=out_tiles), :] = out[:, j * LANES:(j + 1) * LANES]

        @pl.when(jnp.logical_not(new_tile))
        def _():
            row = lax.broadcasted_iota(jnp.int32, (bm, LANES), 0)
            mine = jnp.logical_and(row >= lo, row < hi)
            for j in range(out_tiles):
                rows_j = pl.ds(j, bm, stride=out_tiles)
                os_ref[rows_j, :] = jnp.where(mine, out[:, j * LANES:(j + 1) * LANES], os_ref[rows_j, :])


def _experts(wt, we, lo, hi, xs, w_gate, w_up, w_down, bm):
    n_slots, row_tiles, _ = xs.shape
    d_model, f = w_gate.shape[1], w_gate.shape[2]
    xs2d = xs.reshape(n_slots * row_tiles, LANES)
    row_map = lambda w, wt, we, lo, hi: (wt[w], 0)
    w_map = lambda w, wt, we, lo, hi: (we[w], 0, 0)
    return pl.pallas_call(
        functools.partial(_expert_kernel, bm=bm),
        grid_spec=pltpu.PrefetchScalarGridSpec(
            num_scalar_prefetch=4,
            grid=(wt.shape[0],),
            in_specs=[pl.BlockSpec((bm * row_tiles, LANES), row_map), pl.BlockSpec((1, d_model, f), w_map),
                      pl.BlockSpec((1, d_model, f), w_map), pl.BlockSpec((1, f, d_model), w_map)],
            out_specs=pl.BlockSpec((bm * row_tiles, LANES), row_map),
            scratch_shapes=[pltpu.VMEM((d_model, 2 * f), BF16), pltpu.VMEM((f, d_model), BF16)],
        ),
        out_shape=jax.ShapeDtypeStruct((n_slots * row_tiles, LANES), jnp.uint32),
        compiler_params=_cparams(("arbitrary",)),
        name="experts",
    )(wt, we, lo, hi, xs2d, w_gate, w_up, w_down).reshape(n_slots, row_tiles, LANES)


def _work_list(counts, bm, n_tiles, n_work):
    n_exp = counts.shape[0]
    cend = jnp.cumsum(counts)
    cstart = cend - counts
    has = counts > 0
    first_tile = cstart // bm
    n_items = jnp.where(has, (cend - 1) // bm - first_tile + 1, 0)
    wend = jnp.cumsum(n_items)
    woff = wend - n_items
    total = wend[-1]
    w = jnp.arange(n_work, dtype=jnp.int32)
    valid = w < total
    w_eff = jnp.minimum(w, total - 1)
    we = jnp.minimum(jnp.sum((wend[None, :] <= w_eff[:, None]).astype(jnp.int32), axis=1), n_exp - 1)
    onehot = we[:, None] == jnp.arange(n_exp, dtype=jnp.int32)[None, :]
    pick = lambda v: jnp.sum(jnp.where(onehot, v[None, :], 0), axis=1)
    wt = pick(first_tile) + (w_eff - pick(woff))
    lo = jnp.where(valid, jnp.maximum(pick(cstart) - wt * bm, 0), 0)
    hi = jnp.where(valid, jnp.minimum(pick(cend) - wt * bm, bm), 0)
    i32 = lambda v: v.astype(jnp.int32)
    return i32(cstart), i32(wt), i32(we), i32(lo), i32(hi)


def _rows_view(ref):
    rows, rt, lanes = ref.shape
    return ref.reshape(rows * rt, lanes)


def _combine_kernel(dest_ref, x_ref, wts_ref, os_ref, wsgu_ref, wsd_ref, lng_ref, lnb_ref, *rest,
                    alpha, n_prompt_blocks):
    final = n_prompt_blocks is not None
    n_out = 2 if final else 1
    outs, gbufs, sem = rest[:n_out], rest[n_out:n_out + TOP_K], rest[n_out + TOP_K]
    tb, d_model = x_ref.shape
    f = wsd_ref.shape[0]
    rt = gbufs[0].shape[1]

    def issue(t, carry):
        for k in range(TOP_K):
            pltpu.make_async_copy(os_ref.at[pl.ds(dest_ref[k, t], 1)], gbufs[k].at[pl.ds(t, 1)],
                                  sem).start(priority=k % 2)
        return carry

    lax.fori_loop(0, tb, issue, 0, unroll=2)

    x = x_ref[...]
    gu = jnp.dot(x.astype(BF16), wsgu_ref[...], preferred_element_type=F32)
    h = jax.nn.silu(gu[:, :f]) * gu[:, f:]
    shared = jnp.dot(h.astype(BF16), wsd_ref[...], preferred_element_type=F32)

    for k in range(TOP_K):
        pltpu.make_async_copy(os_ref.at[pl.ds(0, tb)], gbufs[k], sem).wait()

    views = [_rows_view(g) for g in gbufs]
    w = wts_ref[...]
    rb_rows = 4 * SUBLANES
    blocks = []
    for rb in range(tb // rb_rows):
        acc_lo = [None] * rt
        acc_hi = [None] * rt
        for k in range(TOP_K):
            wk = jnp.broadcast_to(w[rb * rb_rows:(rb + 1) * rb_rows, k:k + 1], (rb_rows, LANES))
            for j in range(rt):
                word = views[k][pl.ds(rb * rb_rows * rt + j, rb_rows, stride=rt), :]
                lo = _unpack_lo(word) * wk
                hi = _unpack_hi(word) * wk
                acc_lo[j] = lo if k == 0 else acc_lo[j] + lo
                acc_hi[j] = hi if k == 0 else acc_hi[j] + hi
        blocks.append(jnp.concatenate(acc_lo + acc_hi, axis=1))
    routed = jnp.concatenate(blocks, axis=0)
    res = alpha * x + (routed + shared)
    mu = jnp.mean(res, axis=-1, keepdims=True)
    xc = res - mu
    var = jnp.mean(xc * xc, axis=-1, keepdims=True)
    y = xc * lax.rsqrt(var + LN_EPS) * lng_ref[...] + lnb_ref[...]
    if not final:
        outs[0][...] = y
        return

    slab = rest[n_out + TOP_K + 1]
    tq = tb // SEQS_PER_GROUP
    for j in range(d_model // LANES):
        slab[j] = y[:, j * LANES:(j + 1) * LANES]

    def write(o_ref):
        for b in range(SEQS_PER_GROUP):
            for j in range(d_model // LANES):
                o_ref[b, :, j * LANES:(j + 1) * LANES] = slab[j, pl.ds(b, tq, stride=SEQS_PER_GROUP), :]

    i = pl.program_id(0)

    @pl.when(i < n_prompt_blocks)
    def _():
        write(outs[0])

    @pl.when(i >= n_prompt_blocks)
    def _():
        write(outs[1])


def _combine(dest, x1, wts_nk, os_, wsgu, wsd, lng, lnb, alpha, tb, final_shapes=None):
    n, d_model = x1.shape
    rt = os_.shape[1]
    scratch = [pltpu.VMEM((tb, rt, LANES), jnp.uint32) for _ in range(TOP_K)] + [pltpu.SemaphoreType.DMA(())]
    if final_shapes is None:
        n_prompt_blocks = None
        out_specs = pl.BlockSpec((tb, d_model), lambda i: (i, 0))
        out_shape = jax.ShapeDtypeStruct((n, d_model), F32)
    else:
        batch, seq, dec_batch, dec_seq = final_shapes
        tq = tb // SEQS_PER_GROUP
        assert dec_seq == tq and seq % tq == 0
        wpg = seq // tq
        n_prompt_blocks = batch * seq // tb
        last = n_prompt_blocks - 1
        out_specs = (
            pl.BlockSpec((None, SEQS_PER_GROUP, tq, d_model),
                         lambda i: (jnp.minimum(i, last) // wpg, 0, jnp.minimum(i, last) % wpg, 0)),
            pl.BlockSpec((None, SEQS_PER_GROUP, tq, d_model), lambda i: (jnp.maximum(i - n_prompt_blocks, 0), 0, 0, 0)),
        )
        out_shape = (jax.ShapeDtypeStruct((batch // SEQS_PER_GROUP, SEQS_PER_GROUP, seq, d_model), F32),
                     jax.ShapeDtypeStruct((dec_batch // SEQS_PER_GROUP, SEQS_PER_GROUP, dec_seq, d_model), F32))
        scratch.append(pltpu.VMEM((d_model // LANES, tb, LANES), F32))
    return pl.pallas_call(
        functools.partial(_combine_kernel, alpha=alpha, n_prompt_blocks=n_prompt_blocks),
        grid=(n // tb,),
        in_specs=[pl.BlockSpec((TOP_K, tb), lambda i: (0, i), memory_space=pltpu.SMEM),
                  pl.BlockSpec((tb, d_model), lambda i: (i, 0)),
                  pl.BlockSpec((tb, TOP_K), lambda i: (i, 0)),
                  pl.BlockSpec(memory_space=pl.ANY),
                  _const_spec(wsgu.shape), _const_spec(wsd.shape), _const_spec(lng.shape), _const_spec(lnb.shape)],
        out_specs=out_specs,
        out_shape=out_shape,
        scratch_shapes=scratch,
        compiler_params=_cparams(("arbitrary",)),
        name="combine" if final_shapes is None else "combine_out",
    )(dest, x1, wts_nk, os_, wsgu, wsd, lng, lnb)


def _to_time_major(x):
    b, t, d = x.shape
    return x.reshape(b // SEQS_PER_GROUP, SEQS_PER_GROUP, t, d).transpose(0, 2, 1, 3).reshape(b * t, d)


def _from_time_major(x, b, t):
    d = x.shape[-1]
    return x.reshape(b // SEQS_PER_GROUP, t, SEQS_PER_GROUP, d).transpose(0, 2, 1, 3).reshape(b, t, d)


def _block_diag_in(bbt, n_gb):
    g, c, p = bbt.shape
    gl = g // n_gb
    eye = jnp.eye(gl, dtype=bbt.dtype)
    return jnp.einsum("jgcp,gh->jgchp", bbt.reshape(n_gb, gl, c, p), eye).reshape(n_gb, gl * c, gl * p)


def _block_diag_out(cm, n_gb):
    g, c, p = cm.shape
    gl = g // n_gb
    eye = jnp.eye(gl, dtype=cm.dtype)
    return jnp.einsum("jgcp,gh->jhpgc", cm.reshape(n_gb, gl, c, p), eye).reshape(n_gb, gl * p, gl * c)


def kernel(x_prompt, x_sample, cache_k, cache_v, state_ssm_re, state_ssm_im, rel_bias, w_in, attn_sinks, w_attn_proj, lam_re, lam_im, log_dt, b_re, b_im, c_re, c_im, d_skip, w_glu, w_out, ln1_g, ln1_b, w_router, b_router, w_gate, w_up, w_down, ws_gate, ws_up, ws_down, ln2_g, ln2_b):
    batch, seq, d_model = x_prompt.shape
    dec_batch, dec_seq, _ = x_sample.shape
    depth = w_in.shape[0]
    win = cache_k.shape[2]
    n_q_heads = attn_sinks.shape[1]
    q_w = n_q_heads * HEAD_DIM
    kv_w = N_KV_HEADS * HEAD_DIM
    d_ssm = d_skip.shape[1]
    n_groups_ssm, state_dim = lam_re.shape[1], lam_re.shape[2]
    gp = n_groups_ssm * state_dim
    n_exp = w_router.shape[2]
    alpha = (2 * depth) ** 0.25
    assert win == WINDOW and seq % CHUNK == 0 and dec_seq <= CHUNK
    assert batch % SEQS_PER_GROUP == 0 and dec_batch % SEQS_PER_GROUP == 0
    qch = (PAST_LEN + np.arange(dec_seq)) // CHUNK
    kch = np.concatenate([PAST_LEN - win + np.arange(win), PAST_LEN + np.arange(dec_seq)]) // CHUNK
    assert np.all((kch[None, :] >= qch[:, None] - WINDOW // CHUNK) & (kch[None, :] <= qch[:, None]))

    n_p = batch * seq
    n_s = dec_batch * dec_seq
    n_tok = n_p + n_s
    n_gb = n_groups_ssm // SSM_GROUPS_PER_BATCH

    bias = _rel_bias_table(rel_bias)
    a_re, a_im, bbt_re, bbt_im = _discretize(lam_re, lam_im, log_dt, b_re, b_im)

    x_all = jnp.concatenate([_to_time_major(x_prompt.astype(F32)), _to_time_major(x_sample.astype(F32))], axis=0)
    zeros_kv = jnp.zeros((batch, win, kv_w), F32)
    zeros_h = jnp.zeros((batch, 2 * gp), F32)

    tb_route = 512 if n_tok % 512 == 0 else 256
    tb_move = 256
    bm = 512
    n_slots = n_tok * TOP_K
    assert n_slots % bm == 0
    n_tiles = n_slots // bm
    n_work = n_tiles + n_exp - 1

    outs = {k: [] for k in ("kp", "vp", "hp", "ks", "vs", "hs")}
    for l in range(depth):
        wl = w_in[l].astype(BF16)
        lw = dict(
            wqkv=wl[:, :q_w + 2 * kv_w], wu=wl[:, q_w + 2 * kv_w:q_w + 2 * kv_w + d_ssm],
            wgl=wl[:, q_w + 2 * kv_w + d_ssm:], wap=w_attn_proj[l].astype(BF16),
            a=jnp.stack([a_re[l].reshape(gp), a_im[l].reshape(gp)]),
            wb=jnp.concatenate([_block_diag_in(bbt_re[l], n_gb), _block_diag_in(bbt_im[l], n_gb)], axis=2).astype(BF16),
            wc=jnp.concatenate([_block_diag_out(c_re[l].astype(F32), n_gb),
                                -_block_diag_out(c_im[l].astype(F32), n_gb)], axis=1).astype(BF16),
            dskip=d_skip[l].astype(F32).reshape(1, d_ssm), wglu=w_glu[l].astype(BF16), wout=w_out[l].astype(BF16),
            ln1g=ln1_g[l].astype(F32).reshape(1, d_model), ln1b=ln1_b[l].astype(F32).reshape(1, d_model),
        )
        x1, kp, vp, hp = _mixer(x_all, 0, batch // SEQS_PER_GROUP, seq // CHUNK, CHUNK, zeros_kv, zeros_kv, zeros_h,
                                lw, bias, attn_sinks[l], alpha, True)
        hinit = jnp.concatenate([state_ssm_re[l].astype(F32).reshape(dec_batch, gp),
                                 state_ssm_im[l].astype(F32).reshape(dec_batch, gp)], axis=1)
        x1, ks, vs, hs = _mixer(x1, n_p // (SEQS_PER_GROUP * dec_seq), dec_batch // SEQS_PER_GROUP, 1, dec_seq,
                                cache_k[l].astype(F32).reshape(dec_batch, win, kv_w),
                                cache_v[l].astype(F32).reshape(dec_batch, win, kv_w), hinit,
                                lw, bias, attn_sinks[l], alpha, False)
        for name, val in zip(("kp", "vp", "hp", "ks", "vs", "hs"), (kp, vp, hp, ks, vs, hs)):
            outs[name].append(val)

        wrt = w_router[l].astype(BF16).T
        brt = jnp.broadcast_to(b_router[l].astype(F32)[:, None], (n_exp, LANES))
        idx_t, wts_t, rank_t, cnt = _router(x1, wrt, brt, tb_route)
        cstart, wt, we, lo, hi = _work_list(cnt[:, 0].astype(jnp.int32), bm, n_tiles, n_work)
        dest = _dest(cstart, idx_t, rank_t, tb_route)
        xs = _dispatch(dest, x1, n_slots, tb_move)
        os_ = _experts(wt, we, lo, hi, xs, w_gate[l], w_up[l], w_down[l], bm)
        wsgu = jnp.concatenate([ws_gate[l], ws_up[l]], axis=1).astype(BF16)
        x_all = _combine(dest, x1, wts_t.T, os_, wsgu, ws_down[l].astype(BF16),
                         ln2_g[l].astype(F32).reshape(1, d_model), ln2_b[l].astype(F32).reshape(1, d_model),
                         alpha, tb_move, final_shapes=(batch, seq, dec_batch, dec_seq) if l == depth - 1 else None)

    y_prompt = x_all[0].reshape(batch, seq, d_model)
    y_sample = x_all[1].reshape(dec_batch, dec_seq, d_model)

    def kv(vals, b):
        return jnp.stack(vals).reshape(depth, b, win, N_KV_HEADS, HEAD_DIM)

    def st(vals, b, part):
        return jnp.stack(vals)[:, :, part * gp:(part + 1) * gp].reshape(depth, b, n_groups_ssm, state_dim)

    return (y_prompt, y_sample, kv(outs["kp"], batch), kv(outs["vp"], batch), st(outs["hp"], batch, 0),
            st(outs["hp"], batch, 1), kv(outs["ks"], dec_batch), kv(outs["vs"], dec_batch),
            st(outs["hs"], dec_batch, 0), st(outs["hs"], dec_batch, 1))
```

```python
import functools
import math

import jax
import jax.numpy as jnp
import numpy as np
from jax import lax
from jax.experimental import pallas as pl
from jax.experimental.pallas import tpu as pltpu

CHUNK = 64
WINDOW = 128
HEAD_DIM = 64
N_KV_HEADS = 2
MAX_DISTANCE = 128
CH_PER_GROUP = 16
STATE_DIM = 64
TOP_K = 8
N_ROUTE_GROUPS = 8
TOPK_GROUPS = 4
ROUTED_SCALE = 2.5
LN_EPS = 1e-5
NEG_INF = -1e30
PAST_LEN = 1024

SUBLANES = 8
LANES = 128
SEQS_PER_GROUP = SUBLANES
SSM_GROUPS_PER_BATCH = LANES // CH_PER_GROUP
VMEM_LIMIT_BYTES = 60 * 1024 * 1024
ATTN_SEQS_PER_ITER = 8
MIXER_ROW_BLOCK = 256

BF16 = jnp.bfloat16
F32 = jnp.float32


def _cparams(sem):
    return pltpu.CompilerParams(dimension_semantics=sem, vmem_limit_bytes=VMEM_LIMIT_BYTES)


def _const_spec(shape):
    nd = len(shape)
    return pl.BlockSpec(shape, lambda *_: (0,) * nd, pipeline_mode=pl.Buffered(1))


def _bias_kernel(bucket_ref, tbl_ref, out_ref):
    n_buckets, n_heads = tbl_ref.shape
    bk = bucket_ref[...]
    for h in range(n_heads):
        acc = jnp.zeros(bk.shape, F32)
        for b in range(n_buckets):
            acc = jnp.where(bk == b, tbl_ref[b, h], acc)
        out_ref[h] = acc


def _rel_bias_table(rel_bias):
    n_buckets, n_heads = rel_bias.shape
    band = WINDOW + CHUNK
    rel = (jnp.arange(band) - WINDOW)[None, :] - jnp.arange(CHUNK)[:, None]
    nb = n_buckets // 2
    max_exact = nb // 2
    n = jnp.abs(rel)
    nf = jnp.maximum(n, 1).astype(F32)
    large = max_exact + (jnp.log(nf / max_exact) / math.log(MAX_DISTANCE / max_exact) * (nb - max_exact)).astype(jnp.int32)
    large = jnp.minimum(large, nb - 1)
    bucket = (jnp.where(rel > 0, nb, 0) + jnp.where(n < max_exact, n, large)).astype(jnp.int32)
    return pl.pallas_call(
        _bias_kernel,
        out_shape=jax.ShapeDtypeStruct((n_heads, CHUNK, band), F32),
        in_specs=[pl.BlockSpec(memory_space=pltpu.VMEM), pl.BlockSpec(memory_space=pltpu.SMEM)],
        out_specs=pl.BlockSpec(memory_space=pltpu.VMEM),
        name="rel_bias_table",
    )(bucket, rel_bias.astype(F32))


def _disc_kernel(lr_ref, li_ref, ldt_ref, br_ref, bi_ref, are_ref, aim_ref, bbr_ref, bbi_ref):
    lr = lr_ref[0]
    li = li_ref[0]
    dt = jnp.exp(ldt_ref[0])
    mag = jnp.exp(lr * dt)
    a_re = mag * jnp.cos(li * dt)
    a_im = mag * jnp.sin(li * dt)
    den = lr * lr + li * li
    f_re = ((a_re - 1.0) * lr + a_im * li) / den
    f_im = (a_im * lr - (a_re - 1.0) * li) / den
    br = br_ref[0]
    bi = bi_ref[0]
    are_ref[0] = a_re
    aim_ref[0] = a_im
    bbr_ref[0] = f_re * br - f_im * bi
    bbi_ref[0] = f_re * bi + f_im * br


def _discretize(lam_re, lam_im, log_dt, b_re, b_im):
    depth, g, p = lam_re.shape
    c = b_re.shape[-1]
    spec_gp = pl.BlockSpec((1, g, 1, p), lambda l: (l, 0, 0, 0))
    spec_g1 = pl.BlockSpec((1, g, 1, 1), lambda l: (l, 0, 0, 0))
    spec_gcp = pl.BlockSpec((1, g, c, p), lambda l: (l, 0, 0, 0))
    return pl.pallas_call(
        _disc_kernel,
        grid=(depth,),
        out_shape=(jax.ShapeDtypeStruct((depth, g, 1, p), F32), jax.ShapeDtypeStruct((depth, g, 1, p), F32),
                   jax.ShapeDtypeStruct((depth, g, c, p), F32), jax.ShapeDtypeStruct((depth, g, c, p), F32)),
        in_specs=[spec_gp, spec_gp, spec_g1, spec_gcp, spec_gcp],
        out_specs=(spec_gp, spec_gp, spec_gcp, spec_gcp),
        name="ssm_discretize",
    )(lam_re.astype(F32).reshape(depth, g, 1, p), lam_im.astype(F32).reshape(depth, g, 1, p),
      log_dt.astype(F32).reshape(depth, g, 1, 1),
      jnp.swapaxes(b_re.astype(F32), -1, -2), jnp.swapaxes(b_im.astype(F32), -1, -2))


def _mixer_kernel(x_ref, kinit_ref, vinit_ref, hinit_ref, wqkv_ref, wu_ref, wgl_ref, bias_ref,
                  wap_ref, a_ref, wb_ref, wc_ref, dskip_ref, wglu_ref, wout_ref, lng_ref, lnb_ref,
                  x1_ref, kout_ref, vout_ref, hout_ref,
                  qkv_scr, attn_scr, kwin, vwin, h_scr, bu_scr, u_scr,
                  *, t_len, alpha, mask_missing_chunks, row_block, seqs_per_iter):
    c = pl.program_id(1)
    n_c = pl.num_programs(1)
    rows, d_model = x_ref.shape
    win = kinit_ref.shape[1]
    key_pad = kwin.shape[1]
    q_slabs = attn_scr.shape[0]
    gq2 = q_slabs // N_KV_HEADS
    gp = a_ref.shape[1]
    d_ssm = u_scr.shape[1]
    n_gb = wb_ref.shape[0]
    gb_state = gp // n_gb
    scale = HEAD_DIM ** -0.5

    @pl.when(c == 0)
    def _():
        kwin[:, 0:win, :] = kinit_ref[...]
        vwin[:, 0:win, :] = vinit_ref[...]
        kwin[:, win + t_len:, :] = jnp.zeros((SEQS_PER_GROUP, key_pad - win - t_len, kwin.shape[2]), F32)
        vwin[:, win + t_len:, :] = jnp.zeros((SEQS_PER_GROUP, key_pad - win - t_len, vwin.shape[2]), F32)
        h_scr[...] = hinit_ref[...]

    xb = x_ref[...].astype(BF16)

    qkv = jnp.dot(xb, wqkv_ref[...], preferred_element_type=F32)
    for j in range(q_slabs + 2):
        qkv_scr[j] = qkv[:, j * LANES:(j + 1) * LANES]

    def append_kv(b, carry):
        seq_rows = pl.ds(b, t_len, stride=SEQS_PER_GROUP)
        kwin[b, win:win + t_len, :] = qkv_scr[q_slabs, seq_rows, :]
        vwin[b, win:win + t_len, :] = qkv_scr[q_slabs + 1, seq_rows, :]
        return carry

    lax.fori_loop(0, SEQS_PER_GROUP, append_kv, 0)

    def attend_group(i, carry):
        chains = [(i * seqs_per_iter + u, hk) for u in range(seqs_per_iter) for hk in range(N_KV_HEADS)]
        scores, values = [], []
        for b, hk in chains:
            seq_rows = pl.ds(b, t_len, stride=SEQS_PER_GROUP)
            parts = []
            for s in range(gq2):
                qs = qkv_scr[hk * gq2 + s, seq_rows, :]
                parts += [qs[:, :HEAD_DIM], qs[:, HEAD_DIM:]]
            q4 = jnp.concatenate(parts, axis=0).astype(BF16)
            kh = kwin[b, :, hk * HEAD_DIM:(hk + 1) * HEAD_DIM].astype(BF16)
            values.append(vwin[b, :, hk * HEAD_DIM:(hk + 1) * HEAD_DIM].astype(BF16))
            scores.append(lax.dot_general(q4, kh, (((1,), (1,)), ((), ())), preferred_element_type=F32))
        probs = []
        for (b, hk), s_ in zip(chains, scores):
            s_ = s_ * scale + bias_ref[hk]
            if mask_missing_chunks:
                key = lax.broadcasted_iota(jnp.int32, s_.shape, 1)
                first_valid = (WINDOW // CHUNK - jnp.minimum(c, WINDOW // CHUNK)) * CHUNK
                s_ = jnp.where(key >= first_valid, s_, NEG_INF)
            m = jnp.max(s_, axis=-1, keepdims=True)
            probs.append(jnp.exp(s_ - m).astype(BF16))
        ones = jnp.ones((key_pad, HEAD_DIM), BF16)
        outs = [jnp.dot(p, vh, preferred_element_type=F32) / jnp.dot(p, ones, preferred_element_type=F32)
                for p, vh in zip(probs, values)]
        for (b, hk), o in zip(chains, outs):
            for s in range(gq2):
                pair = jnp.concatenate([o[(2 * s) * t_len:(2 * s + 1) * t_len],
                                        o[(2 * s + 1) * t_len:(2 * s + 2) * t_len]], axis=1)
                attn_scr[hk * gq2 + s, pl.ds(b, t_len, stride=SEQS_PER_GROUP), :] = pair
        return carry

    lax.fori_loop(0, SEQS_PER_GROUP // seqs_per_iter, attend_group, 0)

    def slide_window(b, carry):
        knext = kwin[b, t_len:t_len + win, :]
        vnext = vwin[b, t_len:t_len + win, :]
        kwin[b, 0:win, :] = knext
        vwin[b, 0:win, :] = vnext
        return carry

    lax.fori_loop(0, SEQS_PER_GROUP, slide_window, 0)

    @pl.when(c == n_c - 1)
    def _():
        kout_ref[...] = kwin[:, 0:win, :]
        vout_ref[...] = vwin[:, 0:win, :]

    u = jnp.dot(xb, wu_ref[...], preferred_element_type=F32)
    u_scr[...] = u
    ch_b = d_ssm // n_gb
    for j in range(n_gb):
        bu = jnp.dot(u[:, j * ch_b:(j + 1) * ch_b].astype(BF16), wb_ref[j], preferred_element_type=F32)
        bu_scr[:, j * gb_state:(j + 1) * gb_state] = bu[:, :gb_state]
        bu_scr[:, gp + j * gb_state:gp + (j + 1) * gb_state] = bu[:, gb_state:]

    scan_lanes = 4 * LANES
    for lc in range(gp // scan_lanes):
        re_l = slice(lc * scan_lanes, (lc + 1) * scan_lanes)
        im_l = slice(gp + lc * scan_lanes, gp + (lc + 1) * scan_lanes)
        a_re = jnp.broadcast_to(a_ref[0:1, re_l], (SEQS_PER_GROUP, scan_lanes))
        a_im = jnp.broadcast_to(a_ref[1:2, re_l], (SEQS_PER_GROUP, scan_lanes))

        def step(t, h, re_l=re_l, im_l=im_l, a_re=a_re, a_im=a_im):
            h_re, h_im = h
            r0 = pl.multiple_of(t * SEQS_PER_GROUP, SEQS_PER_GROUP)
            n_re = a_re * h_re - a_im * h_im + bu_scr[pl.ds(r0, SEQS_PER_GROUP), re_l]
            n_im = a_re * h_im + a_im * h_re + bu_scr[pl.ds(r0, SEQS_PER_GROUP), im_l]
            bu_scr[pl.ds(r0, SEQS_PER_GROUP), re_l] = n_re
            bu_scr[pl.ds(r0, SEQS_PER_GROUP), im_l] = n_im
            return n_re, n_im

        h_re, h_im = lax.fori_loop(0, t_len, step, (h_scr[:, re_l], h_scr[:, im_l]), unroll=8)
        h_scr[:, re_l] = h_re
        h_scr[:, im_l] = h_im

    @pl.when(c == n_c - 1)
    def _():
        hout_ref[...] = h_scr[...]

    for rb in range(rows // row_block):
        r = slice(rb * row_block, (rb + 1) * row_block)
        ys = []
        for j in range(n_gb):
            hcat = jnp.concatenate([bu_scr[r, j * gb_state:(j + 1) * gb_state],
                                    bu_scr[r, gp + j * gb_state:gp + (j + 1) * gb_state]], axis=1)
            ys.append(jnp.dot(hcat.astype(BF16), wc_ref[j], preferred_element_type=F32))
        y = jnp.concatenate(ys, axis=1)
        z = jax.nn.gelu(y + dskip_ref[...] * u_scr[r, :])
        glu = jnp.dot(z.astype(BF16), wglu_ref[...], preferred_element_type=F32)
        br_b = glu[:, :d_model] * jax.nn.sigmoid(glu[:, d_model:])
        attn = jnp.concatenate([attn_scr[s, r, :] for s in range(q_slabs)], axis=1)
        br_a = jnp.dot(attn.astype(BF16), wap_ref[...], preferred_element_type=F32)
        xr = x_ref[r, :]
        gl = jnp.dot(xr.astype(BF16), wgl_ref[...], preferred_element_type=F32)
        mix = jax.nn.sigmoid(gl[:, :d_model]) * br_a + jax.nn.sigmoid(gl[:, d_model:]) * br_b
        res = alpha * xr + jnp.dot(mix.astype(BF16), wout_ref[...], preferred_element_type=F32)
        mu = jnp.mean(res, axis=-1, keepdims=True)
        xc = res - mu
        var = jnp.mean(xc * xc, axis=-1, keepdims=True)
        x1_ref[r, :] = xc * lax.rsqrt(var + LN_EPS) * lng_ref[...] + lnb_ref[...]


def _mixer(x_all, row_offset_blocks, n_groups, n_steps, t_len, kinit, vinit, hinit, lw, bias, sinks, alpha,
           mask_missing_chunks):
    n_total, d_model = x_all.shape
    rows = SEQS_PER_GROUP * t_len
    win = kinit.shape[1]
    kvw = kinit.shape[2]
    gp2 = hinit.shape[1]
    gp = gp2 // 2
    q_slabs = lw["wap"].shape[0] // LANES
    d_ssm = lw["wu"].shape[1]
    band = win + t_len
    gq = bias.shape[0] // N_KV_HEADS
    key_pad = -(-(band + 1) // LANES) * LANES
    sink_col = jnp.broadcast_to(sinks.astype(F32).reshape(N_KV_HEADS * gq, 1, 1), (N_KV_HEADS * gq, t_len, 1))
    dead_cols = jnp.full((N_KV_HEADS * gq, t_len, key_pad - band - 1), NEG_INF, F32)
    bias_s = jnp.concatenate([bias[:, :t_len, :band], sink_col, dead_cols], axis=2)
    bias_s = bias_s.reshape(N_KV_HEADS, gq * t_len, key_pad)
    row_block = min(rows, MIXER_ROW_BLOCK)

    def xmap(g, c):
        return (row_offset_blocks + g * n_steps + c, 0)

    grp3 = lambda g, c: (g, 0, 0)
    grp2 = lambda g, c: (g, 0)
    in_specs = [
        pl.BlockSpec((rows, d_model), xmap),
        pl.BlockSpec((SEQS_PER_GROUP, win, kvw), grp3),
        pl.BlockSpec((SEQS_PER_GROUP, win, kvw), grp3),
        pl.BlockSpec((SEQS_PER_GROUP, gp2), grp2),
        _const_spec(lw["wqkv"].shape), _const_spec(lw["wu"].shape), _const_spec(lw["wgl"].shape),
        _const_spec(bias_s.shape), _const_spec(lw["wap"].shape),
        _const_spec(lw["a"].shape), _const_spec(lw["wb"].shape), _const_spec(lw["wc"].shape),
        _const_spec(lw["dskip"].shape), _const_spec(lw["wglu"].shape), _const_spec(lw["wout"].shape),
        _const_spec(lw["ln1g"].shape), _const_spec(lw["ln1b"].shape),
    ]
    out_specs = (
        pl.BlockSpec((rows, d_model), xmap),
        pl.BlockSpec((SEQS_PER_GROUP, win, kvw), grp3),
        pl.BlockSpec((SEQS_PER_GROUP, win, kvw), grp3),
        pl.BlockSpec((SEQS_PER_GROUP, gp2), grp2),
    )
    n_seq = n_groups * SEQS_PER_GROUP
    out_shape = (
        jax.ShapeDtypeStruct((n_total, d_model), F32),
        jax.ShapeDtypeStruct((n_seq, win, kvw), F32),
        jax.ShapeDtypeStruct((n_seq, win, kvw), F32),
        jax.ShapeDtypeStruct((n_seq, gp2), F32),
    )
    scratch = [
        pltpu.VMEM((q_slabs + 2, rows, LANES), F32),
        pltpu.VMEM((q_slabs, rows, LANES), F32),
        pltpu.VMEM((SEQS_PER_GROUP, key_pad, kvw), F32),
        pltpu.VMEM((SEQS_PER_GROUP, key_pad, kvw), F32),
        pltpu.VMEM((SEQS_PER_GROUP, gp2), F32),
        pltpu.VMEM((rows, gp2), F32),
        pltpu.VMEM((rows, d_ssm), F32),
    ]
    kern = functools.partial(_mixer_kernel, t_len=t_len, alpha=alpha,
                             mask_missing_chunks=mask_missing_chunks, row_block=row_block,
                             seqs_per_iter=ATTN_SEQS_PER_ITER)
    args = [x_all, kinit, vinit, hinit, lw["wqkv"], lw["wu"], lw["wgl"], bias_s, lw["wap"], lw["a"],
            lw["wb"], lw["wc"], lw["dskip"], lw["wglu"], lw["wout"], lw["ln1g"], lw["ln1b"]]
    return pl.pallas_call(
        kern,
        grid=(n_groups, n_steps),
        in_specs=in_specs,
        out_specs=out_specs,
        out_shape=out_shape,
        scratch_shapes=scratch,
        input_output_aliases={0: 0},
        compiler_params=_cparams(("arbitrary", "arbitrary")),
        name="mixer_t%d" % t_len,
    )(*args)


def _router_kernel(x_ref, wrt_ref, brt_ref, upper_ref, idx_ref, wts_ref, rank_ref, cnt_ref, base_scr):
    i = pl.program_id(0)
    n_exp = wrt_ref.shape[0]
    tb = x_ref.shape[0]
    per_group = n_exp // N_ROUTE_GROUPS
    lane_rep = tb // LANES

    @pl.when(i == 0)
    def _():
        base_scr[...] = jnp.zeros_like(base_scr)

    xb = x_ref[...].astype(BF16)
    logits = lax.dot_general(wrt_ref[...], xb, (((1,), (1,)), ((), ())), preferred_element_type=F32)
    scores = jax.nn.sigmoid(logits)
    biased = scores + jnp.concatenate([brt_ref[...]] * lane_rep, axis=1)
    neg = jnp.float32(-jnp.inf)
    big = jnp.float32(2 ** 20)

    gs = []
    member = lax.broadcasted_iota(jnp.int32, (per_group, tb), 0).astype(F32)
    for g in range(N_ROUTE_GROUPS):
        bg = biased[g * per_group:(g + 1) * per_group, :]
        m1 = jnp.max(bg, axis=0, keepdims=True)
        first = jnp.min(jnp.where(bg == m1, member, big), axis=0, keepdims=True)
        m2 = jnp.max(jnp.where(member == first, neg, bg), axis=0, keepdims=True)
        gs.append(m1 + m2)

    keep = [jnp.zeros((1, tb), F32) for _ in range(N_ROUTE_GROUPS)]
    for _ in range(TOPK_GROUPS):
        best = gs[0]
        for g in range(1, N_ROUTE_GROUPS):
            best = jnp.maximum(best, gs[g])
        taken = jnp.zeros((1, tb), F32)
        for g in range(N_ROUTE_GROUPS):
            pick = jnp.where(gs[g] == best, 1.0 - taken, 0.0)
            taken = taken + pick
            keep[g] = keep[g] + pick
            gs[g] = jnp.where(pick > 0.5, neg, gs[g])

    vals = jnp.concatenate(
        [jnp.where(jnp.broadcast_to(keep[g], (per_group, tb)) > 0.5,
                   biased[g * per_group:(g + 1) * per_group, :], NEG_INF) for g in range(N_ROUTE_GROUPS)],
        axis=0)
    eid = lax.broadcasted_iota(jnp.int32, (n_exp, tb), 0).astype(F32)
    onehot = jnp.zeros((n_exp, tb), F32)
    sel_idx = []
    sel_score = []
    for _ in range(TOP_K):
        m = jnp.max(vals, axis=0, keepdims=True)
        first = jnp.min(jnp.where(vals == m, eid, big), axis=0, keepdims=True)
        sel = eid == first
        sel_idx.append(first)
        sel_score.append(jnp.sum(jnp.where(sel, scores, 0.0), axis=0, keepdims=True))
        vals = jnp.where(sel, neg, vals)
        onehot = jnp.where(sel, 1.0, onehot)

    total = sel_score[0]
    for k in range(1, TOP_K):
        total = total + sel_score[k]

    prefix = jnp.dot(onehot.astype(BF16), upper_ref[...], preferred_element_type=F32)
    pos = prefix + jnp.concatenate([base_scr[...]] * lane_rep, axis=1)
    for k in range(TOP_K):
        idx_ref[k:k + 1, :] = sel_idx[k].astype(jnp.int32)
        wts_ref[k:k + 1, :] = sel_score[k] / total * ROUTED_SCALE
        rk = jnp.sum(jnp.where(eid == sel_idx[k], pos, 0.0), axis=0, keepdims=True)
        rank_ref[k:k + 1, :] = rk.astype(jnp.int32)
    base_scr[...] = base_scr[...] + jnp.broadcast_to(jnp.sum(onehot, axis=1, keepdims=True), base_scr.shape)
    cnt_ref[...] = base_scr[...]


def _router(x1, wrt, brt, tb):
    n, d_model = x1.shape
    n_exp = wrt.shape[0]
    upper = jnp.triu(jnp.ones((tb, tb), BF16), k=1)
    tok = lambda i: (0, i)
    return pl.pallas_call(
        _router_kernel,
        grid=(n // tb,),
        in_specs=[pl.BlockSpec((tb, d_model), lambda i: (i, 0)), _const_spec(wrt.shape), _const_spec(brt.shape),
                  _const_spec(upper.shape)],
        out_specs=(pl.BlockSpec((TOP_K, tb), tok), pl.BlockSpec((TOP_K, tb), tok), pl.BlockSpec((TOP_K, tb), tok),
                   pl.BlockSpec((n_exp, LANES), lambda i: (0, 0))),
        out_shape=(jax.ShapeDtypeStruct((TOP_K, n), jnp.int32), jax.ShapeDtypeStruct((TOP_K, n), F32),
                   jax.ShapeDtypeStruct((TOP_K, n), jnp.int32), jax.ShapeDtypeStruct((n_exp, LANES), F32)),
        scratch_shapes=[pltpu.VMEM((n_exp, LANES), F32)],
        compiler_params=_cparams(("arbitrary",)),
        name="router",
    )(x1, wrt, brt, upper)


def _pack_bf16_pairs(x):
    h = x.shape[1] // 2
    xb = x.astype(BF16).astype(F32)
    lo = lax.bitcast_convert_type(xb[:, :h], jnp.uint32) >> 16
    return lo | (lax.bitcast_convert_type(xb[:, h:], jnp.uint32) & jnp.uint32(0xFFFF0000))


def _unpack_lo(w):
    return lax.bitcast_convert_type(w << 16, F32)


def _unpack_hi(w):
    return lax.bitcast_convert_type(w & jnp.uint32(0xFFFF0000), F32)


def _dest_kernel(cstart_ref, idx_ref, rank_ref, dest_ref):
    idx = idx_ref[...]
    rank = rank_ref[...]

    def body(e, acc):
        return jnp.where(idx == e, cstart_ref[e] + rank, acc)

    dest_ref[...] = lax.fori_loop(0, cstart_ref.shape[0], body, jnp.zeros_like(rank), unroll=8)


def _dest(cstart, idx_t, rank_t, tb):
    n = idx_t.shape[1]
    tok = lambda i: (0, i)
    return pl.pallas_call(
        _dest_kernel,
        grid=(n // tb,),
        in_specs=[pl.BlockSpec(memory_space=pltpu.SMEM), pl.BlockSpec((TOP_K, tb), tok),
                  pl.BlockSpec((TOP_K, tb), tok)],
        out_specs=pl.BlockSpec((TOP_K, tb), tok),
        out_shape=jax.ShapeDtypeStruct((TOP_K, n), jnp.int32),
        compiler_params=_cparams(("arbitrary",)),
        name="slot_index",
    )(cstart, idx_t, rank_t)


def _dispatch_kernel(dest_ref, x_ref, xs_ref, buf, sem):
    tb = x_ref.shape[0]
    words = _pack_bf16_pairs(x_ref[...])
    for j in range(buf.shape[1]):
        buf[:, j, :] = words[:, j * LANES:(j + 1) * LANES]

    def issue(t, carry):
        for k in range(TOP_K):
            pltpu.make_async_copy(buf.at[pl.ds(t, 1)], xs_ref.at[pl.ds(dest_ref[k, t], 1)], sem).start(priority=k % 2)
        return carry

    lax.fori_loop(0, tb, issue, 0, unroll=2)
    for k in range(TOP_K):
        pltpu.make_async_copy(buf, xs_ref.at[pl.ds(0, tb)], sem).wait()


def _dispatch(dest, x1, n_slots, tb):
    n, d_model = x1.shape
    row_tiles = d_model // 2 // LANES
    return pl.pallas_call(
        _dispatch_kernel,
        grid=(n // tb,),
        in_specs=[pl.BlockSpec((TOP_K, tb), lambda i: (0, i), memory_space=pltpu.SMEM),
                  pl.BlockSpec((tb, d_model), lambda i: (i, 0))],
        out_specs=pl.BlockSpec(memory_space=pl.ANY),
        out_shape=jax.ShapeDtypeStruct((n_slots, row_tiles, LANES), jnp.uint32),
        scratch_shapes=[pltpu.VMEM((tb, row_tiles, LANES), jnp.uint32), pltpu.SemaphoreType.DMA(())],
        compiler_params=_cparams(("arbitrary",)),
        name="dispatch",
    )(dest, x1)


def _expert_kernel(wt_ref, we_ref, lo_ref, hi_ref, xs_ref, wg_ref, wu_ref, wd_ref, os_ref, wgu_scr, wd_scr, *, bm):
    w = pl.program_id(0)
    row_tiles = xs_ref.shape[0] // bm
    f = wg_ref.shape[2]
    prev = jnp.maximum(w - 1, 0)
    lo = lo_ref[w]
    hi = hi_ref[w]
    live = hi > lo
    new_expert = jnp.logical_or(w == 0, we_ref[w] != we_ref[prev])
    new_tile = jnp.logical_or(w == 0, wt_ref[w] != wt_ref[prev])

    @pl.when(jnp.logical_and(live, new_expert))
    def _():
        wgu_scr[:, :f] = wg_ref[0].astype(BF16)
        wgu_scr[:, f:] = wu_ref[0].astype(BF16)
        wd_scr[...] = wd_ref[0].astype(BF16)

    @pl.when(live)
    def _():
        words = [xs_ref[pl.ds(j, bm, stride=row_tiles), :] for j in range(row_tiles)]
        x = jnp.concatenate([_unpack_lo(word).astype(BF16) for word in words]
                            + [_unpack_hi(word).astype(BF16) for word in words], axis=1)
        gu = jnp.dot(x, wgu_scr[...], preferred_element_type=F32)
        h = jax.nn.silu(gu[:, :f]) * gu[:, f:]
        out = _pack_bf16_pairs(jnp.dot(h.astype(BF16), wd_scr[...], preferred_element_type=F32))
        out_tiles = os_ref.shape[0] // bm

        @pl.when(new_tile)
        def _():
            for j in range(out_tiles):
                os_ref[pl.ds(j, bm, stride=out_tiles), :] = out[:, j * LANES:(j + 1) * LANES]

        @pl.when(jnp.logical_not(new_tile))
        def _():
            row = lax.broadcasted_iota(jnp.int32, (bm, LANES), 0)
            mine = jnp.logical_and(row >= lo, row < hi)
            for j in range(out_tiles):
                rows_j = pl.ds(j, bm, stride=out_tiles)
                os_ref[rows_j, :] = jnp.where(mine, out[:, j * LANES:(j + 1) * LANES], os_ref[rows_j, :])


def _experts(wt, we, lo, hi, xs, layer, w_gate, w_up, w_down, bm):
    n_slots, row_tiles, _ = xs.shape
    d_model, f = w_gate.shape[2], w_gate.shape[3]
    xs2d = xs.reshape(n_slots * row_tiles, LANES)
    row_map = lambda w, wt, we, lo, hi: (wt[w], 0)
    w_map = lambda w, wt, we, lo, hi: (layer, we[w], 0, 0)
    return pl.pallas_call(
        functools.partial(_expert_kernel, bm=bm),
        grid_spec=pltpu.PrefetchScalarGridSpec(
            num_scalar_prefetch=4,
            grid=(wt.shape[0],),
            in_specs=[pl.BlockSpec((bm * row_tiles, LANES), row_map), pl.BlockSpec((None, 1, d_model, f), w_map),
                      pl.BlockSpec((None, 1, d_model, f), w_map), pl.BlockSpec((None, 1, f, d_model), w_map)],
            out_specs=pl.BlockSpec((bm * row_tiles, LANES), row_map),
            scratch_shapes=[pltpu.VMEM((d_model, 2 * f), BF16), pltpu.VMEM((f, d_model), BF16)],
        ),
        out_shape=jax.ShapeDtypeStruct((n_slots * row_tiles, LANES), jnp.uint32),
        compiler_params=_cparams(("arbitrary",)),
        name="experts",
    )(wt, we, lo, hi, xs2d, w_gate, w_up, w_down).reshape(n_slots, row_tiles, LANES)


def _work_list(counts, bm, n_tiles, n_work):
    n_exp = counts.shape[0]
    cend = jnp.cumsum(counts)
    cstart = cend - counts
    has = counts > 0
    first_tile = cstart // bm
    n_items = jnp.where(has, (cend - 1) // bm - first_tile + 1, 0)
    wend = jnp.cumsum(n_items)
    woff = wend - n_items
    total = wend[-1]
    w = jnp.arange(n_work, dtype=jnp.int32)
    valid = w < total
    w_eff = jnp.minimum(w, total - 1)
    we = jnp.minimum(jnp.sum((wend[None, :] <= w_eff[:, None]).astype(jnp.int32), axis=1), n_exp - 1)
    onehot = we[:, None] == jnp.arange(n_exp, dtype=jnp.int32)[None, :]
    pick = lambda v: jnp.sum(jnp.where(onehot, v[None, :], 0), axis=1)
    wt = pick(first_tile) + (w_eff - pick(woff))
    lo = jnp.where(valid, jnp.maximum(pick(cstart) - wt * bm, 0), 0)
    hi = jnp.where(valid, jnp.minimum(pick(cend) - wt * bm, bm), 0)
    i32 = lambda v: v.astype(jnp.int32)
    return i32(cstart), i32(wt), i32(we), i32(lo), i32(hi)


def _rows_view(ref):
    rows, rt, lanes = ref.shape
    return ref.reshape(rows * rt, lanes)


def _combine_kernel(dest_ref, x_ref, wts_ref, os_ref, wsgu_ref, wsd_ref, lng_ref, lnb_ref, *rest,
                    alpha, n_prompt_blocks):
    final = n_prompt_blocks is not None
    n_out = 2 if final else 1
    outs, gbufs, sem = rest[:n_out], rest[n_out:n_out + TOP_K], rest[n_out + TOP_K]
    tb, d_model = x_ref.shape
    f = wsd_ref.shape[0]
    rt = gbufs[0].shape[1]

    def issue(t, carry):
        for k in range(TOP_K):
            pltpu.make_async_copy(os_ref.at[pl.ds(dest_ref[k, t], 1)], gbufs[k].at[pl.ds(t, 1)],
                                  sem).start(priority=k % 2)
        return carry

    lax.fori_loop(0, tb, issue, 0, unroll=2)

    x = x_ref[...]
    gu = jnp.dot(x.astype(BF16), wsgu_ref[...], preferred_element_type=F32)
    h = jax.nn.silu(gu[:, :f]) * gu[:, f:]
    shared = jnp.dot(h.astype(BF16), wsd_ref[...], preferred_element_type=F32)

    for k in range(TOP_K):
        pltpu.make_async_copy(os_ref.at[pl.ds(0, tb)], gbufs[k], sem).wait()

    views = [_rows_view(g) for g in gbufs]
    w = wts_ref[...]
    rb_rows = 4 * SUBLANES
    blocks = []
    for rb in range(tb // rb_rows):
        acc_lo = [None] * rt
        acc_hi = [None] * rt
        for k in range(TOP_K):
            wk = jnp.broadcast_to(w[rb * rb_rows:(rb + 1) * rb_rows, k:k + 1], (rb_rows, LANES))
            for j in range(rt):
                word = views[k][pl.ds(rb * rb_rows * rt + j, rb_rows, stride=rt), :]
                lo = _unpack_lo(word) * wk
                hi = _unpack_hi(word) * wk
                acc_lo[j] = lo if k == 0 else acc_lo[j] + lo
                acc_hi[j] = hi if k == 0 else acc_hi[j] + hi
        blocks.append(jnp.concatenate(acc_lo + acc_hi, axis=1))
    routed = jnp.concatenate(blocks, axis=0)
    res = alpha * x + (routed + shared)
    mu = jnp.mean(res, axis=-1, keepdims=True)
    xc = res - mu
    var = jnp.mean(xc * xc, axis=-1, keepdims=True)
    y = xc * lax.rsqrt(var + LN_EPS) * lng_ref[...] + lnb_ref[...]
    if not final:
        outs[0][...] = y
        return

    slab = rest[n_out + TOP_K + 1]
    tq = tb // SEQS_PER_GROUP
    for j in range(d_model // LANES):
        slab[j] = y[:, j * LANES:(j + 1) * LANES]

    def write(o_ref):
        for b in range(SEQS_PER_GROUP):
            for j in range(d_model // LANES):
                o_ref[b, :, j * LANES:(j + 1) * LANES] = slab[j, pl.ds(b, tq, stride=SEQS_PER_GROUP), :]

    i = pl.program_id(0)

    @pl.when(i < n_prompt_blocks)
    def _():
        write(outs[0])

    @pl.when(i >= n_prompt_blocks)
    def _():
        write(outs[1])


def _combine(dest, x1, wts_nk, os_, wsgu, wsd, lng, lnb, alpha, tb, final_shapes=None):
    n, d_model = x1.shape
    rt = os_.shape[1]
    scratch = [pltpu.VMEM((tb, rt, LANES), jnp.uint32) for _ in range(TOP_K)] + [pltpu.SemaphoreType.DMA(())]
    if final_shapes is None:
        n_prompt_blocks = None
        out_specs = pl.BlockSpec((tb, d_model), lambda i: (i, 0))
        out_shape = jax.ShapeDtypeStruct((n, d_model), F32)
    else:
        batch, seq, dec_batch, dec_seq = final_shapes
        tq = tb // SEQS_PER_GROUP
        assert dec_seq == tq and seq % tq == 0
        wpg = seq // tq
        n_prompt_blocks = batch * seq // tb
        last = n_prompt_blocks - 1
        out_specs = (
            pl.BlockSpec((None, SEQS_PER_GROUP, tq, d_model),
                         lambda i: (jnp.minimum(i, last) // wpg, 0, jnp.minimum(i, last) % wpg, 0)),
            pl.BlockSpec((None, SEQS_PER_GROUP, tq, d_model), lambda i: (jnp.maximum(i - n_prompt_blocks, 0), 0, 0, 0)),
        )
        out_shape = (jax.ShapeDtypeStruct((batch // SEQS_PER_GROUP, SEQS_PER_GROUP, seq, d_model), F32),
                     jax.ShapeDtypeStruct((dec_batch // SEQS_PER_GROUP, SEQS_PER_GROUP, dec_seq, d_model), F32))
        scratch.append(pltpu.VMEM((d_model // LANES, tb, LANES), F32))
    return pl.pallas_call(
        functools.partial(_combine_kernel, alpha=alpha, n_prompt_blocks=n_prompt_blocks),
        grid=(n // tb,),
        in_specs=[pl.BlockSpec((TOP_K, tb), lambda i: (0, i), memory_space=pltpu.SMEM),
                  pl.BlockSpec((tb, d_model), lambda i: (i, 0)),
                  pl.BlockSpec((tb, TOP_K), lambda i: (i, 0)),
                  pl.BlockSpec(memory_space=pl.ANY),
                  _const_spec(wsgu.shape), _const_spec(wsd.shape), _const_spec(lng.shape), _const_spec(lnb.shape)],
        out_specs=out_specs,
        out_shape=out_shape,
        scratch_shapes=scratch,
        compiler_params=_cparams(("arbitrary",)),
        name="combine" if final_shapes is None else "combine_out",
    )(dest, x1, wts_nk, os_, wsgu, wsd, lng, lnb)


def _to_time_major(x):
    b, t, d = x.shape
    return x.reshape(b // SEQS_PER_GROUP, SEQS_PER_GROUP, t, d).transpose(0, 2, 1, 3).reshape(b * t, d)


def _from_time_major(x, b, t):
    d = x.shape[-1]
    return x.reshape(b // SEQS_PER_GROUP, t, SEQS_PER_GROUP, d).transpose(0, 2, 1, 3).reshape(b, t, d)


def _block_diag_in(bbt, n_gb):
    g, c, p = bbt.shape
    gl = g // n_gb
    eye = jnp.eye(gl, dtype=bbt.dtype)
    return jnp.einsum("jgcp,gh->jgchp", bbt.reshape(n_gb, gl, c, p), eye).reshape(n_gb, gl * c, gl * p)


def _block_diag_out(cm, n_gb):
    g, c, p = cm.shape
    gl = g // n_gb
    eye = jnp.eye(gl, dtype=cm.dtype)
    return jnp.einsum("jgcp,gh->jhpgc", cm.reshape(n_gb, gl, c, p), eye).reshape(n_gb, gl * p, gl * c)


def kernel(x_prompt, x_sample, cache_k, cache_v, state_ssm_re, state_ssm_im, rel_bias, w_in, attn_sinks, w_attn_proj, lam_re, lam_im, log_dt, b_re, b_im, c_re, c_im, d_skip, w_glu, w_out, ln1_g, ln1_b, w_router, b_router, w_gate, w_up, w_down, ws_gate, ws_up, ws_down, ln2_g, ln2_b):
    batch, seq, d_model = x_prompt.shape
    dec_batch, dec_seq, _ = x_sample.shape
    depth = w_in.shape[0]
    win = cache_k.shape[2]
    n_q_heads = attn_sinks.shape[1]
    q_w = n_q_heads * HEAD_DIM
    kv_w = N_KV_HEADS * HEAD_DIM
    d_ssm = d_skip.shape[1]
    n_groups_ssm, state_dim = lam_re.shape[1], lam_re.shape[2]
    gp = n_groups_ssm * state_dim
    n_exp = w_router.shape[2]
    alpha = (2 * depth) ** 0.25
    assert win == WINDOW and seq % CHUNK == 0 and dec_seq <= CHUNK
    assert batch % SEQS_PER_GROUP == 0 and dec_batch % SEQS_PER_GROUP == 0
    qch = (PAST_LEN + np.arange(dec_seq)) // CHUNK
    kch = np.concatenate([PAST_LEN - win + np.arange(win), PAST_LEN + np.arange(dec_seq)]) // CHUNK
    assert np.all((kch[None, :] >= qch[:, None] - WINDOW // CHUNK) & (kch[None, :] <= qch[:, None]))

    n_p = batch * seq
    n_s = dec_batch * dec_seq
    n_tok = n_p + n_s
    n_gb = n_groups_ssm // SSM_GROUPS_PER_BATCH

    bias = _rel_bias_table(rel_bias)
    a_re, a_im, bbt_re, bbt_im = _discretize(lam_re, lam_im, log_dt, b_re, b_im)

    x_all = jnp.concatenate([_to_time_major(x_prompt.astype(F32)), _to_time_major(x_sample.astype(F32))], axis=0)
    zeros_kv = jnp.zeros((batch, win, kv_w), F32)
    zeros_h = jnp.zeros((batch, 2 * gp), F32)

    tb_route = 512 if n_tok % 512 == 0 else 256
    tb_move = 256
    bm = 512
    n_slots = n_tok * TOP_K
    assert n_slots % bm == 0
    n_tiles = n_slots // bm
    n_work = n_tiles + n_exp - 1

    outs = {k: [] for k in ("kp", "vp", "hp", "ks", "vs", "hs")}
    for l in range(depth):
        wl = w_in[l].astype(BF16)
        lw = dict(
            wqkv=wl[:, :q_w + 2 * kv_w], wu=wl[:, q_w + 2 * kv_w:q_w + 2 * kv_w + d_ssm],
            wgl=wl[:, q_w + 2 * kv_w + d_ssm:], wap=w_attn_proj[l].astype(BF16),
            a=jnp.stack([a_re[l].reshape(gp), a_im[l].reshape(gp)]),
            wb=jnp.concatenate([_block_diag_in(bbt_re[l], n_gb), _block_diag_in(bbt_im[l], n_gb)], axis=2).astype(BF16),
            wc=jnp.concatenate([_block_diag_out(c_re[l].astype(F32), n_gb),
                                -_block_diag_out(c_im[l].astype(F32), n_gb)], axis=1).astype(BF16),
            dskip=d_skip[l].astype(F32).reshape(1, d_ssm), wglu=w_glu[l].astype(BF16), wout=w_out[l].astype(BF16),
            ln1g=ln1_g[l].astype(F32).reshape(1, d_model), ln1b=ln1_b[l].astype(F32).reshape(1, d_model),
        )
        x1, kp, vp, hp = _mixer(x_all, 0, batch // SEQS_PER_GROUP, seq // CHUNK, CHUNK, zeros_kv, zeros_kv, zeros_h,
                                lw, bias, attn_sinks[l], alpha, True)
        hinit = jnp.concatenate([state_ssm_re[l].astype(F32).reshape(dec_batch, gp),
                                 state_ssm_im[l].astype(F32).reshape(dec_batch, gp)], axis=1)
        x1, ks, vs, hs = _mixer(x1, n_p // (SEQS_PER_GROUP * dec_seq), dec_batch // SEQS_PER_GROUP, 1, dec_seq,
                                cache_k[l].astype(F32).reshape(dec_batch, win, kv_w),
                                cache_v[l].astype(F32).reshape(dec_batch, win, kv_w), hinit,
                                lw, bias, attn_sinks[l], alpha, False)
        for name, val in zip(("kp", "vp", "hp", "ks", "vs", "hs"), (kp, vp, hp, ks, vs, hs)):
            outs[name].append(val)

        wrt = w_router[l].astype(BF16).T
        brt = jnp.broadcast_to(b_router[l].astype(F32)[:, None], (n_exp, LANES))
        idx_t, wts_t, rank_t, cnt = _router(x1, wrt, brt, tb_route)
        cstart, wt, we, lo, hi = _work_list(cnt[:, 0].astype(jnp.int32), bm, n_tiles, n_work)
        dest = _dest(cstart, idx_t, rank_t, tb_route)
        xs = _dispatch(dest, x1, n_slots, tb_move)
        os_ = _experts(wt, we, lo, hi, xs, l, w_gate, w_up, w_down, bm)
        wsgu = jnp.concatenate([ws_gate[l], ws_up[l]], axis=1).astype(BF16)
        x_all = _combine(dest, x1, wts_t.T, os_, wsgu, ws_down[l].astype(BF16),
                         ln2_g[l].astype(F32).reshape(1, d_model), ln2_b[l].astype(F32).reshape(1, d_model),
                         alpha, tb_move, final_shapes=(batch, seq, dec_batch, dec_seq) if l == depth - 1 else None)

    y_prompt = x_all[0].reshape(batch, seq, d_model)
    y_sample = x_all[1].reshape(dec_batch, dec_seq, d_model)

    def kv(vals, b):
        return jnp.stack(vals).reshape(depth, b, win, N_KV_HEADS, HEAD_DIM)

    def st(vals, b, part):
        return jnp.stack(vals)[:, :, part * gp:(part + 1) * gp].reshape(depth, b, n_groups_ssm, state_dim)

    return (y_prompt, y_sample, kv(outs["kp"], batch), kv(outs["vp"], batch), st(outs["hp"], batch, 0),
            st(outs["hp"], batch, 1), kv(outs["ks"], dec_batch), kv(outs["vs"], dec_batch),
            st(outs["hs"], dec_batch, 0), st(outs["hs"], dec_batch, 1))
```

```python
import functools
import math

import jax
import jax.numpy as jnp
import numpy as np
from jax import lax
from jax.experimental import pallas as pl
from jax.experimental.pallas import tpu as pltpu

CHUNK = 64
WINDOW = 128
HEAD_DIM = 64
N_KV_HEADS = 2
MAX_DISTANCE = 128
CH_PER_GROUP = 16
STATE_DIM = 64
TOP_K = 8
N_ROUTE_GROUPS = 8
TOPK_GROUPS = 4
ROUTED_SCALE = 2.5
LN_EPS = 1e-5
NEG_INF = -1e30
PAST_LEN = 1024

SUBLANES = 8
LANES = 128
SEQS_PER_GROUP = SUBLANES
SSM_GROUPS_PER_BATCH = LANES // CH_PER_GROUP
VMEM_LIMIT_BYTES = 60 * 1024 * 1024
ATTN_SEQS_PER_ITER = 8
MIXER_ROW_BLOCK = 256

BF16 = jnp.bfloat16
F32 = jnp.float32


def _cparams(sem):
    return pltpu.CompilerParams(dimension_semantics=sem, vmem_limit_bytes=VMEM_LIMIT_BYTES)


def _const_spec(shape):
    nd = len(shape)
    return pl.BlockSpec(shape, lambda *_: (0,) * nd, pipeline_mode=pl.Buffered(1))


def _bias_kernel(bucket_ref, tbl_ref, out_ref):
    n_buckets, n_heads = tbl_ref.shape
    bk = bucket_ref[...]
    for h in range(n_heads):
        acc = jnp.zeros(bk.shape, F32)
        for b in range(n_buckets):
            acc = jnp.where(bk == b, tbl_ref[b, h], acc)
        out_ref[h] = acc


def _rel_bias_table(rel_bias):
    n_buckets, n_heads = rel_bias.shape
    band = WINDOW + CHUNK
    rel = (jnp.arange(band) - WINDOW)[None, :] - jnp.arange(CHUNK)[:, None]
    nb = n_buckets // 2
    max_exact = nb // 2
    n = jnp.abs(rel)
    nf = jnp.maximum(n, 1).astype(F32)
    large = max_exact + (jnp.log(nf / max_exact) / math.log(MAX_DISTANCE / max_exact) * (nb - max_exact)).astype(jnp.int32)
    large = jnp.minimum(large, nb - 1)
    bucket = (jnp.where(rel > 0, nb, 0) + jnp.where(n < max_exact, n, large)).astype(jnp.int32)
    return pl.pallas_call(
        _bias_kernel,
        out_shape=jax.ShapeDtypeStruct((n_heads, CHUNK, band), F32),
        in_specs=[pl.BlockSpec(memory_space=pltpu.VMEM), pl.BlockSpec(memory_space=pltpu.SMEM)],
        out_specs=pl.BlockSpec(memory_space=pltpu.VMEM),
        name="rel_bias_table",
    )(bucket, rel_bias.astype(F32))


def _disc_kernel(lr_ref, li_ref, ldt_ref, br_ref, bi_ref, are_ref, aim_ref, bbr_ref, bbi_ref):
    lr = lr_ref[0]
    li = li_ref[0]
    dt = jnp.exp(ldt_ref[0])
    mag = jnp.exp(lr * dt)
    a_re = mag * jnp.cos(li * dt)
    a_im = mag * jnp.sin(li * dt)
    den = lr * lr + li * li
    f_re = ((a_re - 1.0) * lr + a_im * li) / den
    f_im = (a_im * lr - (a_re - 1.0) * li) / den
    br = br_ref[0]
    bi = bi_ref[0]
    are_ref[0] = a_re
    aim_ref[0] = a_im
    bbr_ref[0] = f_re * br - f_im * bi
    bbi_ref[0] = f_re * bi + f_im * br


def _discretize(lam_re, lam_im, log_dt, b_re, b_im):
    depth, g, p = lam_re.shape
    c = b_re.shape[-1]
    spec_gp = pl.BlockSpec((1, g, 1, p), lambda l: (l, 0, 0, 0))
    spec_g1 = pl.BlockSpec((1, g, 1, 1), lambda l: (l, 0, 0, 0))
    spec_gcp = pl.BlockSpec((1, g, c, p), lambda l: (l, 0, 0, 0))
    return pl.pallas_call(
        _disc_kernel,
        grid=(depth,),
        out_shape=(jax.ShapeDtypeStruct((depth, g, 1, p), F32), jax.ShapeDtypeStruct((depth, g, 1, p), F32),
                   jax.ShapeDtypeStruct((depth, g, c, p), F32), jax.ShapeDtypeStruct((depth, g, c, p), F32)),
        in_specs=[spec_gp, spec_gp, spec_g1, spec_gcp, spec_gcp],
        out_specs=(spec_gp, spec_gp, spec_gcp, spec_gcp),
        name="ssm_discretize",
    )(lam_re.astype(F32).reshape(depth, g, 1, p), lam_im.astype(F32).reshape(depth, g, 1, p),
      log_dt.astype(F32).reshape(depth, g, 1, 1),
      jnp.swapaxes(b_re.astype(F32), -1, -2), jnp.swapaxes(b_im.astype(F32), -1, -2))


def _mixer_kernel(x_ref, kinit_ref, vinit_ref, hinit_ref, wqkv_ref, wu_ref, wgl_ref, bias_ref,
                  wap_ref, a_ref, wb_ref, wc_ref, dskip_ref, wglu_ref, wout_ref, lng_ref, lnb_ref,
                  x1_ref, kout_ref, vout_ref, hout_ref,
                  qkv_scr, attn_scr, kwin, vwin, h_scr, bu_scr, u_scr,
                  *, t_len, alpha, mask_missing_chunks, row_block, seqs_per_iter):
    c = pl.program_id(1)
    n_c = pl.num_programs(1)
    rows, d_model = x_ref.shape
    win = kinit_ref.shape[1]
    key_pad = kwin.shape[1]
    q_slabs = attn_scr.shape[0]
    gq2 = q_slabs // N_KV_HEADS
    gp = a_ref.shape[1]
    d_ssm = u_scr.shape[1]
    n_gb = wb_ref.shape[0]
    gb_state = gp // n_gb
    scale = HEAD_DIM ** -0.5

    @pl.when(c == 0)
    def _():
        kwin[:, 0:win, :] = kinit_ref[...]
        vwin[:, 0:win, :] = vinit_ref[...]
        kwin[:, win + t_len:, :] = jnp.zeros((SEQS_PER_GROUP, key_pad - win - t_len, kwin.shape[2]), F32)
        vwin[:, win + t_len:, :] = jnp.zeros((SEQS_PER_GROUP, key_pad - win - t_len, vwin.shape[2]), F32)
        h_scr[...] = hinit_ref[...]

    xb = x_ref[...].astype(BF16)

    qkv = jnp.dot(xb, wqkv_ref[...], preferred_element_type=F32)
    for j in range(q_slabs + 2):
        qkv_scr[j] = qkv[:, j * LANES:(j + 1) * LANES]

    def append_kv(b, carry):
        seq_rows = pl.ds(b, t_len, stride=SEQS_PER_GROUP)
        kwin[b, win:win + t_len, :] = qkv_scr[q_slabs, seq_rows, :]
        vwin[b, win:win + t_len, :] = qkv_scr[q_slabs + 1, seq_rows, :]
        return carry

    lax.fori_loop(0, SEQS_PER_GROUP, append_kv, 0)

    def attend_group(i, carry):
        chains = [(i * seqs_per_iter + u, hk) for u in range(seqs_per_iter) for hk in range(N_KV_HEADS)]
        scores, values = [], []
        for b, hk in chains:
            seq_rows = pl.ds(b, t_len, stride=SEQS_PER_GROUP)
            parts = []
            for s in range(gq2):
                qs = qkv_scr[hk * gq2 + s, seq_rows, :]
                parts += [qs[:, :HEAD_DIM], qs[:, HEAD_DIM:]]
            q4 = jnp.concatenate(parts, axis=0).astype(BF16)
            kh = kwin[b, :, hk * HEAD_DIM:(hk + 1) * HEAD_DIM].astype(BF16)
            values.append(vwin[b, :, hk * HEAD_DIM:(hk + 1) * HEAD_DIM].astype(BF16))
            scores.append(lax.dot_general(q4, kh, (((1,), (1,)), ((), ())), preferred_element_type=F32))
        probs = []
        for (b, hk), s_ in zip(chains, scores):
            s_ = s_ * scale + bias_ref[hk]
            if mask_missing_chunks:
                key = lax.broadcasted_iota(jnp.int32, s_.shape, 1)
                first_valid = (WINDOW // CHUNK - jnp.minimum(c, WINDOW // CHUNK)) * CHUNK
                s_ = jnp.where(key >= first_valid, s_, NEG_INF)
            m = jnp.max(s_, axis=-1, keepdims=True)
            probs.append(jnp.exp(s_ - m).astype(BF16))
        ones = jnp.ones((key_pad, HEAD_DIM), BF16)
        outs = [jnp.dot(p, vh, preferred_element_type=F32) / jnp.dot(p, ones, preferred_element_type=F32)
                for p, vh in zip(probs, values)]
        for (b, hk), o in zip(chains, outs):
            for s in range(gq2):
                pair = jnp.concatenate([o[(2 * s) * t_len:(2 * s + 1) * t_len],
                                        o[(2 * s + 1) * t_len:(2 * s + 2) * t_len]], axis=1)
                attn_scr[hk * gq2 + s, pl.ds(b, t_len, stride=SEQS_PER_GROUP), :] = pair
        return carry

    lax.fori_loop(0, SEQS_PER_GROUP // seqs_per_iter, attend_group, 0)

    def slide_window(b, carry):
        knext = kwin[b, t_len:t_len + win, :]
        vnext = vwin[b, t_len:t_len + win, :]
        kwin[b, 0:win, :] = knext
        vwin[b, 0:win, :] = vnext
        return carry

    lax.fori_loop(0, SEQS_PER_GROUP, slide_window, 0)

    @pl.when(c == n_c - 1)
    def _():
        kout_ref[...] = kwin[:, 0:win, :]
        vout_ref[...] = vwin[:, 0:win, :]

    u = jnp.dot(xb, wu_ref[...], preferred_element_type=F32)
    u_scr[...] = u
    ch_b = d_ssm // n_gb
    for j in range(n_gb):
        bu = jnp.dot(u[:, j * ch_b:(j + 1) * ch_b].astype(BF16), wb_ref[j], preferred_element_type=F32)
        bu_scr[:, j * gb_state:(j + 1) * gb_state] = bu[:, :gb_state]
        bu_scr[:, gp + j * gb_state:gp + (j + 1) * gb_state] = bu[:, gb_state:]

    scan_lanes = 4 * LANES
    for lc in range(gp // scan_lanes):
        re_l = slice(lc * scan_lanes, (lc + 1) * scan_lanes)
        im_l = slice(gp + lc * scan_lanes, gp + (lc + 1) * scan_lanes)
        a_re = jnp.broadcast_to(a_ref[0:1, re_l], (SEQS_PER_GROUP, scan_lanes))
        a_im = jnp.broadcast_to(a_ref[1:2, re_l], (SEQS_PER_GROUP, scan_lanes))

        def step(t, h, re_l=re_l, im_l=im_l, a_re=a_re, a_im=a_im):
            h_re, h_im = h
            r0 = pl.multiple_of(t * SEQS_PER_GROUP, SEQS_PER_GROUP)
            n_re = a_re * h_re - a_im * h_im + bu_scr[pl.ds(r0, SEQS_PER_GROUP), re_l]
            n_im = a_re * h_im + a_im * h_re + bu_scr[pl.ds(r0, SEQS_PER_GROUP), im_l]
            bu_scr[pl.ds(r0, SEQS_PER_GROUP), re_l] = n_re
            bu_scr[pl.ds(r0, SEQS_PER_GROUP), im_l] = n_im
            return n_re, n_im

        h_re, h_im = lax.fori_loop(0, t_len, step, (h_scr[:, re_l], h_scr[:, im_l]), unroll=8)
        h_scr[:, re_l] = h_re
        h_scr[:, im_l] = h_im

    @pl.when(c == n_c - 1)
    def _():
        hout_ref[...] = h_scr[...]

    for rb in range(rows // row_block):
        r = slice(rb * row_block, (rb + 1) * row_block)
        ys = []
        for j in range(n_gb):
            hcat = jnp.concatenate([bu_scr[r, j * gb_state:(j + 1) * gb_state],
                                    bu_scr[r, gp + j * gb_state:gp + (j + 1) * gb_state]], axis=1)
            ys.append(jnp.dot(hcat.astype(BF16), wc_ref[j], preferred_element_type=F32))
        y = jnp.concatenate(ys, axis=1)
        z = jax.nn.gelu(y + dskip_ref[...] * u_scr[r, :])
        glu = jnp.dot(z.astype(BF16), wglu_ref[...], preferred_element_type=F32)
        br_b = glu[:, :d_model] * jax.nn.sigmoid(glu[:, d_model:])
        attn = jnp.concatenate([attn_scr[s, r, :] for s in range(q_slabs)], axis=1)
        br_a = jnp.dot(attn.astype(BF16), wap_ref[...], preferred_element_type=F32)
        xr = x_ref[r, :]
        gl = jnp.dot(xr.astype(BF16), wgl_ref[...], preferred_element_type=F32)
        mix = jax.nn.sigmoid(gl[:, :d_model]) * br_a + jax.nn.sigmoid(gl[:, d_model:]) * br_b
        res = alpha * xr + jnp.dot(mix.astype(BF16), wout_ref[...], preferred_element_type=F32)
        mu = jnp.mean(res, axis=-1, keepdims=True)
        xc = res - mu
        var = jnp.mean(xc * xc, axis=-1, keepdims=True)
        x1_ref[r, :] = xc * lax.rsqrt(var + LN_EPS) * lng_ref[...] + lnb_ref[...]


def _mixer(x_all, row_offset_blocks, n_groups, n_steps, t_len, kinit, vinit, hinit, lw, bias, sinks, alpha,
           mask_missing_chunks):
    n_total, d_model = x_all.shape
    rows = SEQS_PER_GROUP * t_len
    win = kinit.shape[1]
    kvw = kinit.shape[2]
    gp2 = hinit.shape[1]
    gp = gp2 // 2
    q_slabs = lw["wap"].shape[0] // LANES
    d_ssm = lw["wu"].shape[1]
    band = win + t_len
    gq = bias.shape[0] // N_KV_HEADS
    key_pad = -(-(band + 1) // LANES) * LANES
    sink_col = jnp.broadcast_to(sinks.astype(F32).reshape(N_KV_HEADS * gq, 1, 1), (N_KV_HEADS * gq, t_len, 1))
    dead_cols = jnp.full((N_KV_HEADS * gq, t_len, key_pad - band - 1), NEG_INF, F32)
    bias_s = jnp.concatenate([bias[:, :t_len, :band], sink_col, dead_cols], axis=2)
    bias_s = bias_s.reshape(N_KV_HEADS, gq * t_len, key_pad)
    row_block = min(rows, MIXER_ROW_BLOCK)

    def xmap(g, c):
        return (row_offset_blocks + g * n_steps + c, 0)

    grp3 = lambda g, c: (g, 0, 0)
    grp2 = lambda g, c: (g, 0)
    in_specs = [
        pl.BlockSpec((rows, d_model), xmap),
        pl.BlockSpec((SEQS_PER_GROUP, win, kvw), grp3),
        pl.BlockSpec((SEQS_PER_GROUP, win, kvw), grp3),
        pl.BlockSpec((SEQS_PER_GROUP, gp2), grp2),
        _const_spec(lw["wqkv"].shape), _const_spec(lw["wu"].shape), _const_spec(lw["wgl"].shape),
        _const_spec(bias_s.shape), _const_spec(lw["wap"].shape),
        _const_spec(lw["a"].shape), _const_spec(lw["wb"].shape), _const_spec(lw["wc"].shape),
        _const_spec(lw["dskip"].shape), _const_spec(lw["wglu"].shape), _const_spec(lw["wout"].shape),
        _const_spec(lw["ln1g"].shape), _const_spec(lw["ln1b"].shape),
    ]
    out_specs = (
        pl.BlockSpec((rows, d_model), xmap),
        pl.BlockSpec((SEQS_PER_GROUP, win, kvw), grp3),
        pl.BlockSpec((SEQS_PER_GROUP, win, kvw), grp3),
        pl.BlockSpec((SEQS_PER_GROUP, gp2), grp2),
    )
    n_seq = n_groups * SEQS_PER_GROUP
    out_shape = (
        jax.ShapeDtypeStruct((n_total, d_model), F32),
        jax.ShapeDtypeStruct((n_seq, win, kvw), F32),
        jax.ShapeDtypeStruct((n_seq, win, kvw), F32),
        jax.ShapeDtypeStruct((n_seq, gp2), F32),
    )
    scratch = [
        pltpu.VMEM((q_slabs + 2, rows, LANES), F32),
        pltpu.VMEM((q_slabs, rows, LANES), F32),
        pltpu.VMEM((SEQS_PER_GROUP, key_pad, kvw), F32),
        pltpu.VMEM((SEQS_PER_GROUP, key_pad, kvw), F32),
        pltpu.VMEM((SEQS_PER_GROUP, gp2), F32),
        pltpu.VMEM((rows, gp2), F32),
        pltpu.VMEM((rows, d_ssm), F32),
    ]
    kern = functools.partial(_mixer_kernel, t_len=t_len, alpha=alpha,
                             mask_missing_chunks=mask_missing_chunks, row_block=row_block,
                             seqs_per_iter=ATTN_SEQS_PER_ITER)
    args = [x_all, kinit, vinit, hinit, lw["wqkv"], lw["wu"], lw["wgl"], bias_s, lw["wap"], lw["a"],
            lw["wb"], lw["wc"], lw["dskip"], lw["wglu"], lw["wout"], lw["ln1g"], lw["ln1b"]]
    return pl.pallas_call(
        kern,
        grid=(n_groups, n_steps),
        in_specs=in_specs,
        out_specs=out_specs,
        out_shape=out_shape,
        scratch_shapes=scratch,
        input_output_aliases={0: 0},
        compiler_params=_cparams(("arbitrary", "arbitrary")),
        name="mixer_t%d" % t_len,
    )(*args)


def _router_kernel(x_ref, wrt_ref, brt_ref, upper_ref, idx_ref, wts_ref, rank_ref, cnt_ref, base_scr):
    i = pl.program_id(0)
    n_exp = wrt_ref.shape[0]
    tb = x_ref.shape[0]
    per_group = n_exp // N_ROUTE_GROUPS
    lane_rep = tb // LANES

    @pl.when(i == 0)
    def _():
        base_scr[...] = jnp.zeros_like(base_scr)

    xb = x_ref[...].astype(BF16)
    logits = lax.dot_general(wrt_ref[...], xb, (((1,), (1,)), ((), ())), preferred_element_type=F32)
    scores = jax.nn.sigmoid(logits)
    biased = scores + jnp.concatenate([brt_ref[...]] * lane_rep, axis=1)
    neg = jnp.float32(-jnp.inf)
    big = jnp.float32(2 ** 20)

    gs = []
    member = lax.broadcasted_iota(jnp.int32, (per_group, tb), 0).astype(F32)
    for g in range(N_ROUTE_GROUPS):
        bg = biased[g * per_group:(g + 1) * per_group, :]
        m1 = jnp.max(bg, axis=0, keepdims=True)
        first = jnp.min(jnp.where(bg == m1, member, big), axis=0, keepdims=True)
        m2 = jnp.max(jnp.where(member == first, neg, bg), axis=0, keepdims=True)
        gs.append(m1 + m2)

    keep = [jnp.zeros((1, tb), F32) for _ in range(N_ROUTE_GROUPS)]
    for _ in range(TOPK_GROUPS):
        best = gs[0]
        for g in range(1, N_ROUTE_GROUPS):
            best = jnp.maximum(best, gs[g])
        taken = jnp.zeros((1, tb), F32)
        for g in range(N_ROUTE_GROUPS):
            pick = jnp.where(gs[g] == best, 1.0 - taken, 0.0)
            taken = taken + pick
            keep[g] = keep[g] + pick
            gs[g] = jnp.where(pick > 0.5, neg, gs[g])

    vals = jnp.concatenate(
        [jnp.where(jnp.broadcast_to(keep[g], (per_group, tb)) > 0.5,
                   biased[g * per_group:(g + 1) * per_group, :], NEG_INF) for g in range(N_ROUTE_GROUPS)],
        axis=0)
    eid = lax.broadcasted_iota(jnp.int32, (n_exp, tb), 0).astype(F32)
    onehot = jnp.zeros((n_exp, tb), F32)
    sel_idx = []
    sel_score = []
    for _ in range(TOP_K):
        m = jnp.max(vals, axis=0, keepdims=True)
        first = jnp.min(jnp.where(vals == m, eid, big), axis=0, keepdims=True)
        sel = eid == first
        sel_idx.append(first)
        sel_score.append(jnp.sum(jnp.where(sel, scores, 0.0), axis=0, keepdims=True))
        vals = jnp.where(sel, neg, vals)
        onehot = jnp.where(sel, 1.0, onehot)

    total = sel_score[0]
    for k in range(1, TOP_K):
        total = total + sel_score[k]

    prefix = jnp.dot(onehot.astype(BF16), upper_ref[...], preferred_element_type=F32)
    pos = prefix + jnp.concatenate([base_scr[...]] * lane_rep, axis=1)
    for k in range(TOP_K):
        idx_ref[k:k + 1, :] = sel_idx[k].astype(jnp.int32)
        wts_ref[k:k + 1, :] = sel_score[k] / total * ROUTED_SCALE
        rk = jnp.sum(jnp.where(eid == sel_idx[k], pos, 0.0), axis=0, keepdims=True)
        rank_ref[k:k + 1, :] = rk.astype(jnp.int32)
    base_scr[...] = base_scr[...] + jnp.broadcast_to(jnp.sum(onehot, axis=1, keepdims=True), base_scr.shape)
    cnt_ref[...] = base_scr[...]


def _router(x1, wrt, brt, tb):
    n, d_model = x1.shape
    n_exp = wrt.shape[0]
    upper = jnp.triu(jnp.ones((tb, tb), BF16), k=1)
    tok = lambda i: (0, i)
    return pl.pallas_call(
        _router_kernel,
        grid=(n // tb,),
        in_specs=[pl.BlockSpec((tb, d_model), lambda i: (i, 0)), _const_spec(wrt.shape), _const_spec(brt.shape),
                  _const_spec(upper.shape)],
        out_specs=(pl.BlockSpec((TOP_K, tb), tok), pl.BlockSpec((TOP_K, tb), tok), pl.BlockSpec((TOP_K, tb), tok),
                   pl.BlockSpec((n_exp, LANES), lambda i: (0, 0))),
        out_shape=(jax.ShapeDtypeStruct((TOP_K, n), jnp.int32), jax.ShapeDtypeStruct((TOP_K, n), F32),
                   jax.ShapeDtypeStruct((TOP_K, n), jnp.int32), jax.ShapeDtypeStruct((n_exp, LANES), F32)),
        scratch_shapes=[pltpu.VMEM((n_exp, LANES), F32)],
        compiler_params=_cparams(("arbitrary",)),
        name="router",
    )(x1, wrt, brt, upper)


def _pack_bf16_pairs(x):
    h = x.shape[1] // 2
    xb = x.astype(BF16).astype(F32)
    lo = lax.bitcast_convert_type(xb[:, :h], jnp.uint32) >> 16
    return lo | (lax.bitcast_convert_type(xb[:, h:], jnp.uint32) & jnp.uint32(0xFFFF0000))


def _unpack_lo(w):
    return lax.bitcast_convert_type(w << 16, F32)


def _unpack_hi(w):
    return lax.bitcast_convert_type(w & jnp.uint32(0xFFFF0000), F32)


def _dest_kernel(cstart_ref, idx_ref, rank_ref, dest_ref):
    idx = idx_ref[...]
    rank = rank_ref[...]

    def body(e, acc):
        return jnp.where(idx == e, cstart_ref[e] + rank, acc)

    dest_ref[...] = lax.fori_loop(0, cstart_ref.shape[0], body, jnp.zeros_like(rank), unroll=8)


def _dest(cstart, idx_t, rank_t, tb):
    n = idx_t.shape[1]
    tok = lambda i: (0, i)
    return pl.pallas_call(
        _dest_kernel,
        grid=(n // tb,),
        in_specs=[pl.BlockSpec(memory_space=pltpu.SMEM), pl.BlockSpec((TOP_K, tb), tok),
                  pl.BlockSpec((TOP_K, tb), tok)],
        out_specs=pl.BlockSpec((TOP_K, tb), tok),
        out_shape=jax.ShapeDtypeStruct((TOP_K, n), jnp.int32),
        compiler_params=_cparams(("arbitrary",)),
        name="slot_index",
    )(cstart, idx_t, rank_t)


def _dispatch_kernel(dest_ref, x_ref, xs_ref, buf, sem):
    tb = x_ref.shape[0]
    words = _pack_bf16_pairs(x_ref[...])
    for j in range(buf.shape[1]):
        buf[:, j, :] = words[:, j * LANES:(j + 1) * LANES]

    def issue(t, carry):
        for k in range(TOP_K):
            pltpu.make_async_copy(buf.at[pl.ds(t, 1)], xs_ref.at[pl.ds(dest_ref[k, t], 1)], sem).start(priority=k % 2)
        return carry

    lax.fori_loop(0, tb, issue, 0, unroll=2)
    for k in range(TOP_K):
        pltpu.make_async_copy(buf, xs_ref.at[pl.ds(0, tb)], sem).wait()


def _dispatch(dest, x1, n_slots, tb):
    n, d_model = x1.shape
    row_tiles = d_model // 2 // LANES
    return pl.pallas_call(
        _dispatch_kernel,
        grid=(n // tb,),
        in_specs=[pl.BlockSpec((TOP_K, tb), lambda i: (0, i), memory_space=pltpu.SMEM),
                  pl.BlockSpec((tb, d_model), lambda i: (i, 0))],
        out_specs=pl.BlockSpec(memory_space=pl.ANY),
        out_shape=jax.ShapeDtypeStruct((n_slots, row_tiles, LANES), jnp.uint32),
        scratch_shapes=[pltpu.VMEM((tb, row_tiles, LANES), jnp.uint32), pltpu.SemaphoreType.DMA(())],
        compiler_params=_cparams(("arbitrary",)),
        name="dispatch",
    )(dest, x1)


def _expert_kernel(wt_ref, we_ref, lo_ref, hi_ref, xs_ref, wg_ref, wu_ref, wd_ref, os_ref, wgu_scr, wd_scr, *, bm):
    w = pl.program_id(0)
    row_tiles = xs_ref.shape[0] // bm
    f = wg_ref.shape[2]
    prev = jnp.maximum(w - 1, 0)
    lo = lo_ref[w]
    hi = hi_ref[w]
    live = hi > lo
    new_expert = jnp.logical_or(w == 0, we_ref[w] != we_ref[prev])
    new_tile = jnp.logical_or(w == 0, wt_ref[w] != wt_ref[prev])

    @pl.when(jnp.logical_and(live, new_expert))
    def _():
        wgu_scr[:, :f] = wg_ref[0].astype(BF16)
        wgu_scr[:, f:] = wu_ref[0].astype(BF16)
        wd_scr[...] = wd_ref[0].astype(BF16)

    @pl.when(live)
    def _():
        words = [xs_ref[pl.ds(j, bm, stride=row_tiles), :] for j in range(row_tiles)]
        x = jnp.concatenate([_unpack_lo(word).astype(BF16) for word in words]
                            + [_unpack_hi(word).astype(BF16) for word in words], axis=1)
        gu = jnp.dot(x, wgu_scr[...], preferred_element_type=F32)
        h = jax.nn.silu(gu[:, :f]) * gu[:, f:]
        out = _pack_bf16_pairs(jnp.dot(h.astype(BF16), wd_scr[...], preferred_element_type=F32))
        out_tiles = os_ref.shape[0] // bm

        @pl.when(new_tile)
        def _():
            for j in range(out_tiles):
                os_ref[pl.ds(j, bm, stride=out_tiles), :] = out[:, j * LANES:(j + 1) * LANES]

        @pl.when(jnp.logical_not(new_tile))
        def _():
            row = lax.broadcasted_iota(jnp.int32, (bm, LANES), 0)
            mine = jnp.logical_and(row >= lo, row < hi)
            for j in range(out_tiles):
                rows_j = pl.ds(j, bm, stride=out_tiles)
                os_ref[rows_j, :] = jnp.where(mine, out[:, j * LANES:(j + 1) * LANES], os_ref[rows_j, :])


def _experts(wt, we, lo, hi, xs, layer, w_gate, w_up, w_down, bm):
    n_slots, row_tiles, _ = xs.shape
    d_model, f = w_gate.shape[2], w_gate.shape[3]
    xs2d = xs.reshape(n_slots * row_tiles, LANES)
    row_map = lambda w, wt, we, lo, hi: (wt[w], 0)
    w_map = lambda w, wt, we, lo, hi: (layer, we[w], 0, 0)
    return pl.pallas_call(
        functools.partial(_expert_kernel, bm=bm),
        grid_spec=pltpu.PrefetchScalarGridSpec(
            num_scalar_prefetch=4,
            grid=(wt.shape[0],),
            in_specs=[pl.BlockSpec((bm * row_tiles, LANES), row_map), pl.BlockSpec((None, 1, d_model, f), w_map),
                      pl.BlockSpec((None, 1, d_model, f), w_map), pl.BlockSpec((None, 1, f, d_model), w_map)],
            out_specs=pl.BlockSpec((bm * row_tiles, LANES), row_map),
            scratch_shapes=[pltpu.VMEM((d_model, 2 * f), BF16), pltpu.VMEM((f, d_model), BF16)],
        ),
        out_shape=jax.ShapeDtypeStruct((n_slots * row_tiles, LANES), jnp.uint32),
        compiler_params=_cparams(("arbitrary",)),
        name="experts",
    )(wt, we, lo, hi, xs2d, w_gate, w_up, w_down).reshape(n_slots, row_tiles, LANES)


def _work_list(counts, bm, n_tiles, n_work):
    n_exp = counts.shape[0]
    cend = jnp.cumsum(counts)
    cstart = cend - counts
    has = counts > 0
    first_tile = cstart // bm
    n_items = jnp.where(has, (cend - 1) // bm - first_tile + 1, 0)
    wend = jnp.cumsum(n_items)
    woff = wend - n_items
    total = wend[-1]
    w = jnp.arange(n_work, dtype=jnp.int32)
    valid = w < total
    w_eff = jnp.minimum(w, total - 1)
    we = jnp.minimum(jnp.sum((wend[None, :] <= w_eff[:, None]).astype(jnp.int32), axis=1), n_exp - 1)
    onehot = we[:, None] == jnp.arange(n_exp, dtype=jnp.int32)[None, :]
    pick = lambda v: jnp.sum(jnp.where(onehot, v[None, :], 0), axis=1)
    wt = pick(first_tile) + (w_eff - pick(woff))
    lo = jnp.where(valid, jnp.maximum(pick(cstart) - wt * bm, 0), 0)
    hi = jnp.where(valid, jnp.minimum(pick(cend) - wt * bm, bm), 0)
    i32 = lambda v: v.astype(jnp.int32)
    return i32(cstart), i32(wt), i32(we), i32(lo), i32(hi)


def _rows_view(ref):
    rows, rt, lanes = ref.shape
    return ref.reshape(rows * rt, lanes)


def _combine_kernel(dest0_ref, destn_ref, x_ref, wts_ref, os_ref, wsgu_ref, wsd_ref, lng_ref, lnb_ref, *rest,
                    alpha, n_prompt_blocks):
    final = n_prompt_blocks is not None
    n_out = 2 if final else 1
    outs, gbufs, sems = rest[:n_out], rest[n_out:n_out + TOP_K], rest[n_out + TOP_K]
    tb, d_model = x_ref.shape
    f = wsd_ref.shape[0]
    rt = gbufs[0].shape[1]
    i = pl.program_id(0)
    slot = lax.rem(i, 2)
    nslot = 1 - slot

    def row_copy(dref, k, t, s):
        return pltpu.make_async_copy(os_ref.at[pl.ds(dref[k, t], 1)], gbufs[k].at[pl.ds(s * tb + t, 1)], sems.at[s])

    def slot_copy(k, s):
        return pltpu.make_async_copy(os_ref.at[pl.ds(0, tb)], gbufs[k].at[pl.ds(s * tb, tb)], sems.at[s])

    @pl.when(i == 0)
    def _():
        def issue(t, carry):
            for k in range(TOP_K):
                row_copy(dest0_ref, k, t, 0).start(priority=k % 2)
            return carry

        lax.fori_loop(0, tb, issue, 0, unroll=2)

    for k in range(TOP_K):
        slot_copy(k, slot).wait()

    views = [_rows_view(g) for g in gbufs]
    w = wts_ref[...]
    rb_rows = 4 * SUBLANES
    units = [(k, j) for k in range(TOP_K) for j in range(rt)]
    view_base = slot * (tb * rt)
    blocks = []
    for rb in range(tb // rb_rows):
        acc_lo = [None] * rt
        acc_hi = [None] * rt
        wk = None
        for tt in range(rb_rows):
            t = rb * rb_rows + tt
            for k in range(TOP_K):
                row_copy(destn_ref, k, t, nslot).start(priority=k % 2)
            for k, j in units[tt * len(units) // rb_rows:(tt + 1) * len(units) // rb_rows]:
                if j == 0:
                    wk = jnp.broadcast_to(w[rb * rb_rows:(rb + 1) * rb_rows, k:k + 1], (rb_rows, LANES))
                word = views[k][pl.ds(view_base + rb * rb_rows * rt + j, rb_rows, stride=rt), :]
                lo = _unpack_lo(word) * wk
                hi = _unpack_hi(word) * wk
                acc_lo[j] = lo if k == 0 else acc_lo[j] + lo
                acc_hi[j] = hi if k == 0 else acc_hi[j] + hi
        blocks.append(jnp.concatenate(acc_lo + acc_hi, axis=1))
    routed = jnp.concatenate(blocks, axis=0)

    @pl.when(i == pl.num_programs(0) - 1)
    def _():
        for k in range(TOP_K):
            slot_copy(k, nslot).wait()

    x = x_ref[...]
    gu = jnp.dot(x.astype(BF16), wsgu_ref[...], preferred_element_type=F32)
    h = jax.nn.silu(gu[:, :f]) * gu[:, f:]
    shared = jnp.dot(h.astype(BF16), wsd_ref[...], preferred_element_type=F32)
    res = alpha * x + (routed + shared)
    mu = jnp.mean(res, axis=-1, keepdims=True)
    xc = res - mu
    var = jnp.mean(xc * xc, axis=-1, keepdims=True)
    y = xc * lax.rsqrt(var + LN_EPS) * lng_ref[...] + lnb_ref[...]
    if not final:
        outs[0][...] = y
        return

    slab = rest[n_out + TOP_K + 1]
    tq = tb // SEQS_PER_GROUP
    for j in range(d_model // LANES):
        slab[j] = y[:, j * LANES:(j + 1) * LANES]

    def write(o_ref):
        for b in range(SEQS_PER_GROUP):
            for j in range(d_model // LANES):
                o_ref[b, :, j * LANES:(j + 1) * LANES] = slab[j, pl.ds(b, tq, stride=SEQS_PER_GROUP), :]

    i = pl.program_id(0)

    @pl.when(i < n_prompt_blocks)
    def _():
        write(outs[0])

    @pl.when(i >= n_prompt_blocks)
    def _():
        write(outs[1])


def _combine(dest, x1, wts_nk, os_, wsgu, wsd, lng, lnb, alpha, tb, final_shapes=None):
    n, d_model = x1.shape
    rt = os_.shape[1]
    scratch = [pltpu.VMEM((2 * tb, rt, LANES), jnp.uint32) for _ in range(TOP_K)] + [pltpu.SemaphoreType.DMA((2,))]
    last_block = n // tb - 1
    if final_shapes is None:
        n_prompt_blocks = None
        out_specs = pl.BlockSpec((tb, d_model), lambda i: (i, 0))
        out_shape = jax.ShapeDtypeStruct((n, d_model), F32)
    else:
        batch, seq, dec_batch, dec_seq = final_shapes
        tq = tb // SEQS_PER_GROUP
        assert dec_seq == tq and seq % tq == 0
        wpg = seq // tq
        n_prompt_blocks = batch * seq // tb
        last = n_prompt_blocks - 1
        out_specs = (
            pl.BlockSpec((None, SEQS_PER_GROUP, tq, d_model),
                         lambda i: (jnp.minimum(i, last) // wpg, 0, jnp.minimum(i, last) % wpg, 0)),
            pl.BlockSpec((None, SEQS_PER_GROUP, tq, d_model), lambda i: (jnp.maximum(i - n_prompt_blocks, 0), 0, 0, 0)),
        )
        out_shape = (jax.ShapeDtypeStruct((batch // SEQS_PER_GROUP, SEQS_PER_GROUP, seq, d_model), F32),
                     jax.ShapeDtypeStruct((dec_batch // SEQS_PER_GROUP, SEQS_PER_GROUP, dec_seq, d_model), F32))
        scratch.append(pltpu.VMEM((d_model // LANES, tb, LANES), F32))
    return pl.pallas_call(
        functools.partial(_combine_kernel, alpha=alpha, n_prompt_blocks=n_prompt_blocks),
        grid=(n // tb,),
        in_specs=[pl.BlockSpec((TOP_K, tb), lambda i: (0, 0), memory_space=pltpu.SMEM),
                  pl.BlockSpec((TOP_K, tb), lambda i: (0, jnp.minimum(i + 1, last_block)), memory_space=pltpu.SMEM),
                  pl.BlockSpec((tb, d_model), lambda i: (i, 0)),
                  pl.BlockSpec((tb, TOP_K), lambda i: (i, 0)),
                  pl.BlockSpec(memory_space=pl.ANY),
                  _const_spec(wsgu.shape), _const_spec(wsd.shape), _const_spec(lng.shape), _const_spec(lnb.shape)],
        out_specs=out_specs,
        out_shape=out_shape,
        scratch_shapes=scratch,
        compiler_params=_cparams(("arbitrary",)),
        name="combine" if final_shapes is None else "combine_out",
    )(dest, dest, x1, wts_nk, os_, wsgu, wsd, lng, lnb)


def _to_time_major(x):
    b, t, d = x.shape
    return x.reshape(b // SEQS_PER_GROUP, SEQS_PER_GROUP, t, d).transpose(0, 2, 1, 3).reshape(b * t, d)


def _from_time_major(x, b, t):
    d = x.shape[-1]
    return x.reshape(b // SEQS_PER_GROUP, t, SEQS_PER_GROUP, d).transpose(0, 2, 1, 3).reshape(b, t, d)


def _block_diag_in(bbt, n_gb):
    g, c, p = bbt.shape
    gl = g // n_gb
    eye = jnp.eye(gl, dtype=bbt.dtype)
    return jnp.einsum("jgcp,gh->jgchp", bbt.reshape(n_gb, gl, c, p), eye).reshape(n_gb, gl * c, gl * p)


def _block_diag_out(cm, n_gb):
    g, c, p = cm.shape
    gl = g // n_gb
    eye = jnp.eye(gl, dtype=cm.dtype)
    return jnp.einsum("jgcp,gh->jhpgc", cm.reshape(n_gb, gl, c, p), eye).reshape(n_gb, gl * p, gl * c)


def kernel(x_prompt, x_sample, cache_k, cache_v, state_ssm_re, state_ssm_im, rel_bias, w_in, attn_sinks, w_attn_proj, lam_re, lam_im, log_dt, b_re, b_im, c_re, c_im, d_skip, w_glu, w_out, ln1_g, ln1_b, w_router, b_router, w_gate, w_up, w_down, ws_gate, ws_up, ws_down, ln2_g, ln2_b):
    batch, seq, d_model = x_prompt.shape
    dec_batch, dec_seq, _ = x_sample.shape
    depth = w_in.shape[0]
    win = cache_k.shape[2]
    n_q_heads = attn_sinks.shape[1]
    q_w = n_q_heads * HEAD_DIM
    kv_w = N_KV_HEADS * HEAD_DIM
    d_ssm = d_skip.shape[1]
    n_groups_ssm, state_dim = lam_re.shape[1], lam_re.shape[2]
    gp = n_groups_ssm * state_dim
    n_exp = w_router.shape[2]
    alpha = (2 * depth) ** 0.25
    assert win == WINDOW and seq % CHUNK == 0 and dec_seq <= CHUNK
    assert batch % SEQS_PER_GROUP == 0 and dec_batch % SEQS_PER_GROUP == 0
    qch = (PAST_LEN + np.arange(dec_seq)) // CHUNK
    kch = np.concatenate([PAST_LEN - win + np.arange(win), PAST_LEN + np.arange(dec_seq)]) // CHUNK
    assert np.all((kch[None, :] >= qch[:, None] - WINDOW // CHUNK) & (kch[None, :] <= qch[:, None]))

    n_p = batch * seq
    n_s = dec_batch * dec_seq
    n_tok = n_p + n_s
    n_gb = n_groups_ssm // SSM_GROUPS_PER_BATCH

    bias = _rel_bias_table(rel_bias)
    a_re, a_im, bbt_re, bbt_im = _discretize(lam_re, lam_im, log_dt, b_re, b_im)

    x_all = jnp.concatenate([_to_time_major(x_prompt.astype(F32)), _to_time_major(x_sample.astype(F32))], axis=0)
    zeros_kv = jnp.zeros((batch, win, kv_w), F32)
    zeros_h = jnp.zeros((batch, 2 * gp), F32)

    tb_route = 512 if n_tok % 512 == 0 else 256
    tb_move = 256
    bm = 512
    n_slots = n_tok * TOP_K
    assert n_slots % bm == 0
    n_tiles = n_slots // bm
    n_work = n_tiles + n_exp - 1

    outs = {k: [] for k in ("kp", "vp", "hp", "ks", "vs", "hs")}
    for l in range(depth):
        wl = w_in[l].astype(BF16)
        lw = dict(
            wqkv=wl[:, :q_w + 2 * kv_w], wu=wl[:, q_w + 2 * kv_w:q_w + 2 * kv_w + d_ssm],
            wgl=wl[:, q_w + 2 * kv_w + d_ssm:], wap=w_attn_proj[l].astype(BF16),
            a=jnp.stack([a_re[l].reshape(gp), a_im[l].reshape(gp)]),
            wb=jnp.concatenate([_block_diag_in(bbt_re[l], n_gb), _block_diag_in(bbt_im[l], n_gb)], axis=2).astype(BF16),
            wc=jnp.concatenate([_block_diag_out(c_re[l].astype(F32), n_gb),
                                -_block_diag_out(c_im[l].astype(F32), n_gb)], axis=1).astype(BF16),
            dskip=d_skip[l].astype(F32).reshape(1, d_ssm), wglu=w_glu[l].astype(BF16), wout=w_out[l].astype(BF16),
            ln1g=ln1_g[l].astype(F32).reshape(1, d_model), ln1b=ln1_b[l].astype(F32).reshape(1, d_model),
        )
        x1, kp, vp, hp = _mixer(x_all, 0, batch // SEQS_PER_GROUP, seq // CHUNK, CHUNK, zeros_kv, zeros_kv, zeros_h,
                                lw, bias, attn_sinks[l], alpha, True)
        hinit = jnp.concatenate([state_ssm_re[l].astype(F32).reshape(dec_batch, gp),
                                 state_ssm_im[l].astype(F32).reshape(dec_batch, gp)], axis=1)
        x1, ks, vs, hs = _mixer(x1, n_p // (SEQS_PER_GROUP * dec_seq), dec_batch // SEQS_PER_GROUP, 1, dec_seq,
                                cache_k[l].astype(F32).reshape(dec_batch, win, kv_w),
                                cache_v[l].astype(F32).reshape(dec_batch, win, kv_w), hinit,
                                lw, bias, attn_sinks[l], alpha, False)
        for name, val in zip(("kp", "vp", "hp", "ks", "vs", "hs"), (kp, vp, hp, ks, vs, hs)):
            outs[name].append(val)

        wrt = w_router[l].astype(BF16).T
        brt = jnp.broadcast_to(b_router[l].astype(F32)[:, None], (n_exp, LANES))
        idx_t, wts_t, rank_t, cnt = _router(x1, wrt, brt, tb_route)
        cstart, wt, we, lo, hi = _work_list(cnt[:, 0].astype(jnp.int32), bm, n_tiles, n_work)
        dest = _dest(cstart, idx_t, rank_t, tb_route)
        xs = _dispatch(dest, x1, n_slots, tb_move)
        os_ = _experts(wt, we, lo, hi, xs, l, w_gate, w_up, w_down, bm)
        wsgu = jnp.concatenate([ws_gate[l], ws_up[l]], axis=1).astype(BF16)
        x_all = _combine(dest, x1, wts_t.T, os_, wsgu, ws_down[l].astype(BF16),
                         ln2_g[l].astype(F32).reshape(1, d_model), ln2_b[l].astype(F32).reshape(1, d_model),
                         alpha, tb_move, final_shapes=(batch, seq, dec_batch, dec_seq) if l == depth - 1 else None)

    y_prompt = x_all[0].reshape(batch, seq, d_model)
    y_sample = x_all[1].reshape(dec_batch, dec_seq, d_model)

    def kv(vals, b):
        return jnp.stack(vals).reshape(depth, b, win, N_KV_HEADS, HEAD_DIM)

    def st(vals, b, part):
        return jnp.stack(vals)[:, :, part * gp:(part + 1) * gp].reshape(depth, b, n_groups_ssm, state_dim)

    return (y_prompt, y_sample, kv(outs["kp"], batch), kv(outs["vp"], batch), st(outs["hp"], batch, 0),
            st(outs["hp"], batch, 1), kv(outs["ks"], dec_batch), kv(outs["vs"], dec_batch),
            st(outs["hs"], dec_batch, 0), st(outs["hs"], dec_batch, 1))
```

```python
import functools
import math

import jax
import jax.numpy as jnp
import numpy as np
from jax import lax
from jax.experimental import pallas as pl
from jax.experimental.pallas import tpu as pltpu

CHUNK = 64
WINDOW = 128
HEAD_DIM = 64
N_KV_HEADS = 2
MAX_DISTANCE = 128
CH_PER_GROUP = 16
STATE_DIM = 64
TOP_K = 8
N_ROUTE_GROUPS = 8
TOPK_GROUPS = 4
ROUTED_SCALE = 2.5
LN_EPS = 1e-5
NEG_INF = -1e30
PAST_LEN = 1024

SUBLANES = 8
LANES = 128
SEQS_PER_GROUP = SUBLANES
SSM_GROUPS_PER_BATCH = LANES // CH_PER_GROUP
VMEM_LIMIT_BYTES = 60 * 1024 * 1024
ATTN_SEQS_PER_ITER = 8
MIXER_ROW_BLOCK = 256

BF16 = jnp.bfloat16
F32 = jnp.float32


def _cparams(sem):
    return pltpu.CompilerParams(dimension_semantics=sem, vmem_limit_bytes=VMEM_LIMIT_BYTES)


def _const_spec(shape):
    nd = len(shape)
    return pl.BlockSpec(shape, lambda *_: (0,) * nd, pipeline_mode=pl.Buffered(1))


def _bias_kernel(bucket_ref, tbl_ref, out_ref):
    n_buckets, n_heads = tbl_ref.shape
    bk = bucket_ref[...]
    for h in range(n_heads):
        acc = jnp.zeros(bk.shape, F32)
        for b in range(n_buckets):
            acc = jnp.where(bk == b, tbl_ref[b, h], acc)
        out_ref[h] = acc


def _rel_bias_table(rel_bias):
    n_buckets, n_heads = rel_bias.shape
    band = WINDOW + CHUNK
    rel = (jnp.arange(band) - WINDOW)[None, :] - jnp.arange(CHUNK)[:, None]
    nb = n_buckets // 2
    max_exact = nb // 2
    n = jnp.abs(rel)
    nf = jnp.maximum(n, 1).astype(F32)
    large = max_exact + (jnp.log(nf / max_exact) / math.log(MAX_DISTANCE / max_exact) * (nb - max_exact)).astype(jnp.int32)
    large = jnp.minimum(large, nb - 1)
    bucket = (jnp.where(rel > 0, nb, 0) + jnp.where(n < max_exact, n, large)).astype(jnp.int32)
    return pl.pallas_call(
        _bias_kernel,
        out_shape=jax.ShapeDtypeStruct((n_heads, CHUNK, band), F32),
        in_specs=[pl.BlockSpec(memory_space=pltpu.VMEM), pl.BlockSpec(memory_space=pltpu.SMEM)],
        out_specs=pl.BlockSpec(memory_space=pltpu.VMEM),
        name="rel_bias_table",
    )(bucket, rel_bias.astype(F32))


def _disc_kernel(lr_ref, li_ref, ldt_ref, br_ref, bi_ref, are_ref, aim_ref, bbr_ref, bbi_ref):
    lr = lr_ref[0]
    li = li_ref[0]
    dt = jnp.exp(ldt_ref[0])
    mag = jnp.exp(lr * dt)
    a_re = mag * jnp.cos(li * dt)
    a_im = mag * jnp.sin(li * dt)
    den = lr * lr + li * li
    f_re = ((a_re - 1.0) * lr + a_im * li) / den
    f_im = (a_im * lr - (a_re - 1.0) * li) / den
    br = br_ref[0]
    bi = bi_ref[0]
    are_ref[0] = a_re
    aim_ref[0] = a_im
    bbr_ref[0] = f_re * br - f_im * bi
    bbi_ref[0] = f_re * bi + f_im * br


def _discretize(lam_re, lam_im, log_dt, b_re, b_im):
    depth, g, p = lam_re.shape
    c = b_re.shape[-1]
    spec_gp = pl.BlockSpec((1, g, 1, p), lambda l: (l, 0, 0, 0))
    spec_g1 = pl.BlockSpec((1, g, 1, 1), lambda l: (l, 0, 0, 0))
    spec_gcp = pl.BlockSpec((1, g, c, p), lambda l: (l, 0, 0, 0))
    return pl.pallas_call(
        _disc_kernel,
        grid=(depth,),
        out_shape=(jax.ShapeDtypeStruct((depth, g, 1, p), F32), jax.ShapeDtypeStruct((depth, g, 1, p), F32),
                   jax.ShapeDtypeStruct((depth, g, c, p), F32), jax.ShapeDtypeStruct((depth, g, c, p), F32)),
        in_specs=[spec_gp, spec_gp, spec_g1, spec_gcp, spec_gcp],
        out_specs=(spec_gp, spec_gp, spec_gcp, spec_gcp),
        name="ssm_discretize",
    )(lam_re.astype(F32).reshape(depth, g, 1, p), lam_im.astype(F32).reshape(depth, g, 1, p),
      log_dt.astype(F32).reshape(depth, g, 1, 1),
      jnp.swapaxes(b_re.astype(F32), -1, -2), jnp.swapaxes(b_im.astype(F32), -1, -2))


def _mixer_kernel(x_ref, kinit_ref, vinit_ref, hinit_ref, wqkv_ref, wu_ref, wgl_ref, bias_ref,
                  wap_ref, a_ref, wb_ref, wc_ref, dskip_ref, wglu_ref, wout_ref, lng_ref, lnb_ref,
                  x1_ref, kout_ref, vout_ref, hout_ref,
                  qkv_scr, attn_scr, kwin, vwin, h_scr, bu_scr, u_scr,
                  *, t_len, alpha, mask_missing_chunks, row_block, seqs_per_iter):
    c = pl.program_id(1)
    n_c = pl.num_programs(1)
    rows, d_model = x_ref.shape
    win = kinit_ref.shape[1]
    key_pad = kwin.shape[1]
    q_slabs = attn_scr.shape[0]
    gq2 = q_slabs // N_KV_HEADS
    gp = a_ref.shape[1]
    d_ssm = u_scr.shape[1]
    n_gb = wb_ref.shape[0]
    gb_state = gp // n_gb
    scale = HEAD_DIM ** -0.5

    @pl.when(c == 0)
    def _():
        kwin[:, 0:win, :] = kinit_ref[...]
        vwin[:, 0:win, :] = vinit_ref[...]
        kwin[:, win + t_len:, :] = jnp.zeros((SEQS_PER_GROUP, key_pad - win - t_len, kwin.shape[2]), F32)
        vwin[:, win + t_len:, :] = jnp.zeros((SEQS_PER_GROUP, key_pad - win - t_len, vwin.shape[2]), F32)
        h_scr[...] = hinit_ref[...]

    xb = x_ref[...].astype(BF16)

    qkv = jnp.dot(xb, wqkv_ref[...], preferred_element_type=F32)
    for j in range(q_slabs + 2):
        qkv_scr[j] = qkv[:, j * LANES:(j + 1) * LANES]

    def append_kv(b, carry):
        seq_rows = pl.ds(b, t_len, stride=SEQS_PER_GROUP)
        kwin[b, win:win + t_len, :] = qkv_scr[q_slabs, seq_rows, :]
        vwin[b, win:win + t_len, :] = qkv_scr[q_slabs + 1, seq_rows, :]
        return carry

    lax.fori_loop(0, SEQS_PER_GROUP, append_kv, 0)

    def attend_group(i, carry):
        chains = [(i * seqs_per_iter + u, hk) for u in range(seqs_per_iter) for hk in range(N_KV_HEADS)]
        scores, values = [], []
        for b, hk in chains:
            seq_rows = pl.ds(b, t_len, stride=SEQS_PER_GROUP)
            parts = []
            for s in range(gq2):
                qs = qkv_scr[hk * gq2 + s, seq_rows, :]
                parts += [qs[:, :HEAD_DIM], qs[:, HEAD_DIM:]]
            q4 = jnp.concatenate(parts, axis=0).astype(BF16)
            kh = kwin[b, :, hk * HEAD_DIM:(hk + 1) * HEAD_DIM].astype(BF16)
            values.append(vwin[b, :, hk * HEAD_DIM:(hk + 1) * HEAD_DIM].astype(BF16))
            scores.append(lax.dot_general(q4, kh, (((1,), (1,)), ((), ())), preferred_element_type=F32))
        probs = []
        for (b, hk), s_ in zip(chains, scores):
            s_ = s_ * scale + bias_ref[hk]
            if mask_missing_chunks:
                key = lax.broadcasted_iota(jnp.int32, s_.shape, 1)
                first_valid = (WINDOW // CHUNK - jnp.minimum(c, WINDOW // CHUNK)) * CHUNK
                s_ = jnp.where(key >= first_valid, s_, NEG_INF)
            m = jnp.max(s_, axis=-1, keepdims=True)
            probs.append(jnp.exp(s_ - m).astype(BF16))
        ones = jnp.ones((key_pad, HEAD_DIM), BF16)
        outs = [jnp.dot(p, vh, preferred_element_type=F32) / jnp.dot(p, ones, preferred_element_type=F32)
                for p, vh in zip(probs, values)]
        for (b, hk), o in zip(chains, outs):
            for s in range(gq2):
                pair = jnp.concatenate([o[(2 * s) * t_len:(2 * s + 1) * t_len],
                                        o[(2 * s + 1) * t_len:(2 * s + 2) * t_len]], axis=1)
                attn_scr[hk * gq2 + s, pl.ds(b, t_len, stride=SEQS_PER_GROUP), :] = pair
        return carry

    lax.fori_loop(0, SEQS_PER_GROUP // seqs_per_iter, attend_group, 0)

    def slide_window(b, carry):
        knext = kwin[b, t_len:t_len + win, :]
        vnext = vwin[b, t_len:t_len + win, :]
        kwin[b, 0:win, :] = knext
        vwin[b, 0:win, :] = vnext
        return carry

    lax.fori_loop(0, SEQS_PER_GROUP, slide_window, 0)

    @pl.when(c == n_c - 1)
    def _():
        kout_ref[...] = kwin[:, 0:win, :]
        vout_ref[...] = vwin[:, 0:win, :]

    u = jnp.dot(xb, wu_ref[...], preferred_element_type=F32)
    u_scr[...] = u
    ch_b = d_ssm // n_gb
    for j in range(n_gb):
        bu = jnp.dot(u[:, j * ch_b:(j + 1) * ch_b].astype(BF16), wb_ref[j], preferred_element_type=F32)
        bu_scr[:, j * gb_state:(j + 1) * gb_state] = bu[:, :gb_state]
        bu_scr[:, gp + j * gb_state:gp + (j + 1) * gb_state] = bu[:, gb_state:]

    scan_lanes = 4 * LANES
    for lc in range(gp // scan_lanes):
        re_l = slice(lc * scan_lanes, (lc + 1) * scan_lanes)
        im_l = slice(gp + lc * scan_lanes, gp + (lc + 1) * scan_lanes)
        a_re = jnp.broadcast_to(a_ref[0:1, re_l], (SEQS_PER_GROUP, scan_lanes))
        a_im = jnp.broadcast_to(a_ref[1:2, re_l], (SEQS_PER_GROUP, scan_lanes))

        def step(t, h, re_l=re_l, im_l=im_l, a_re=a_re, a_im=a_im):
            h_re, h_im = h
            r0 = pl.multiple_of(t * SEQS_PER_GROUP, SEQS_PER_GROUP)
            n_re = a_re * h_re - a_im * h_im + bu_scr[pl.ds(r0, SEQS_PER_GROUP), re_l]
            n_im = a_re * h_im + a_im * h_re + bu_scr[pl.ds(r0, SEQS_PER_GROUP), im_l]
            bu_scr[pl.ds(r0, SEQS_PER_GROUP), re_l] = n_re
            bu_scr[pl.ds(r0, SEQS_PER_GROUP), im_l] = n_im
            return n_re, n_im

        h_re, h_im = lax.fori_loop(0, t_len, step, (h_scr[:, re_l], h_scr[:, im_l]), unroll=8)
        h_scr[:, re_l] = h_re
        h_scr[:, im_l] = h_im

    @pl.when(c == n_c - 1)
    def _():
        hout_ref[...] = h_scr[...]

    for rb in range(rows // row_block):
        r = slice(rb * row_block, (rb + 1) * row_block)
        ys = []
        for j in range(n_gb):
            hcat = jnp.concatenate([bu_scr[r, j * gb_state:(j + 1) * gb_state],
                                    bu_scr[r, gp + j * gb_state:gp + (j + 1) * gb_state]], axis=1)
            ys.append(jnp.dot(hcat.astype(BF16), wc_ref[j], preferred_element_type=F32))
        y = jnp.concatenate(ys, axis=1)
        z = jax.nn.gelu(y + dskip_ref[...] * u_scr[r, :])
        glu = jnp.dot(z.astype(BF16), wglu_ref[...], preferred_element_type=F32)
        br_b = glu[:, :d_model] * jax.nn.sigmoid(glu[:, d_model:])
        attn = jnp.concatenate([attn_scr[s, r, :] for s in range(q_slabs)], axis=1)
        br_a = jnp.dot(attn.astype(BF16), wap_ref[...], preferred_element_type=F32)
        xr = x_ref[r, :]
        gl = jnp.dot(xr.astype(BF16), wgl_ref[...], preferred_element_type=F32)
        mix = jax.nn.sigmoid(gl[:, :d_model]) * br_a + jax.nn.sigmoid(gl[:, d_model:]) * br_b
        res = alpha * xr + jnp.dot(mix.astype(BF16), wout_ref[...], preferred_element_type=F32)
        mu = jnp.mean(res, axis=-1, keepdims=True)
        xc = res - mu
        var = jnp.mean(xc * xc, axis=-1, keepdims=True)
        x1_ref[r, :] = xc * lax.rsqrt(var + LN_EPS) * lng_ref[...] + lnb_ref[...]


def _mixer(x_all, row_offset_blocks, n_groups, n_steps, t_len, kinit, vinit, hinit, lw, bias, sinks, alpha,
           mask_missing_chunks):
    n_total, d_model = x_all.shape
    rows = SEQS_PER_GROUP * t_len
    win = kinit.shape[1]
    kvw = kinit.shape[2]
    gp2 = hinit.shape[1]
    gp = gp2 // 2
    q_slabs = lw["wap"].shape[0] // LANES
    d_ssm = lw["wu"].shape[1]
    band = win + t_len
    gq = bias.shape[0] // N_KV_HEADS
    key_pad = -(-(band + 1) // LANES) * LANES
    sink_col = jnp.broadcast_to(sinks.astype(F32).reshape(N_KV_HEADS * gq, 1, 1), (N_KV_HEADS * gq, t_len, 1))
    dead_cols = jnp.full((N_KV_HEADS * gq, t_len, key_pad - band - 1), NEG_INF, F32)
    bias_s = jnp.concatenate([bias[:, :t_len, :band], sink_col, dead_cols], axis=2)
    bias_s = bias_s.reshape(N_KV_HEADS, gq * t_len, key_pad)
    row_block = min(rows, MIXER_ROW_BLOCK)

    def xmap(g, c):
        return (row_offset_blocks + g * n_steps + c, 0)

    grp3 = lambda g, c: (g, 0, 0)
    grp2 = lambda g, c: (g, 0)
    in_specs = [
        pl.BlockSpec((rows, d_model), xmap),
        pl.BlockSpec((SEQS_PER_GROUP, win, kvw), grp3),
        pl.BlockSpec((SEQS_PER_GROUP, win, kvw), grp3),
        pl.BlockSpec((SEQS_PER_GROUP, gp2), grp2),
        _const_spec(lw["wqkv"].shape), _const_spec(lw["wu"].shape), _const_spec(lw["wgl"].shape),
        _const_spec(bias_s.shape), _const_spec(lw["wap"].shape),
        _const_spec(lw["a"].shape), _const_spec(lw["wb"].shape), _const_spec(lw["wc"].shape),
        _const_spec(lw["dskip"].shape), _const_spec(lw["wglu"].shape), _const_spec(lw["wout"].shape),
        _const_spec(lw["ln1g"].shape), _const_spec(lw["ln1b"].shape),
    ]
    out_specs = (
        pl.BlockSpec((rows, d_model), xmap),
        pl.BlockSpec((SEQS_PER_GROUP, win, kvw), grp3),
        pl.BlockSpec((SEQS_PER_GROUP, win, kvw), grp3),
        pl.BlockSpec((SEQS_PER_GROUP, gp2), grp2),
    )
    n_seq = n_groups * SEQS_PER_GROUP
    out_shape = (
        jax.ShapeDtypeStruct((n_total, d_model), F32),
        jax.ShapeDtypeStruct((n_seq, win, kvw), F32),
        jax.ShapeDtypeStruct((n_seq, win, kvw), F32),
        jax.ShapeDtypeStruct((n_seq, gp2), F32),
    )
    scratch = [
        pltpu.VMEM((q_slabs + 2, rows, LANES), F32),
        pltpu.VMEM((q_slabs, rows, LANES), F32),
        pltpu.VMEM((SEQS_PER_GROUP, key_pad, kvw), F32),
        pltpu.VMEM((SEQS_PER_GROUP, key_pad, kvw), F32),
        pltpu.VMEM((SEQS_PER_GROUP, gp2), F32),
        pltpu.VMEM((rows, gp2), F32),
        pltpu.VMEM((rows, d_ssm), F32),
    ]
    kern = functools.partial(_mixer_kernel, t_len=t_len, alpha=alpha,
                             mask_missing_chunks=mask_missing_chunks, row_block=row_block,
                             seqs_per_iter=ATTN_SEQS_PER_ITER)
    args = [x_all, kinit, vinit, hinit, lw["wqkv"], lw["wu"], lw["wgl"], bias_s, lw["wap"], lw["a"],
            lw["wb"], lw["wc"], lw["dskip"], lw["wglu"], lw["wout"], lw["ln1g"], lw["ln1b"]]
    return pl.pallas_call(
        kern,
        grid=(n_groups, n_steps),
        in_specs=in_specs,
        out_specs=out_specs,
        out_shape=out_shape,
        scratch_shapes=scratch,
        input_output_aliases={0: 0},
        compiler_params=_cparams(("arbitrary", "arbitrary")),
        name="mixer_t%d" % t_len,
    )(*args)


def _router_kernel(x_ref, wrt_ref, brt_ref, upper_ref, idx_ref, wts_ref, rank_ref, cnt_ref, base_scr):
    i = pl.program_id(0)
    n_exp = wrt_ref.shape[0]
    tb = x_ref.shape[0]
    per_group = n_exp // N_ROUTE_GROUPS
    lane_rep = tb // LANES

    @pl.when(i == 0)
    def _():
        base_scr[...] = jnp.zeros_like(base_scr)

    xb = x_ref[...].astype(BF16)
    logits = lax.dot_general(wrt_ref[...], xb, (((1,), (1,)), ((), ())), preferred_element_type=F32)
    scores = jax.nn.sigmoid(logits)
    biased = scores + jnp.concatenate([brt_ref[...]] * lane_rep, axis=1)
    neg = jnp.float32(-jnp.inf)
    big = jnp.float32(2 ** 20)

    gs = []
    member = lax.broadcasted_iota(jnp.int32, (per_group, tb), 0).astype(F32)
    for g in range(N_ROUTE_GROUPS):
        bg = biased[g * per_group:(g + 1) * per_group, :]
        m1 = jnp.max(bg, axis=0, keepdims=True)
        first = jnp.min(jnp.where(bg == m1, member, big), axis=0, keepdims=True)
        m2 = jnp.max(jnp.where(member == first, neg, bg), axis=0, keepdims=True)
        gs.append(m1 + m2)

    keep = [jnp.zeros((1, tb), F32) for _ in range(N_ROUTE_GROUPS)]
    for _ in range(TOPK_GROUPS):
        best = gs[0]
        for g in range(1, N_ROUTE_GROUPS):
            best = jnp.maximum(best, gs[g])
        taken = jnp.zeros((1, tb), F32)
        for g in range(N_ROUTE_GROUPS):
            pick = jnp.where(gs[g] == best, 1.0 - taken, 0.0)
            taken = taken + pick
            keep[g] = keep[g] + pick
            gs[g] = jnp.where(pick > 0.5, neg, gs[g])

    vals = jnp.concatenate(
        [jnp.where(jnp.broadcast_to(keep[g], (per_group, tb)) > 0.5,
                   biased[g * per_group:(g + 1) * per_group, :], NEG_INF) for g in range(N_ROUTE_GROUPS)],
        axis=0)
    eid = lax.broadcasted_iota(jnp.int32, (n_exp, tb), 0).astype(F32)
    onehot = jnp.zeros((n_exp, tb), F32)
    sel_idx = []
    sel_score = []
    for _ in range(TOP_K):
        m = jnp.max(vals, axis=0, keepdims=True)
        first = jnp.min(jnp.where(vals == m, eid, big), axis=0, keepdims=True)
        sel = eid == first
        sel_idx.append(first)
        sel_score.append(jnp.sum(jnp.where(sel, scores, 0.0), axis=0, keepdims=True))
        vals = jnp.where(sel, neg, vals)
        onehot = jnp.where(sel, 1.0, onehot)

    total = sel_score[0]
    for k in range(1, TOP_K):
        total = total + sel_score[k]

    prefix = jnp.dot(onehot.astype(BF16), upper_ref[...], preferred_element_type=F32)
    pos = prefix + jnp.concatenate([base_scr[...]] * lane_rep, axis=1)
    for k in range(TOP_K):
        idx_ref[k:k + 1, :] = sel_idx[k].astype(jnp.int32)
        wts_ref[k:k + 1, :] = sel_score[k] / total * ROUTED_SCALE
        rk = jnp.sum(jnp.where(eid == sel_idx[k], pos, 0.0), axis=0, keepdims=True)
        rank_ref[k:k + 1, :] = rk.astype(jnp.int32)
    base_scr[...] = base_scr[...] + jnp.broadcast_to(jnp.sum(onehot, axis=1, keepdims=True), base_scr.shape)
    cnt_ref[...] = base_scr[...]


def _router(x1, wrt, brt, tb):
    n, d_model = x1.shape
    n_exp = wrt.shape[0]
    upper = jnp.triu(jnp.ones((tb, tb), BF16), k=1)
    tok = lambda i: (0, i)
    return pl.pallas_call(
        _router_kernel,
        grid=(n // tb,),
        in_specs=[pl.BlockSpec((tb, d_model), lambda i: (i, 0)), _const_spec(wrt.shape), _const_spec(brt.shape),
                  _const_spec(upper.shape)],
        out_specs=(pl.BlockSpec((TOP_K, tb), tok), pl.BlockSpec((TOP_K, tb), tok), pl.BlockSpec((TOP_K, tb), tok),
                   pl.BlockSpec((n_exp, LANES), lambda i: (0, 0))),
        out_shape=(jax.ShapeDtypeStruct((TOP_K, n), jnp.int32), jax.ShapeDtypeStruct((TOP_K, n), F32),
                   jax.ShapeDtypeStruct((TOP_K, n), jnp.int32), jax.ShapeDtypeStruct((n_exp, LANES), F32)),
        scratch_shapes=[pltpu.VMEM((n_exp, LANES), F32)],
        compiler_params=_cparams(("arbitrary",)),
        name="router",
    )(x1, wrt, brt, upper)


def _pack_bf16_pairs(x):
    h = x.shape[1] // 2
    xb = x.astype(BF16).astype(F32)
    lo = lax.bitcast_convert_type(xb[:, :h], jnp.uint32) >> 16
    return lo | (lax.bitcast_convert_type(xb[:, h:], jnp.uint32) & jnp.uint32(0xFFFF0000))


def _unpack_lo(w):
    return lax.bitcast_convert_type(w << 16, F32)


def _unpack_hi(w):
    return lax.bitcast_convert_type(w & jnp.uint32(0xFFFF0000), F32)


def _dest_kernel(cstart_ref, idx_ref, rank_ref, dest_ref):
    idx = idx_ref[...]
    rank = rank_ref[...]

    def body(e, acc):
        return jnp.where(idx == e, cstart_ref[e] + rank, acc)

    dest_ref[...] = lax.fori_loop(0, cstart_ref.shape[0], body, jnp.zeros_like(rank), unroll=8)


def _dest(cstart, idx_t, rank_t, tb):
    n = idx_t.shape[1]
    tok = lambda i: (0, i)
    return pl.pallas_call(
        _dest_kernel,
        grid=(n // tb,),
        in_specs=[pl.BlockSpec(memory_space=pltpu.SMEM), pl.BlockSpec((TOP_K, tb), tok),
                  pl.BlockSpec((TOP_K, tb), tok)],
        out_specs=pl.BlockSpec((TOP_K, tb), tok),
        out_shape=jax.ShapeDtypeStruct((TOP_K, n), jnp.int32),
        compiler_params=_cparams(("arbitrary",)),
        name="slot_index",
    )(cstart, idx_t, rank_t)


def _dispatch_kernel(dest_ref, x_ref, xs_ref, buf, sem):
    tb = x_ref.shape[0]
    words = _pack_bf16_pairs(x_ref[...])
    for j in range(buf.shape[1]):
        buf[:, j, :] = words[:, j * LANES:(j + 1) * LANES]

    def issue(t, carry):
        for k in range(TOP_K):
            pltpu.make_async_copy(buf.at[pl.ds(t, 1)], xs_ref.at[pl.ds(dest_ref[k, t], 1)], sem).start(priority=k % 2)
        return carry

    lax.fori_loop(0, tb, issue, 0, unroll=2)
    for k in range(TOP_K):
        pltpu.make_async_copy(buf, xs_ref.at[pl.ds(0, tb)], sem).wait()


def _dispatch(dest, x1, n_slots, tb):
    n, d_model = x1.shape
    row_tiles = d_model // 2 // LANES
    return pl.pallas_call(
        _dispatch_kernel,
        grid=(n // tb,),
        in_specs=[pl.BlockSpec((TOP_K, tb), lambda i: (0, i), memory_space=pltpu.SMEM),
                  pl.BlockSpec((tb, d_model), lambda i: (i, 0))],
        out_specs=pl.BlockSpec(memory_space=pl.ANY),
        out_shape=jax.ShapeDtypeStruct((n_slots, row_tiles, LANES), jnp.uint32),
        scratch_shapes=[pltpu.VMEM((tb, row_tiles, LANES), jnp.uint32), pltpu.SemaphoreType.DMA(())],
        compiler_params=_cparams(("arbitrary",)),
        name="dispatch",
    )(dest, x1)


def _expert_kernel(wt_ref, we_ref, wn_ref, ws_ref, lo_ref, hi_ref, xs_ref, wg_hbm, wu_hbm, wd_hbm, os_ref,
                   wgu_scr, wd_scr, stage_g, stage_u, stage_d, wsem, *, bm, layer):
    w = pl.program_id(0)
    row_tiles = xs_ref.shape[0] // bm
    f = stage_g.shape[2]
    prev = jnp.maximum(w - 1, 0)
    lo = lo_ref[w]
    hi = hi_ref[w]
    live = hi > lo
    expert = we_ref[w]
    slot = ws_ref[w]
    new_expert = jnp.logical_or(w == 0, expert != we_ref[prev])
    new_tile = jnp.logical_or(w == 0, wt_ref[w] != wt_ref[prev])

    def weight_copies(e, s):
        return [pltpu.make_async_copy(wg_hbm.at[layer, e], stage_g.at[s], wsem.at[s]),
                pltpu.make_async_copy(wu_hbm.at[layer, e], stage_u.at[s], wsem.at[s]),
                pltpu.make_async_copy(wd_hbm.at[layer, e], stage_d.at[s], wsem.at[s])]

    @pl.when(jnp.logical_and(live, new_expert))
    def _():
        @pl.when(w == 0)
        def _():
            for cp in weight_copies(expert, slot):
                cp.start()

        for cp in weight_copies(expert, slot):
            cp.wait()
        wgu_scr[:, :f] = stage_g[slot].astype(BF16)
        wgu_scr[:, f:] = stage_u[slot].astype(BF16)
        wd_scr[...] = stage_d[slot].astype(BF16)
        nxt = wn_ref[w]

        @pl.when(nxt != expert)
        def _():
            for cp in weight_copies(nxt, 1 - slot):
                cp.start()

    @pl.when(live)
    def _():
        words = [xs_ref[pl.ds(j, bm, stride=row_tiles), :] for j in range(row_tiles)]
        x = jnp.concatenate([_unpack_lo(word).astype(BF16) for word in words]
                            + [_unpack_hi(word).astype(BF16) for word in words], axis=1)
        gu = jnp.dot(x, wgu_scr[...], preferred_element_type=F32)
        h = jax.nn.silu(gu[:, :f]) * gu[:, f:]
        out = _pack_bf16_pairs(jnp.dot(h.astype(BF16), wd_scr[...], preferred_element_type=F32))
        out_tiles = os_ref.shape[0] // bm

        @pl.when(new_tile)
        def _():
            for j in range(out_tiles):
                os_ref[pl.ds(j, bm, stride=out_tiles), :] = out[:, j * LANES:(j + 1) * LANES]

        @pl.when(jnp.logical_not(new_tile))
        def _():
            row = lax.broadcasted_iota(jnp.int32, (bm, LANES), 0)
            mine = jnp.logical_and(row >= lo, row < hi)
            for j in range(out_tiles):
                rows_j = pl.ds(j, bm, stride=out_tiles)
                os_ref[rows_j, :] = jnp.where(mine, out[:, j * LANES:(j + 1) * LANES], os_ref[rows_j, :])


def _experts(wt, we, wn, ws, lo, hi, xs, layer, w_gate, w_up, w_down, bm):
    n_slots, row_tiles, _ = xs.shape
    d_model, f = w_gate.shape[2], w_gate.shape[3]
    xs2d = xs.reshape(n_slots * row_tiles, LANES)
    row_map = lambda w, wt, *others: (wt[w], 0)
    hbm = pl.BlockSpec(memory_space=pl.ANY)
    return pl.pallas_call(
        functools.partial(_expert_kernel, bm=bm, layer=layer),
        grid_spec=pltpu.PrefetchScalarGridSpec(
            num_scalar_prefetch=6,
            grid=(wt.shape[0],),
            in_specs=[pl.BlockSpec((bm * row_tiles, LANES), row_map), hbm, hbm, hbm],
            out_specs=pl.BlockSpec((bm * row_tiles, LANES), row_map),
            scratch_shapes=[pltpu.VMEM((d_model, 2 * f), BF16), pltpu.VMEM((f, d_model), BF16),
                            pltpu.VMEM((2, d_model, f), F32), pltpu.VMEM((2, d_model, f), F32),
                            pltpu.VMEM((2, f, d_model), F32), pltpu.SemaphoreType.DMA((2,))],
        ),
        out_shape=jax.ShapeDtypeStruct((n_slots * row_tiles, LANES), jnp.uint32),
        compiler_params=_cparams(("arbitrary",)),
        name="experts",
    )(wt, we, wn, ws, lo, hi, xs2d, w_gate, w_up, w_down).reshape(n_slots, row_tiles, LANES)


def _work_list(counts, bm, n_tiles, n_work):
    n_exp = counts.shape[0]
    cend = jnp.cumsum(counts)
    cstart = cend - counts
    has = counts > 0
    first_tile = cstart // bm
    n_items = jnp.where(has, (cend - 1) // bm - first_tile + 1, 0)
    wend = jnp.cumsum(n_items)
    woff = wend - n_items
    total = wend[-1]
    w = jnp.arange(n_work, dtype=jnp.int32)
    valid = w < total
    w_eff = jnp.minimum(w, total - 1)
    we = jnp.minimum(jnp.sum((wend[None, :] <= w_eff[:, None]).astype(jnp.int32), axis=1), n_exp - 1)
    onehot = we[:, None] == jnp.arange(n_exp, dtype=jnp.int32)[None, :]
    pick = lambda v: jnp.sum(jnp.where(onehot, v[None, :], 0), axis=1)
    wt = pick(first_tile) + (w_eff - pick(woff))
    lo = jnp.where(valid, jnp.maximum(pick(cstart) - wt * bm, 0), 0)
    hi = jnp.where(valid, jnp.minimum(pick(cend) - wt * bm, bm), 0)
    eids = jnp.arange(n_exp, dtype=jnp.int32)
    later = jnp.logical_and(has[None, :], eids[None, :] > eids[:, None])
    next_present = jnp.min(jnp.where(later, eids[None, :], n_exp), axis=1)
    next_present = jnp.where(next_present == n_exp, eids, next_present)
    ws = (pick(jnp.cumsum(has.astype(jnp.int32))) - 1) % 2
    wn = pick(next_present)
    i32 = lambda v: v.astype(jnp.int32)
    return i32(cstart), i32(wt), i32(we), i32(wn), i32(ws), i32(lo), i32(hi)


def _rows_view(ref):
    rows, rt, lanes = ref.shape
    return ref.reshape(rows * rt, lanes)


def _combine_kernel(dest0_ref, destn_ref, x_ref, wts_ref, os_ref, wsgu_ref, wsd_ref, lng_ref, lnb_ref, *rest,
                    alpha, n_prompt_blocks):
    final = n_prompt_blocks is not None
    n_out = 2 if final else 1
    outs, gbufs, sems = rest[:n_out], rest[n_out:n_out + TOP_K], rest[n_out + TOP_K]
    tb, d_model = x_ref.shape
    f = wsd_ref.shape[0]
    rt = gbufs[0].shape[1]
    i = pl.program_id(0)
    slot = lax.rem(i, 2)
    nslot = 1 - slot

    def row_copy(dref, k, t, s):
        return pltpu.make_async_copy(os_ref.at[pl.ds(dref[k, t], 1)], gbufs[k].at[pl.ds(s * tb + t, 1)], sems.at[s])

    def slot_copy(k, s):
        return pltpu.make_async_copy(os_ref.at[pl.ds(0, tb)], gbufs[k].at[pl.ds(s * tb, tb)], sems.at[s])

    @pl.when(i == 0)
    def _():
        def issue(t, carry):
            for k in range(TOP_K):
                row_copy(dest0_ref, k, t, 0).start(priority=k % 2)
            return carry

        lax.fori_loop(0, tb, issue, 0, unroll=2)

    for k in range(TOP_K):
        slot_copy(k, slot).wait()

    views = [_rows_view(g) for g in gbufs]
    w = wts_ref[...]
    rb_rows = 4 * SUBLANES
    units = [(k, j) for k in range(TOP_K) for j in range(rt)]
    view_base = slot * (tb * rt)
    blocks = []
    for rb in range(tb // rb_rows):
        acc_lo = [None] * rt
        acc_hi = [None] * rt
        wk = None
        for tt in range(rb_rows):
            t = rb * rb_rows + tt
            for k in range(TOP_K):
                row_copy(destn_ref, k, t, nslot).start(priority=k % 2)
            for k, j in units[tt * len(units) // rb_rows:(tt + 1) * len(units) // rb_rows]:
                if j == 0:
                    wk = jnp.broadcast_to(w[rb * rb_rows:(rb + 1) * rb_rows, k:k + 1], (rb_rows, LANES))
                word = views[k][pl.ds(view_base + rb * rb_rows * rt + j, rb_rows, stride=rt), :]
                lo = _unpack_lo(word) * wk
                hi = _unpack_hi(word) * wk
                acc_lo[j] = lo if k == 0 else acc_lo[j] + lo
                acc_hi[j] = hi if k == 0 else acc_hi[j] + hi
        blocks.append(jnp.concatenate(acc_lo + acc_hi, axis=1))
    routed = jnp.concatenate(blocks, axis=0)

    @pl.when(i == pl.num_programs(0) - 1)
    def _():
        for k in range(TOP_K):
            slot_copy(k, nslot).wait()

    x = x_ref[...]
    gu = jnp.dot(x.astype(BF16), wsgu_ref[...], preferred_element_type=F32)
    h = jax.nn.silu(gu[:, :f]) * gu[:, f:]
    shared = jnp.dot(h.astype(BF16), wsd_ref[...], preferred_element_type=F32)
    res = alpha * x + (routed + shared)
    mu = jnp.mean(res, axis=-1, keepdims=True)
    xc = res - mu
    var = jnp.mean(xc * xc, axis=-1, keepdims=True)
    y = xc * lax.rsqrt(var + LN_EPS) * lng_ref[...] + lnb_ref[...]
    if not final:
        outs[0][...] = y
        return

    slab = rest[n_out + TOP_K + 1]
    tq = tb // SEQS_PER_GROUP
    for j in range(d_model // LANES):
        slab[j] = y[:, j * LANES:(j + 1) * LANES]

    def write(o_ref):
        for b in range(SEQS_PER_GROUP):
            for j in range(d_model // LANES):
                o_ref[b, :, j * LANES:(j + 1) * LANES] = slab[j, pl.ds(b, tq, stride=SEQS_PER_GROUP), :]

    i = pl.program_id(0)

    @pl.when(i < n_prompt_blocks)
    def _():
        write(outs[0])

    @pl.when(i >= n_prompt_blocks)
    def _():
        write(outs[1])


def _combine(dest, x1, wts_nk, os_, wsgu, wsd, lng, lnb, alpha, tb, final_shapes=None):
    n, d_model = x1.shape
    rt = os_.shape[1]
    scratch = [pltpu.VMEM((2 * tb, rt, LANES), jnp.uint32) for _ in range(TOP_K)] + [pltpu.SemaphoreType.DMA((2,))]
    last_block = n // tb - 1
    if final_shapes is None:
        n_prompt_blocks = None
        out_specs = pl.BlockSpec((tb, d_model), lambda i: (i, 0))
        out_shape = jax.ShapeDtypeStruct((n, d_model), F32)
    else:
        batch, seq, dec_batch, dec_seq = final_shapes
        tq = tb // SEQS_PER_GROUP
        assert dec_seq == tq and seq % tq == 0
        wpg = seq // tq
        n_prompt_blocks = batch * seq // tb
        last = n_prompt_blocks - 1
        out_specs = (
            pl.BlockSpec((None, SEQS_PER_GROUP, tq, d_model),
                         lambda i: (jnp.minimum(i, last) // wpg, 0, jnp.minimum(i, last) % wpg, 0)),
            pl.BlockSpec((None, SEQS_PER_GROUP, tq, d_model), lambda i: (jnp.maximum(i - n_prompt_blocks, 0), 0, 0, 0)),
        )
        out_shape = (jax.ShapeDtypeStruct((batch // SEQS_PER_GROUP, SEQS_PER_GROUP, seq, d_model), F32),
                     jax.ShapeDtypeStruct((dec_batch // SEQS_PER_GROUP, SEQS_PER_GROUP, dec_seq, d_model), F32))
        scratch.append(pltpu.VMEM((d_model // LANES, tb, LANES), F32))
    return pl.pallas_call(
        functools.partial(_combine_kernel, alpha=alpha, n_prompt_blocks=n_prompt_blocks),
        grid=(n // tb,),
        in_specs=[pl.BlockSpec((TOP_K, tb), lambda i: (0, 0), memory_space=pltpu.SMEM),
                  pl.BlockSpec((TOP_K, tb), lambda i: (0, jnp.minimum(i + 1, last_block)), memory_space=pltpu.SMEM),
                  pl.BlockSpec((tb, d_model), lambda i: (i, 0)),
                  pl.BlockSpec((tb, TOP_K), lambda i: (i, 0)),
                  pl.BlockSpec(memory_space=pl.ANY),
                  _const_spec(wsgu.shape), _const_spec(wsd.shape), _const_spec(lng.shape), _const_spec(lnb.shape)],
        out_specs=out_specs,
        out_shape=out_shape,
        scratch_shapes=scratch,
        compiler_params=_cparams(("arbitrary",)),
        name="combine" if final_shapes is None else "combine_out",
    )(dest, dest, x1, wts_nk, os_, wsgu, wsd, lng, lnb)


def _to_time_major(x):
    b, t, d = x.shape
    return x.reshape(b // SEQS_PER_GROUP, SEQS_PER_GROUP, t, d).transpose(0, 2, 1, 3).reshape(b * t, d)


def _from_time_major(x, b, t):
    d = x.shape[-1]
    return x.reshape(b // SEQS_PER_GROUP, t, SEQS_PER_GROUP, d).transpose(0, 2, 1, 3).reshape(b, t, d)


def _block_diag_in(bbt, n_gb):
    g, c, p = bbt.shape
    gl = g // n_gb
    eye = jnp.eye(gl, dtype=bbt.dtype)
    return jnp.einsum("jgcp,gh->jgchp", bbt.reshape(n_gb, gl, c, p), eye).reshape(n_gb, gl * c, gl * p)


def _block_diag_out(cm, n_gb):
    g, c, p = cm.shape
    gl = g // n_gb
    eye = jnp.eye(gl, dtype=cm.dtype)
    return jnp.einsum("jgcp,gh->jhpgc", cm.reshape(n_gb, gl, c, p), eye).reshape(n_gb, gl * p, gl * c)


def kernel(x_prompt, x_sample, cache_k, cache_v, state_ssm_re, state_ssm_im, rel_bias, w_in, attn_sinks, w_attn_proj, lam_re, lam_im, log_dt, b_re, b_im, c_re, c_im, d_skip, w_glu, w_out, ln1_g, ln1_b, w_router, b_router, w_gate, w_up, w_down, ws_gate, ws_up, ws_down, ln2_g, ln2_b):
    batch, seq, d_model = x_prompt.shape
    dec_batch, dec_seq, _ = x_sample.shape
    depth = w_in.shape[0]
    win = cache_k.shape[2]
    n_q_heads = attn_sinks.shape[1]
    q_w = n_q_heads * HEAD_DIM
    kv_w = N_KV_HEADS * HEAD_DIM
    d_ssm = d_skip.shape[1]
    n_groups_ssm, state_dim = lam_re.shape[1], lam_re.shape[2]
    gp = n_groups_ssm * state_dim
    n_exp = w_router.shape[2]
    alpha = (2 * depth) ** 0.25
    assert win == WINDOW and seq % CHUNK == 0 and dec_seq <= CHUNK
    assert batch % SEQS_PER_GROUP == 0 and dec_batch % SEQS_PER_GROUP == 0
    qch = (PAST_LEN + np.arange(dec_seq)) // CHUNK
    kch = np.concatenate([PAST_LEN - win + np.arange(win), PAST_LEN + np.arange(dec_seq)]) // CHUNK
    assert np.all((kch[None, :] >= qch[:, None] - WINDOW // CHUNK) & (kch[None, :] <= qch[:, None]))

    n_p = batch * seq
    n_s = dec_batch * dec_seq
    n_tok = n_p + n_s
    n_gb = n_groups_ssm // SSM_GROUPS_PER_BATCH

    bias = _rel_bias_table(rel_bias)
    a_re, a_im, bbt_re, bbt_im = _discretize(lam_re, lam_im, log_dt, b_re, b_im)

    x_all = jnp.concatenate([_to_time_major(x_prompt.astype(F32)), _to_time_major(x_sample.astype(F32))], axis=0)
    zeros_kv = jnp.zeros((batch, win, kv_w), F32)
    zeros_h = jnp.zeros((batch, 2 * gp), F32)

    tb_route = 512 if n_tok % 512 == 0 else 256
    tb_move = 256
    bm = 512
    n_slots = n_tok * TOP_K
    assert n_slots % bm == 0
    n_tiles = n_slots // bm
    n_work = n_tiles + n_exp - 1

    outs = {k: [] for k in ("kp", "vp", "hp", "ks", "vs", "hs")}
    for l in range(depth):
        wl = w_in[l].astype(BF16)
        lw = dict(
            wqkv=wl[:, :q_w + 2 * kv_w], wu=wl[:, q_w + 2 * kv_w:q_w + 2 * kv_w + d_ssm],
            wgl=wl[:, q_w + 2 * kv_w + d_ssm:], wap=w_attn_proj[l].astype(BF16),
            a=jnp.stack([a_re[l].reshape(gp), a_im[l].reshape(gp)]),
            wb=jnp.concatenate([_block_diag_in(bbt_re[l], n_gb), _block_diag_in(bbt_im[l], n_gb)], axis=2).astype(BF16),
            wc=jnp.concatenate([_block_diag_out(c_re[l].astype(F32), n_gb),
                                -_block_diag_out(c_im[l].astype(F32), n_gb)], axis=1).astype(BF16),
            dskip=d_skip[l].astype(F32).reshape(1, d_ssm), wglu=w_glu[l].astype(BF16), wout=w_out[l].astype(BF16),
            ln1g=ln1_g[l].astype(F32).reshape(1, d_model), ln1b=ln1_b[l].astype(F32).reshape(1, d_model),
        )
        x1, kp, vp, hp = _mixer(x_all, 0, batch // SEQS_PER_GROUP, seq // CHUNK, CHUNK, zeros_kv, zeros_kv, zeros_h,
                                lw, bias, attn_sinks[l], alpha, True)
        hinit = jnp.concatenate([state_ssm_re[l].astype(F32).reshape(dec_batch, gp),
                                 state_ssm_im[l].astype(F32).reshape(dec_batch, gp)], axis=1)
        x1, ks, vs, hs = _mixer(x1, n_p // (SEQS_PER_GROUP * dec_seq), dec_batch // SEQS_PER_GROUP, 1, dec_seq,
                                cache_k[l].astype(F32).reshape(dec_batch, win, kv_w),
                                cache_v[l].astype(F32).reshape(dec_batch, win, kv_w), hinit,
                                lw, bias, attn_sinks[l], alpha, False)
        for name, val in zip(("kp", "vp", "hp", "ks", "vs", "hs"), (kp, vp, hp, ks, vs, hs)):
            outs[name].append(val)

        wrt = w_router[l].astype(BF16).T
        brt = jnp.broadcast_to(b_router[l].astype(F32)[:, None], (n_exp, LANES))
        idx_t, wts_t, rank_t, cnt = _router(x1, wrt, brt, tb_route)
        cstart, wt, we, wn, wslot, lo, hi = _work_list(cnt[:, 0].astype(jnp.int32), bm, n_tiles, n_work)
        dest = _dest(cstart, idx_t, rank_t, tb_route)
        xs = _dispatch(dest, x1, n_slots, tb_move)
        os_ = _experts(wt, we, wn, wslot, lo, hi, xs, l, w_gate.astype(F32), w_up.astype(F32), w_down.astype(F32), bm)
        wsgu = jnp.concatenate([ws_gate[l], ws_up[l]], axis=1).astype(BF16)
        x_all = _combine(dest, x1, wts_t.T, os_, wsgu, ws_down[l].astype(BF16),
                         ln2_g[l].astype(F32).reshape(1, d_model), ln2_b[l].astype(F32).reshape(1, d_model),
                         alpha, tb_move, final_shapes=(batch, seq, dec_batch, dec_seq) if l == depth - 1 else None)

    y_prompt = x_all[0].reshape(batch, seq, d_model)
    y_sample = x_all[1].reshape(dec_batch, dec_seq, d_model)

    def kv(vals, b):
        return jnp.stack(vals).reshape(depth, b, win, N_KV_HEADS, HEAD_DIM)

    def st(vals, b, part):
        return jnp.stack(vals)[:, :, part * gp:(part + 1) * gp].reshape(depth, b, n_groups_ssm, state_dim)

    return (y_prompt, y_sample, kv(outs["kp"], batch), kv(outs["vp"], batch), st(outs["hp"], batch, 0),
            st(outs["hp"], batch, 1), kv(outs["ks"], dec_batch), kv(outs["vs"], dec_batch),
            st(outs["hs"], dec_batch, 0), st(outs["hs"], dec_batch, 1))
```

```python
import functools
import math

import jax
import jax.numpy as jnp
import numpy as np
from jax import lax
from jax.experimental import pallas as pl
from jax.experimental.pallas import tpu as pltpu

CHUNK = 64
WINDOW = 128
HEAD_DIM = 64
N_KV_HEADS = 2
MAX_DISTANCE = 128
CH_PER_GROUP = 16
STATE_DIM = 64
TOP_K = 8
N_ROUTE_GROUPS = 8
TOPK_GROUPS = 4
ROUTED_SCALE = 2.5
LN_EPS = 1e-5
NEG_INF = -1e30
PAST_LEN = 1024

SUBLANES = 8
LANES = 128
SEQS_PER_GROUP = SUBLANES
SSM_GROUPS_PER_BATCH = LANES // CH_PER_GROUP
VMEM_LIMIT_BYTES = 60 * 1024 * 1024
ATTN_SEQS_PER_ITER = 8
MIXER_ROW_BLOCK = 256

BF16 = jnp.bfloat16
F32 = jnp.float32


def _cparams(sem):
    return pltpu.CompilerParams(dimension_semantics=sem, vmem_limit_bytes=VMEM_LIMIT_BYTES)


def _const_spec(shape):
    nd = len(shape)
    return pl.BlockSpec(shape, lambda *_: (0,) * nd, pipeline_mode=pl.Buffered(1))


def _bias_kernel(bucket_ref, tbl_ref, out_ref):
    n_buckets, n_heads = tbl_ref.shape
    bk = bucket_ref[...]
    for h in range(n_heads):
        acc = jnp.zeros(bk.shape, F32)
        for b in range(n_buckets):
            acc = jnp.where(bk == b, tbl_ref[b, h], acc)
        out_ref[h] = acc


def _rel_bias_table(rel_bias):
    n_buckets, n_heads = rel_bias.shape
    band = WINDOW + CHUNK
    rel = (jnp.arange(band) - WINDOW)[None, :] - jnp.arange(CHUNK)[:, None]
    nb = n_buckets // 2
    max_exact = nb // 2
    n = jnp.abs(rel)
    nf = jnp.maximum(n, 1).astype(F32)
    large = max_exact + (jnp.log(nf / max_exact) / math.log(MAX_DISTANCE / max_exact) * (nb - max_exact)).astype(jnp.int32)
    large = jnp.minimum(large, nb - 1)
    bucket = (jnp.where(rel > 0, nb, 0) + jnp.where(n < max_exact, n, large)).astype(jnp.int32)
    return pl.pallas_call(
        _bias_kernel,
        out_shape=jax.ShapeDtypeStruct((n_heads, CHUNK, band), F32),
        in_specs=[pl.BlockSpec(memory_space=pltpu.VMEM), pl.BlockSpec(memory_space=pltpu.SMEM)],
        out_specs=pl.BlockSpec(memory_space=pltpu.VMEM),
        name="rel_bias_table",
    )(bucket, rel_bias.astype(F32))


def _disc_kernel(lr_ref, li_ref, ldt_ref, br_ref, bi_ref, are_ref, aim_ref, bbr_ref, bbi_ref):
    lr = lr_ref[0]
    li = li_ref[0]
    dt = jnp.exp(ldt_ref[0])
    mag = jnp.exp(lr * dt)
    a_re = mag * jnp.cos(li * dt)
    a_im = mag * jnp.sin(li * dt)
    den = lr * lr + li * li
    f_re = ((a_re - 1.0) * lr + a_im * li) / den
    f_im = (a_im * lr - (a_re - 1.0) * li) / den
    br = br_ref[0]
    bi = bi_ref[0]
    are_ref[0] = a_re
    aim_ref[0] = a_im
    bbr_ref[0] = f_re * br - f_im * bi
    bbi_ref[0] = f_re * bi + f_im * br


def _discretize(lam_re, lam_im, log_dt, b_re, b_im):
    depth, g, p = lam_re.shape
    c = b_re.shape[-1]
    spec_gp = pl.BlockSpec((1, g, 1, p), lambda l: (l, 0, 0, 0))
    spec_g1 = pl.BlockSpec((1, g, 1, 1), lambda l: (l, 0, 0, 0))
    spec_gcp = pl.BlockSpec((1, g, c, p), lambda l: (l, 0, 0, 0))
    return pl.pallas_call(
        _disc_kernel,
        grid=(depth,),
        out_shape=(jax.ShapeDtypeStruct((depth, g, 1, p), F32), jax.ShapeDtypeStruct((depth, g, 1, p), F32),
                   jax.ShapeDtypeStruct((depth, g, c, p), F32), jax.ShapeDtypeStruct((depth, g, c, p), F32)),
        in_specs=[spec_gp, spec_gp, spec_g1, spec_gcp, spec_gcp],
        out_specs=(spec_gp, spec_gp, spec_gcp, spec_gcp),
        name="ssm_discretize",
    )(lam_re.astype(F32).reshape(depth, g, 1, p), lam_im.astype(F32).reshape(depth, g, 1, p),
      log_dt.astype(F32).reshape(depth, g, 1, 1),
      jnp.swapaxes(b_re.astype(F32), -1, -2), jnp.swapaxes(b_im.astype(F32), -1, -2))


def _mixer_kernel(x_ref, kinit_ref, vinit_ref, hinit_ref, wqkv_ref, wu_ref, wgl_ref, bias_ref,
                  wap_ref, a_ref, wb_ref, wc_ref, dskip_ref, wglu_ref, wout_ref, lng_ref, lnb_ref,
                  x1_ref, kout_ref, vout_ref, hout_ref,
                  qkv_scr, attn_scr, kwin, vwin, h_scr, bu_scr, u_scr,
                  *, t_len, alpha, mask_missing_chunks, row_block, seqs_per_iter):
    c = pl.program_id(1)
    n_c = pl.num_programs(1)
    rows, d_model = x_ref.shape
    win = kinit_ref.shape[1]
    key_pad = kwin.shape[1]
    q_slabs = attn_scr.shape[0]
    gq2 = q_slabs // N_KV_HEADS
    gp = a_ref.shape[1]
    d_ssm = u_scr.shape[1]
    n_gb = wb_ref.shape[0]
    gb_state = gp // n_gb
    scale = HEAD_DIM ** -0.5

    @pl.when(c == 0)
    def _():
        kwin[:, 0:win, :] = kinit_ref[...]
        vwin[:, 0:win, :] = vinit_ref[...]
        kwin[:, win + t_len:, :] = jnp.zeros((SEQS_PER_GROUP, key_pad - win - t_len, kwin.shape[2]), F32)
        vwin[:, win + t_len:, :] = jnp.zeros((SEQS_PER_GROUP, key_pad - win - t_len, vwin.shape[2]), F32)
        h_scr[...] = hinit_ref[...]

    xb = x_ref[...].astype(BF16)

    qkv = jnp.dot(xb, wqkv_ref[...], preferred_element_type=F32)
    for j in range(q_slabs + 2):
        qkv_scr[j] = qkv[:, j * LANES:(j + 1) * LANES]

    def append_kv(b, carry):
        seq_rows = pl.ds(b, t_len, stride=SEQS_PER_GROUP)
        kwin[b, win:win + t_len, :] = qkv_scr[q_slabs, seq_rows, :]
        vwin[b, win:win + t_len, :] = qkv_scr[q_slabs + 1, seq_rows, :]
        return carry

    lax.fori_loop(0, SEQS_PER_GROUP, append_kv, 0)

    def attend_group(i, carry):
        chains = [(i * seqs_per_iter + u, hk) for u in range(seqs_per_iter) for hk in range(N_KV_HEADS)]
        scores, values = [], []
        for b, hk in chains:
            seq_rows = pl.ds(b, t_len, stride=SEQS_PER_GROUP)
            parts = []
            for s in range(gq2):
                qs = qkv_scr[hk * gq2 + s, seq_rows, :]
                parts += [qs[:, :HEAD_DIM], qs[:, HEAD_DIM:]]
            q4 = jnp.concatenate(parts, axis=0).astype(BF16)
            kh = kwin[b, :, hk * HEAD_DIM:(hk + 1) * HEAD_DIM].astype(BF16)
            values.append(vwin[b, :, hk * HEAD_DIM:(hk + 1) * HEAD_DIM].astype(BF16))
            scores.append(lax.dot_general(q4, kh, (((1,), (1,)), ((), ())), preferred_element_type=F32))
        probs = []
        for (b, hk), s_ in zip(chains, scores):
            s_ = s_ * scale + bias_ref[hk]
            if mask_missing_chunks:
                key = lax.broadcasted_iota(jnp.int32, s_.shape, 1)
                first_valid = (WINDOW // CHUNK - jnp.minimum(c, WINDOW // CHUNK)) * CHUNK
                s_ = jnp.where(key >= first_valid, s_, NEG_INF)
            m = jnp.max(s_, axis=-1, keepdims=True)
            probs.append(jnp.exp(s_ - m).astype(BF16))
        ones = jnp.ones((key_pad, HEAD_DIM), BF16)
        outs = [jnp.dot(p, vh, preferred_element_type=F32) / jnp.dot(p, ones, preferred_element_type=F32)
                for p, vh in zip(probs, values)]
        for (b, hk), o in zip(chains, outs):
            for s in range(gq2):
                pair = jnp.concatenate([o[(2 * s) * t_len:(2 * s + 1) * t_len],
                                        o[(2 * s + 1) * t_len:(2 * s + 2) * t_len]], axis=1)
                attn_scr[hk * gq2 + s, pl.ds(b, t_len, stride=SEQS_PER_GROUP), :] = pair
        return carry

    lax.fori_loop(0, SEQS_PER_GROUP // seqs_per_iter, attend_group, 0)

    def slide_window(b, carry):
        knext = kwin[b, t_len:t_len + win, :]
        vnext = vwin[b, t_len:t_len + win, :]
        kwin[b, 0:win, :] = knext
        vwin[b, 0:win, :] = vnext
        return carry

    lax.fori_loop(0, SEQS_PER_GROUP, slide_window, 0)

    @pl.when(c == n_c - 1)
    def _():
        kout_ref[...] = kwin[:, 0:win, :]
        vout_ref[...] = vwin[:, 0:win, :]

    u = jnp.dot(xb, wu_ref[...], preferred_element_type=F32)
    u_scr[...] = u
    ch_b = d_ssm // n_gb
    for j in range(n_gb):
        bu = jnp.dot(u[:, j * ch_b:(j + 1) * ch_b].astype(BF16), wb_ref[j], preferred_element_type=F32)
        bu_scr[:, j * gb_state:(j + 1) * gb_state] = bu[:, :gb_state]
        bu_scr[:, gp + j * gb_state:gp + (j + 1) * gb_state] = bu[:, gb_state:]

    scan_lanes = 4 * LANES
    for lc in range(gp // scan_lanes):
        re_l = slice(lc * scan_lanes, (lc + 1) * scan_lanes)
        im_l = slice(gp + lc * scan_lanes, gp + (lc + 1) * scan_lanes)
        a_re = jnp.broadcast_to(a_ref[0:1, re_l], (SEQS_PER_GROUP, scan_lanes))
        a_im = jnp.broadcast_to(a_ref[1:2, re_l], (SEQS_PER_GROUP, scan_lanes))

        def step(t, h, re_l=re_l, im_l=im_l, a_re=a_re, a_im=a_im):
            h_re, h_im = h
            r0 = pl.multiple_of(t * SEQS_PER_GROUP, SEQS_PER_GROUP)
            n_re = a_re * h_re - a_im * h_im + bu_scr[pl.ds(r0, SEQS_PER_GROUP), re_l]
            n_im = a_re * h_im + a_im * h_re + bu_scr[pl.ds(r0, SEQS_PER_GROUP), im_l]
            bu_scr[pl.ds(r0, SEQS_PER_GROUP), re_l] = n_re
            bu_scr[pl.ds(r0, SEQS_PER_GROUP), im_l] = n_im
            return n_re, n_im

        h_re, h_im = lax.fori_loop(0, t_len, step, (h_scr[:, re_l], h_scr[:, im_l]), unroll=8)
        h_scr[:, re_l] = h_re
        h_scr[:, im_l] = h_im

    @pl.when(c == n_c - 1)
    def _():
        hout_ref[...] = h_scr[...]

    for rb in range(rows // row_block):
        r = slice(rb * row_block, (rb + 1) * row_block)
        ys = []
        for j in range(n_gb):
            hcat = jnp.concatenate([bu_scr[r, j * gb_state:(j + 1) * gb_state],
                                    bu_scr[r, gp + j * gb_state:gp + (j + 1) * gb_state]], axis=1)
            ys.append(jnp.dot(hcat.astype(BF16), wc_ref[j], preferred_element_type=F32))
        y = jnp.concatenate(ys, axis=1)
        z = jax.nn.gelu(y + dskip_ref[...] * u_scr[r, :])
        glu = jnp.dot(z.astype(BF16), wglu_ref[...], preferred_element_type=F32)
        br_b = glu[:, :d_model] * jax.nn.sigmoid(glu[:, d_model:])
        attn = jnp.concatenate([attn_scr[s, r, :] for s in range(q_slabs)], axis=1)
        br_a = jnp.dot(attn.astype(BF16), wap_ref[...], preferred_element_type=F32)
        xr = x_ref[r, :]
        gl = jnp.dot(xr.astype(BF16), wgl_ref[...], preferred_element_type=F32)
        mix = jax.nn.sigmoid(gl[:, :d_model]) * br_a + jax.nn.sigmoid(gl[:, d_model:]) * br_b
        res = alpha * xr + jnp.dot(mix.astype(BF16), wout_ref[...], preferred_element_type=F32)
        mu = jnp.mean(res, axis=-1, keepdims=True)
        xc = res - mu
        var = jnp.mean(xc * xc, axis=-1, keepdims=True)
        x1_ref[r, :] = xc * lax.rsqrt(var + LN_EPS) * lng_ref[...] + lnb_ref[...]


def _mixer(x_all, row_offset_blocks, n_groups, n_steps, t_len, kinit, vinit, hinit, lw, bias, sinks, alpha,
           mask_missing_chunks):
    n_total, d_model = x_all.shape
    rows = SEQS_PER_GROUP * t_len
    win = kinit.shape[1]
    kvw = kinit.shape[2]
    gp2 = hinit.shape[1]
    gp = gp2 // 2
    q_slabs = lw["wap"].shape[0] // LANES
    d_ssm = lw["wu"].shape[1]
    band = win + t_len
    gq = bias.shape[0] // N_KV_HEADS
    key_pad = -(-(band + 1) // LANES) * LANES
    sink_col = jnp.broadcast_to(sinks.astype(F32).reshape(N_KV_HEADS * gq, 1, 1), (N_KV_HEADS * gq, t_len, 1))
    dead_cols = jnp.full((N_KV_HEADS * gq, t_len, key_pad - band - 1), NEG_INF, F32)
    bias_s = jnp.concatenate([bias[:, :t_len, :band], sink_col, dead_cols], axis=2)
    bias_s = bias_s.reshape(N_KV_HEADS, gq * t_len, key_pad)
    row_block = min(rows, MIXER_ROW_BLOCK)

    def xmap(g, c):
        return (row_offset_blocks + g * n_steps + c, 0)

    grp3 = lambda g, c: (g, 0, 0)
    grp2 = lambda g, c: (g, 0)
    in_specs = [
        pl.BlockSpec((rows, d_model), xmap),
        pl.BlockSpec((SEQS_PER_GROUP, win, kvw), grp3),
        pl.BlockSpec((SEQS_PER_GROUP, win, kvw), grp3),
        pl.BlockSpec((SEQS_PER_GROUP, gp2), grp2),
        _const_spec(lw["wqkv"].shape), _const_spec(lw["wu"].shape), _const_spec(lw["wgl"].shape),
        _const_spec(bias_s.shape), _const_spec(lw["wap"].shape),
        _const_spec(lw["a"].shape), _const_spec(lw["wb"].shape), _const_spec(lw["wc"].shape),
        _const_spec(lw["dskip"].shape), _const_spec(lw["wglu"].shape), _const_spec(lw["wout"].shape),
        _const_spec(lw["ln1g"].shape), _const_spec(lw["ln1b"].shape),
    ]
    out_specs = (
        pl.BlockSpec((rows, d_model), xmap),
        pl.BlockSpec((SEQS_PER_GROUP, win, kvw), grp3),
        pl.BlockSpec((SEQS_PER_GROUP, win, kvw), grp3),
        pl.BlockSpec((SEQS_PER_GROUP, gp2), grp2),
    )
    n_seq = n_groups * SEQS_PER_GROUP
    out_shape = (
        jax.ShapeDtypeStruct((n_total, d_model), F32),
        jax.ShapeDtypeStruct((n_seq, win, kvw), F32),
        jax.ShapeDtypeStruct((n_seq, win, kvw), F32),
        jax.ShapeDtypeStruct((n_seq, gp2), F32),
    )
    scratch = [
        pltpu.VMEM((q_slabs + 2, rows, LANES), F32),
        pltpu.VMEM((q_slabs, rows, LANES), F32),
        pltpu.VMEM((SEQS_PER_GROUP, key_pad, kvw), F32),
        pltpu.VMEM((SEQS_PER_GROUP, key_pad, kvw), F32),
        pltpu.VMEM((SEQS_PER_GROUP, gp2), F32),
        pltpu.VMEM((rows, gp2), F32),
        pltpu.VMEM((rows, d_ssm), F32),
    ]
    kern = functools.partial(_mixer_kernel, t_len=t_len, alpha=alpha,
                             mask_missing_chunks=mask_missing_chunks, row_block=row_block,
                             seqs_per_iter=ATTN_SEQS_PER_ITER)
    args = [x_all, kinit, vinit, hinit, lw["wqkv"], lw["wu"], lw["wgl"], bias_s, lw["wap"], lw["a"],
            lw["wb"], lw["wc"], lw["dskip"], lw["wglu"], lw["wout"], lw["ln1g"], lw["ln1b"]]
    return pl.pallas_call(
        kern,
        grid=(n_groups, n_steps),
        in_specs=in_specs,
        out_specs=out_specs,
        out_shape=out_shape,
        scratch_shapes=scratch,
        input_output_aliases={0: 0},
        compiler_params=_cparams(("arbitrary", "arbitrary")),
        name="mixer_t%d" % t_len,
    )(*args)


def _router_kernel(x_ref, wrt_ref, brt_ref, upper_ref, idx_ref, wts_ref, rank_ref, cnt_ref, base_scr):
    i = pl.program_id(0)
    n_exp = wrt_ref.shape[0]
    tb = x_ref.shape[0]
    per_group = n_exp // N_ROUTE_GROUPS
    lane_rep = tb // LANES

    @pl.when(i == 0)
    def _():
        base_scr[...] = jnp.zeros_like(base_scr)

    xb = x_ref[...].astype(BF16)
    logits = lax.dot_general(wrt_ref[...], xb, (((1,), (1,)), ((), ())), preferred_element_type=F32)
    scores = jax.nn.sigmoid(logits)
    biased = scores + jnp.concatenate([brt_ref[...]] * lane_rep, axis=1)
    neg = jnp.float32(-jnp.inf)
    big = jnp.float32(2 ** 20)

    gs = []
    member = lax.broadcasted_iota(jnp.int32, (per_group, tb), 0).astype(F32)
    for g in range(N_ROUTE_GROUPS):
        bg = biased[g * per_group:(g + 1) * per_group, :]
        m1 = jnp.max(bg, axis=0, keepdims=True)
        first = jnp.min(jnp.where(bg == m1, member, big), axis=0, keepdims=True)
        m2 = jnp.max(jnp.where(member == first, neg, bg), axis=0, keepdims=True)
        gs.append(m1 + m2)

    keep = [jnp.zeros((1, tb), F32) for _ in range(N_ROUTE_GROUPS)]
    for _ in range(TOPK_GROUPS):
        best = gs[0]
        for g in range(1, N_ROUTE_GROUPS):
            best = jnp.maximum(best, gs[g])
        taken = jnp.zeros((1, tb), F32)
        for g in range(N_ROUTE_GROUPS):
            pick = jnp.where(gs[g] == best, 1.0 - taken, 0.0)
            taken = taken + pick
            keep[g] = keep[g] + pick
            gs[g] = jnp.where(pick > 0.5, neg, gs[g])

    vals = jnp.concatenate(
        [jnp.where(jnp.broadcast_to(keep[g], (per_group, tb)) > 0.5,
                   biased[g * per_group:(g + 1) * per_group, :], NEG_INF) for g in range(N_ROUTE_GROUPS)],
        axis=0)
    eid = lax.broadcasted_iota(jnp.int32, (n_exp, tb), 0).astype(F32)
    onehot = jnp.zeros((n_exp, tb), F32)
    sel_idx = []
    sel_score = []
    for _ in range(TOP_K):
        m = jnp.max(vals, axis=0, keepdims=True)
        first = jnp.min(jnp.where(vals == m, eid, big), axis=0, keepdims=True)
        sel = eid == first
        sel_idx.append(first)
        sel_score.append(jnp.sum(jnp.where(sel, scores, 0.0), axis=0, keepdims=True))
        vals = jnp.where(sel, neg, vals)
        onehot = jnp.where(sel, 1.0, onehot)

    total = sel_score[0]
    for k in range(1, TOP_K):
        total = total + sel_score[k]

    prefix = jnp.dot(onehot.astype(BF16), upper_ref[...], preferred_element_type=F32)
    pos = prefix + jnp.concatenate([base_scr[...]] * lane_rep, axis=1)
    for k in range(TOP_K):
        idx_ref[k:k + 1, :] = sel_idx[k].astype(jnp.int32)
        wts_ref[k:k + 1, :] = sel_score[k] / total * ROUTED_SCALE
        rk = jnp.sum(jnp.where(eid == sel_idx[k], pos, 0.0), axis=0, keepdims=True)
        rank_ref[k:k + 1, :] = rk.astype(jnp.int32)
    base_scr[...] = base_scr[...] + jnp.broadcast_to(jnp.sum(onehot, axis=1, keepdims=True), base_scr.shape)
    cnt_ref[...] = base_scr[...]


def _router(x1, wrt, brt, tb):
    n, d_model = x1.shape
    n_exp = wrt.shape[0]
    upper = jnp.triu(jnp.ones((tb, tb), BF16), k=1)
    tok = lambda i: (0, i)
    return pl.pallas_call(
        _router_kernel,
        grid=(n // tb,),
        in_specs=[pl.BlockSpec((tb, d_model), lambda i: (i, 0)), _const_spec(wrt.shape), _const_spec(brt.shape),
                  _const_spec(upper.shape)],
        out_specs=(pl.BlockSpec((TOP_K, tb), tok), pl.BlockSpec((TOP_K, tb), tok), pl.BlockSpec((TOP_K, tb), tok),
                   pl.BlockSpec((n_exp, LANES), lambda i: (0, 0))),
        out_shape=(jax.ShapeDtypeStruct((TOP_K, n), jnp.int32), jax.ShapeDtypeStruct((TOP_K, n), F32),
                   jax.ShapeDtypeStruct((TOP_K, n), jnp.int32), jax.ShapeDtypeStruct((n_exp, LANES), F32)),
        scratch_shapes=[pltpu.VMEM((n_exp, LANES), F32)],
        compiler_params=_cparams(("arbitrary",)),
        name="router",
    )(x1, wrt, brt, upper)


def _pack_bf16_pairs(x):
    h = x.shape[1] // 2
    xb = x.astype(BF16).astype(F32)
    lo = lax.bitcast_convert_type(xb[:, :h], jnp.uint32) >> 16
    return lo | (lax.bitcast_convert_type(xb[:, h:], jnp.uint32) & jnp.uint32(0xFFFF0000))


def _unpack_lo(w):
    return lax.bitcast_convert_type(w << 16, F32)


def _unpack_hi(w):
    return lax.bitcast_convert_type(w & jnp.uint32(0xFFFF0000), F32)


def _dest_kernel(cstart_ref, idx_ref, rank_ref, dest_ref):
    idx = idx_ref[...]
    rank = rank_ref[...]

    def body(e, acc):
        return jnp.where(idx == e, cstart_ref[e] + rank, acc)

    dest_ref[...] = lax.fori_loop(0, cstart_ref.shape[0], body, jnp.zeros_like(rank), unroll=8)


def _dest(cstart, idx_t, rank_t, tb):
    n = idx_t.shape[1]
    tok = lambda i: (0, i)
    return pl.pallas_call(
        _dest_kernel,
        grid=(n // tb,),
        in_specs=[pl.BlockSpec(memory_space=pltpu.SMEM), pl.BlockSpec((TOP_K, tb), tok),
                  pl.BlockSpec((TOP_K, tb), tok)],
        out_specs=pl.BlockSpec((TOP_K, tb), tok),
        out_shape=jax.ShapeDtypeStruct((TOP_K, n), jnp.int32),
        compiler_params=_cparams(("arbitrary",)),
        name="slot_index",
    )(cstart, idx_t, rank_t)


def _dispatch_kernel(dest_ref, x_ref, xs_ref, buf, sems, *, n_steps):
    i = pl.program_id(0)
    tb = x_ref.shape[0]
    slot = lax.rem(i, 2)

    def drain(s):
        for k in range(TOP_K):
            pltpu.make_async_copy(buf.at[pl.ds(s * tb, tb)], xs_ref.at[pl.ds(0, tb)], sems.at[s]).wait()

    @pl.when(i >= 2)
    def _():
        drain(slot)

    words = _pack_bf16_pairs(x_ref[...])
    for j in range(buf.shape[1]):
        buf[pl.ds(slot * tb, tb), j, :] = words[:, j * LANES:(j + 1) * LANES]

    def issue(t, carry):
        for k in range(TOP_K):
            pltpu.make_async_copy(buf.at[pl.ds(slot * tb + t, 1)], xs_ref.at[pl.ds(dest_ref[k, t], 1)],
                                  sems.at[slot]).start(priority=k % 2)
        return carry

    lax.fori_loop(0, tb, issue, 0, unroll=2)

    @pl.when(i == n_steps - 1)
    def _():
        drain(slot)
        if n_steps >= 2:
            drain(1 - slot)


def _dispatch(dest, x1, n_slots, tb):
    n, d_model = x1.shape
    row_tiles = d_model // 2 // LANES
    return pl.pallas_call(
        functools.partial(_dispatch_kernel, n_steps=n // tb),
        grid=(n // tb,),
        in_specs=[pl.BlockSpec((TOP_K, tb), lambda i: (0, i), memory_space=pltpu.SMEM),
                  pl.BlockSpec((tb, d_model), lambda i: (i, 0))],
        out_specs=pl.BlockSpec(memory_space=pl.ANY),
        out_shape=jax.ShapeDtypeStruct((n_slots, row_tiles, LANES), jnp.uint32),
        scratch_shapes=[pltpu.VMEM((2 * tb, row_tiles, LANES), jnp.uint32), pltpu.SemaphoreType.DMA((2,))],
        compiler_params=_cparams(("arbitrary",)),
        name="dispatch",
    )(dest, x1)


def _expert_kernel(wt_ref, we_ref, wn_ref, ws_ref, lo_ref, hi_ref, xs_ref, wg_hbm, wu_hbm, wd_hbm, os_ref,
                   wgu_scr, wd_scr, stage_g, stage_u, stage_d, wsem, *, bm, layer):
    w = pl.program_id(0)
    row_tiles = xs_ref.shape[0] // bm
    f = stage_g.shape[2]
    prev = jnp.maximum(w - 1, 0)
    lo = lo_ref[w]
    hi = hi_ref[w]
    live = hi > lo
    expert = we_ref[w]
    slot = ws_ref[w]
    new_expert = jnp.logical_or(w == 0, expert != we_ref[prev])
    new_tile = jnp.logical_or(w == 0, wt_ref[w] != wt_ref[prev])

    def weight_copies(e, s):
        return [pltpu.make_async_copy(wg_hbm.at[layer, e], stage_g.at[s], wsem.at[s]),
                pltpu.make_async_copy(wu_hbm.at[layer, e], stage_u.at[s], wsem.at[s]),
                pltpu.make_async_copy(wd_hbm.at[layer, e], stage_d.at[s], wsem.at[s])]

    @pl.when(jnp.logical_and(live, new_expert))
    def _():
        @pl.when(w == 0)
        def _():
            for cp in weight_copies(expert, slot):
                cp.start()

        for cp in weight_copies(expert, slot):
            cp.wait()
        wgu_scr[:, :f] = stage_g[slot].astype(BF16)
        wgu_scr[:, f:] = stage_u[slot].astype(BF16)
        wd_scr[...] = stage_d[slot].astype(BF16)
        nxt = wn_ref[w]

        @pl.when(nxt != expert)
        def _():
            for cp in weight_copies(nxt, 1 - slot):
                cp.start()

    @pl.when(live)
    def _():
        words = [xs_ref[pl.ds(j, bm, stride=row_tiles), :] for j in range(row_tiles)]
        x = jnp.concatenate([_unpack_lo(word).astype(BF16) for word in words]
                            + [_unpack_hi(word).astype(BF16) for word in words], axis=1)
        gu = jnp.dot(x, wgu_scr[...], preferred_element_type=F32)
        h = jax.nn.silu(gu[:, :f]) * gu[:, f:]
        out = _pack_bf16_pairs(jnp.dot(h.astype(BF16), wd_scr[...], preferred_element_type=F32))
        out_tiles = os_ref.shape[0] // bm

        @pl.when(new_tile)
        def _():
            for j in range(out_tiles):
                os_ref[pl.ds(j, bm, stride=out_tiles), :] = out[:, j * LANES:(j + 1) * LANES]

        @pl.when(jnp.logical_not(new_tile))
        def _():
            row = lax.broadcasted_iota(jnp.int32, (bm, LANES), 0)
            mine = jnp.logical_and(row >= lo, row < hi)
            for j in range(out_tiles):
                rows_j = pl.ds(j, bm, stride=out_tiles)
                os_ref[rows_j, :] = jnp.where(mine, out[:, j * LANES:(j + 1) * LANES], os_ref[rows_j, :])


def _experts(wt, we, wn, ws, lo, hi, xs, layer, w_gate, w_up, w_down, bm):
    n_slots, row_tiles, _ = xs.shape
    d_model, f = w_gate.shape[2], w_gate.shape[3]
    xs2d = xs.reshape(n_slots * row_tiles, LANES)
    row_map = lambda w, wt, *others: (wt[w], 0)
    hbm = pl.BlockSpec(memory_space=pl.ANY)
    return pl.pallas_call(
        functools.partial(_expert_kernel, bm=bm, layer=layer),
        grid_spec=pltpu.PrefetchScalarGridSpec(
            num_scalar_prefetch=6,
            grid=(wt.shape[0],),
            in_specs=[pl.BlockSpec((bm * row_tiles, LANES), row_map), hbm, hbm, hbm],
            out_specs=pl.BlockSpec((bm * row_tiles, LANES), row_map),
            scratch_shapes=[pltpu.VMEM((d_model, 2 * f), BF16), pltpu.VMEM((f, d_model), BF16),
                            pltpu.VMEM((2, d_model, f), F32), pltpu.VMEM((2, d_model, f), F32),
                            pltpu.VMEM((2, f, d_model), F32), pltpu.SemaphoreType.DMA((2,))],
        ),
        out_shape=jax.ShapeDtypeStruct((n_slots * row_tiles, LANES), jnp.uint32),
        compiler_params=_cparams(("arbitrary",)),
        name="experts",
    )(wt, we, wn, ws, lo, hi, xs2d, w_gate, w_up, w_down).reshape(n_slots, row_tiles, LANES)


def _work_list(counts, bm, n_tiles, n_work):
    n_exp = counts.shape[0]
    cend = jnp.cumsum(counts)
    cstart = cend - counts
    has = counts > 0
    first_tile = cstart // bm
    n_items = jnp.where(has, (cend - 1) // bm - first_tile + 1, 0)
    wend = jnp.cumsum(n_items)
    woff = wend - n_items
    total = wend[-1]
    w = jnp.arange(n_work, dtype=jnp.int32)
    valid = w < total
    w_eff = jnp.minimum(w, total - 1)
    we = jnp.minimum(jnp.sum((wend[None, :] <= w_eff[:, None]).astype(jnp.int32), axis=1), n_exp - 1)
    onehot = we[:, None] == jnp.arange(n_exp, dtype=jnp.int32)[None, :]
    pick = lambda v: jnp.sum(jnp.where(onehot, v[None, :], 0), axis=1)
    wt = pick(first_tile) + (w_eff - pick(woff))
    lo = jnp.where(valid, jnp.maximum(pick(cstart) - wt * bm, 0), 0)
    hi = jnp.where(valid, jnp.minimum(pick(cend) - wt * bm, bm), 0)
    eids = jnp.arange(n_exp, dtype=jnp.int32)
    later = jnp.logical_and(has[None, :], eids[None, :] > eids[:, None])
    next_present = jnp.min(jnp.where(later, eids[None, :], n_exp), axis=1)
    next_present = jnp.where(next_present == n_exp, eids, next_present)
    ws = (pick(jnp.cumsum(has.astype(jnp.int32))) - 1) % 2
    wn = pick(next_present)
    i32 = lambda v: v.astype(jnp.int32)
    return i32(cstart), i32(wt), i32(we), i32(wn), i32(ws), i32(lo), i32(hi)


def _rows_view(ref):
    rows, rt, lanes = ref.shape
    return ref.reshape(rows * rt, lanes)


def _combine_kernel(dest0_ref, destn_ref, x_ref, wts_ref, os_ref, wsgu_ref, wsd_ref, lng_ref, lnb_ref, *rest,
                    alpha, n_prompt_blocks):
    final = n_prompt_blocks is not None
    n_out = 2 if final else 1
    outs, gbufs, sems = rest[:n_out], rest[n_out:n_out + TOP_K], rest[n_out + TOP_K]
    tb, d_model = x_ref.shape
    f = wsd_ref.shape[0]
    rt = gbufs[0].shape[1]
    i = pl.program_id(0)
    slot = lax.rem(i, 2)
    nslot = 1 - slot

    def row_copy(dref, k, t, s):
        return pltpu.make_async_copy(os_ref.at[pl.ds(dref[k, t], 1)], gbufs[k].at[pl.ds(s * tb + t, 1)], sems.at[s])

    def slot_copy(k, s):
        return pltpu.make_async_copy(os_ref.at[pl.ds(0, tb)], gbufs[k].at[pl.ds(s * tb, tb)], sems.at[s])

    @pl.when(i == 0)
    def _():
        def issue(t, carry):
            for k in range(TOP_K):
                row_copy(dest0_ref, k, t, 0).start(priority=k % 2)
            return carry

        lax.fori_loop(0, tb, issue, 0, unroll=2)

    for k in range(TOP_K):
        slot_copy(k, slot).wait()

    views = [_rows_view(g) for g in gbufs]
    w = wts_ref[...]
    rb_rows = 4 * SUBLANES
    units = [(k, j) for k in range(TOP_K) for j in range(rt)]
    view_base = slot * (tb * rt)
    blocks = []
    for rb in range(tb // rb_rows):
        acc_lo = [None] * rt
        acc_hi = [None] * rt
        wk = None
        for tt in range(rb_rows):
            t = rb * rb_rows + tt
            for k in range(TOP_K):
                row_copy(destn_ref, k, t, nslot).start(priority=k % 2)
            for k, j in units[tt * len(units) // rb_rows:(tt + 1) * len(units) // rb_rows]:
                if j == 0:
                    wk = jnp.broadcast_to(w[rb * rb_rows:(rb + 1) * rb_rows, k:k + 1], (rb_rows, LANES))
                word = views[k][pl.ds(view_base + rb * rb_rows * rt + j, rb_rows, stride=rt), :]
                lo = _unpack_lo(word) * wk
                hi = _unpack_hi(word) * wk
                acc_lo[j] = lo if k == 0 else acc_lo[j] + lo
                acc_hi[j] = hi if k == 0 else acc_hi[j] + hi
        blocks.append(jnp.concatenate(acc_lo + acc_hi, axis=1))
    routed = jnp.concatenate(blocks, axis=0)

    @pl.when(i == pl.num_programs(0) - 1)
    def _():
        for k in range(TOP_K):
            slot_copy(k, nslot).wait()

    x = x_ref[...]
    gu = jnp.dot(x.astype(BF16), wsgu_ref[...], preferred_element_type=F32)
    h = jax.nn.silu(gu[:, :f]) * gu[:, f:]
    shared = jnp.dot(h.astype(BF16), wsd_ref[...], preferred_element_type=F32)
    res = alpha * x + (routed + shared)
    mu = jnp.mean(res, axis=-1, keepdims=True)
    xc = res - mu
    var = jnp.mean(xc * xc, axis=-1, keepdims=True)
    y = xc * lax.rsqrt(var + LN_EPS) * lng_ref[...] + lnb_ref[...]
    if not final:
        outs[0][...] = y
        return

    slab = rest[n_out + TOP_K + 1]
    tq = tb // SEQS_PER_GROUP
    for j in range(d_model // LANES):
        slab[j] = y[:, j * LANES:(j + 1) * LANES]

    def write(o_ref):
        for b in range(SEQS_PER_GROUP):
            for j in range(d_model // LANES):
                o_ref[b, :, j * LANES:(j + 1) * LANES] = slab[j, pl.ds(b, tq, stride=SEQS_PER_GROUP), :]

    i = pl.program_id(0)

    @pl.when(i < n_prompt_blocks)
    def _():
        write(outs[0])

    @pl.when(i >= n_prompt_blocks)
    def _():
        write(outs[1])


def _combine(dest, x1, wts_nk, os_, wsgu, wsd, lng, lnb, alpha, tb, final_shapes=None):
    n, d_model = x1.shape
    rt = os_.shape[1]
    scratch = [pltpu.VMEM((2 * tb, rt, LANES), jnp.uint32) for _ in range(TOP_K)] + [pltpu.SemaphoreType.DMA((2,))]
    last_block = n // tb - 1
    if final_shapes is None:
        n_prompt_blocks = None
        out_specs = pl.BlockSpec((tb, d_model), lambda i: (i, 0))
        out_shape = jax.ShapeDtypeStruct((n, d_model), F32)
    else:
        batch, seq, dec_batch, dec_seq = final_shapes
        tq = tb // SEQS_PER_GROUP
        assert dec_seq == tq and seq % tq == 0
        wpg = seq // tq
        n_prompt_blocks = batch * seq // tb
        last = n_prompt_blocks - 1
        out_specs = (
            pl.BlockSpec((None, SEQS_PER_GROUP, tq, d_model),
                         lambda i: (jnp.minimum(i, last) // wpg, 0, jnp.minimum(i, last) % wpg, 0)),
            pl.BlockSpec((None, SEQS_PER_GROUP, tq, d_model), lambda i: (jnp.maximum(i - n_prompt_blocks, 0), 0, 0, 0)),
        )
        out_shape = (jax.ShapeDtypeStruct((batch // SEQS_PER_GROUP, SEQS_PER_GROUP, seq, d_model), F32),
                     jax.ShapeDtypeStruct((dec_batch // SEQS_PER_GROUP, SEQS_PER_GROUP, dec_seq, d_model), F32))
        scratch.append(pltpu.VMEM((d_model // LANES, tb, LANES), F32))
    return pl.pallas_call(
        functools.partial(_combine_kernel, alpha=alpha, n_prompt_blocks=n_prompt_blocks),
        grid=(n // tb,),
        in_specs=[pl.BlockSpec((TOP_K, tb), lambda i: (0, 0), memory_space=pltpu.SMEM),
                  pl.BlockSpec((TOP_K, tb), lambda i: (0, jnp.minimum(i + 1, last_block)), memory_space=pltpu.SMEM),
                  pl.BlockSpec((tb, d_model), lambda i: (i, 0)),
                  pl.BlockSpec((tb, TOP_K), lambda i: (i, 0)),
                  pl.BlockSpec(memory_space=pl.ANY),
                  _const_spec(wsgu.shape), _const_spec(wsd.shape), _const_spec(lng.shape), _const_spec(lnb.shape)],
        out_specs=out_specs,
        out_shape=out_shape,
        scratch_shapes=scratch,
        compiler_params=_cparams(("arbitrary",)),
        name="combine" if final_shapes is None else "combine_out",
    )(dest, dest, x1, wts_nk, os_, wsgu, wsd, lng, lnb)


def _to_time_major(x):
    b, t, d = x.shape
    return x.reshape(b // SEQS_PER_GROUP, SEQS_PER_GROUP, t, d).transpose(0, 2, 1, 3).reshape(b * t, d)


def _from_time_major(x, b, t):
    d = x.shape[-1]
    return x.reshape(b // SEQS_PER_GROUP, t, SEQS_PER_GROUP, d).transpose(0, 2, 1, 3).reshape(b, t, d)


def _block_diag_in(bbt, n_gb):
    g, c, p = bbt.shape
    gl = g // n_gb
    eye = jnp.eye(gl, dtype=bbt.dtype)
    return jnp.einsum("jgcp,gh->jgchp", bbt.reshape(n_gb, gl, c, p), eye).reshape(n_gb, gl * c, gl * p)


def _block_diag_out(cm, n_gb):
    g, c, p = cm.shape
    gl = g // n_gb
    eye = jnp.eye(gl, dtype=cm.dtype)
    return jnp.einsum("jgcp,gh->jhpgc", cm.reshape(n_gb, gl, c, p), eye).reshape(n_gb, gl * p, gl * c)


def kernel(x_prompt, x_sample, cache_k, cache_v, state_ssm_re, state_ssm_im, rel_bias, w_in, attn_sinks, w_attn_proj, lam_re, lam_im, log_dt, b_re, b_im, c_re, c_im, d_skip, w_glu, w_out, ln1_g, ln1_b, w_router, b_router, w_gate, w_up, w_down, ws_gate, ws_up, ws_down, ln2_g, ln2_b):
    batch, seq, d_model = x_prompt.shape
    dec_batch, dec_seq, _ = x_sample.shape
    depth = w_in.shape[0]
    win = cache_k.shape[2]
    n_q_heads = attn_sinks.shape[1]
    q_w = n_q_heads * HEAD_DIM
    kv_w = N_KV_HEADS * HEAD_DIM
    d_ssm = d_skip.shape[1]
    n_groups_ssm, state_dim = lam_re.shape[1], lam_re.shape[2]
    gp = n_groups_ssm * state_dim
    n_exp = w_router.shape[2]
    alpha = (2 * depth) ** 0.25
    assert win == WINDOW and seq % CHUNK == 0 and dec_seq <= CHUNK
    assert batch % SEQS_PER_GROUP == 0 and dec_batch % SEQS_PER_GROUP == 0
    qch = (PAST_LEN + np.arange(dec_seq)) // CHUNK
    kch = np.concatenate([PAST_LEN - win + np.arange(win), PAST_LEN + np.arange(dec_seq)]) // CHUNK
    assert np.all((kch[None, :] >= qch[:, None] - WINDOW // CHUNK) & (kch[None, :] <= qch[:, None]))

    n_p = batch * seq
    n_s = dec_batch * dec_seq
    n_tok = n_p + n_s
    n_gb = n_groups_ssm // SSM_GROUPS_PER_BATCH

    bias = _rel_bias_table(rel_bias)
    a_re, a_im, bbt_re, bbt_im = _discretize(lam_re, lam_im, log_dt, b_re, b_im)

    x_all = jnp.concatenate([_to_time_major(x_prompt.astype(F32)), _to_time_major(x_sample.astype(F32))], axis=0)
    zeros_kv = jnp.zeros((batch, win, kv_w), F32)
    zeros_h = jnp.zeros((batch, 2 * gp), F32)

    tb_route = 512 if n_tok % 512 == 0 else 256
    tb_move = 256
    bm = 512
    n_slots = n_tok * TOP_K
    assert n_slots % bm == 0
    n_tiles = n_slots // bm
    n_work = n_tiles + n_exp - 1

    outs = {k: [] for k in ("kp", "vp", "hp", "ks", "vs", "hs")}
    for l in range(depth):
        wl = w_in[l].astype(BF16)
        lw = dict(
            wqkv=wl[:, :q_w + 2 * kv_w], wu=wl[:, q_w + 2 * kv_w:q_w + 2 * kv_w + d_ssm],
            wgl=wl[:, q_w + 2 * kv_w + d_ssm:], wap=w_attn_proj[l].astype(BF16),
            a=jnp.stack([a_re[l].reshape(gp), a_im[l].reshape(gp)]),
            wb=jnp.concatenate([_block_diag_in(bbt_re[l], n_gb), _block_diag_in(bbt_im[l], n_gb)], axis=2).astype(BF16),
            wc=jnp.concatenate([_block_diag_out(c_re[l].astype(F32), n_gb),
                                -_block_diag_out(c_im[l].astype(F32), n_gb)], axis=1).astype(BF16),
            dskip=d_skip[l].astype(F32).reshape(1, d_ssm), wglu=w_glu[l].astype(BF16), wout=w_out[l].astype(BF16),
            ln1g=ln1_g[l].astype(F32).reshape(1, d_model), ln1b=ln1_b[l].astype(F32).reshape(1, d_model),
        )
        x1, kp, vp, hp = _mixer(x_all, 0, batch // SEQS_PER_GROUP, seq // CHUNK, CHUNK, zeros_kv, zeros_kv, zeros_h,
                                lw, bias, attn_sinks[l], alpha, True)
        hinit = jnp.concatenate([state_ssm_re[l].astype(F32).reshape(dec_batch, gp),
                                 state_ssm_im[l].astype(F32).reshape(dec_batch, gp)], axis=1)
        x1, ks, vs, hs = _mixer(x1, n_p // (SEQS_PER_GROUP * dec_seq), dec_batch // SEQS_PER_GROUP, 1, dec_seq,
                                cache_k[l].astype(F32).reshape(dec_batch, win, kv_w),
                                cache_v[l].astype(F32).reshape(dec_batch, win, kv_w), hinit,
                                lw, bias, attn_sinks[l], alpha, False)
        for name, val in zip(("kp", "vp", "hp", "ks", "vs", "hs"), (kp, vp, hp, ks, vs, hs)):
            outs[name].append(val)

        wrt = w_router[l].astype(BF16).T
        brt = jnp.broadcast_to(b_router[l].astype(F32)[:, None], (n_exp, LANES))
        idx_t, wts_t, rank_t, cnt = _router(x1, wrt, brt, tb_route)
        cstart, wt, we, wn, wslot, lo, hi = _work_list(cnt[:, 0].astype(jnp.int32), bm, n_tiles, n_work)
        dest = _dest(cstart, idx_t, rank_t, tb_route)
        xs = _dispatch(dest, x1, n_slots, tb_route)
        os_ = _experts(wt, we, wn, wslot, lo, hi, xs, l, w_gate.astype(F32), w_up.astype(F32), w_down.astype(F32), bm)
        wsgu = jnp.concatenate([ws_gate[l], ws_up[l]], axis=1).astype(BF16)
        x_all = _combine(dest, x1, wts_t.T, os_, wsgu, ws_down[l].astype(BF16),
                         ln2_g[l].astype(F32).reshape(1, d_model), ln2_b[l].astype(F32).reshape(1, d_model),
                         alpha, tb_move, final_shapes=(batch, seq, dec_batch, dec_seq) if l == depth - 1 else None)

    y_prompt = x_all[0].reshape(batch, seq, d_model)
    y_sample = x_all[1].reshape(dec_batch, dec_seq, d_model)

    def kv(vals, b):
        return jnp.stack(vals).reshape(depth, b, win, N_KV_HEADS, HEAD_DIM)

    def st(vals, b, part):
        return jnp.stack(vals)[:, :, part * gp:(part + 1) * gp].reshape(depth, b, n_groups_ssm, state_dim)

    return (y_prompt, y_sample, kv(outs["kp"], batch), kv(outs["vp"], batch), st(outs["hp"], batch, 0),
            st(outs["hp"], batch, 1), kv(outs["ks"], dec_batch), kv(outs["vs"], dec_batch),
            st(outs["hs"], dec_batch, 0), st(outs["hs"], dec_batch, 1))
```

```python
import functools
import math

import jax
import jax.numpy as jnp
import numpy as np
from jax import lax
from jax.experimental import pallas as pl
from jax.experimental.pallas import tpu as pltpu

CHUNK = 64
WINDOW = 128
HEAD_DIM = 64
N_KV_HEADS = 2
MAX_DISTANCE = 128
CH_PER_GROUP = 16
STATE_DIM = 64
TOP_K = 8
N_ROUTE_GROUPS = 8
TOPK_GROUPS = 4
ROUTED_SCALE = 2.5
LN_EPS = 1e-5
NEG_INF = -1e30
PAST_LEN = 1024

SUBLANES = 8
LANES = 128
SEQS_PER_GROUP = SUBLANES
SSM_GROUPS_PER_BATCH = LANES // CH_PER_GROUP
VMEM_LIMIT_BYTES = 60 * 1024 * 1024
ATTN_SEQS_PER_ITER = 8
MIXER_ROW_BLOCK = 256

BF16 = jnp.bfloat16
F32 = jnp.float32


def _cparams(sem):
    return pltpu.CompilerParams(dimension_semantics=sem, vmem_limit_bytes=VMEM_LIMIT_BYTES)


def _const_spec(shape):
    nd = len(shape)
    return pl.BlockSpec(shape, lambda *_: (0,) * nd, pipeline_mode=pl.Buffered(1))


def _bias_kernel(bucket_ref, tbl_ref, out_ref):
    n_buckets, n_heads = tbl_ref.shape
    bk = bucket_ref[...]
    for h in range(n_heads):
        acc = jnp.zeros(bk.shape, F32)
        for b in range(n_buckets):
            acc = jnp.where(bk == b, tbl_ref[b, h], acc)
        out_ref[h] = acc


def _rel_bias_table(rel_bias):
    n_buckets, n_heads = rel_bias.shape
    band = WINDOW + CHUNK
    rel = (jnp.arange(band) - WINDOW)[None, :] - jnp.arange(CHUNK)[:, None]
    nb = n_buckets // 2
    max_exact = nb // 2
    n = jnp.abs(rel)
    nf = jnp.maximum(n, 1).astype(F32)
    large = max_exact + (jnp.log(nf / max_exact) / math.log(MAX_DISTANCE / max_exact) * (nb - max_exact)).astype(jnp.int32)
    large = jnp.minimum(large, nb - 1)
    bucket = (jnp.where(rel > 0, nb, 0) + jnp.where(n < max_exact, n, large)).astype(jnp.int32)
    return pl.pallas_call(
        _bias_kernel,
        out_shape=jax.ShapeDtypeStruct((n_heads, CHUNK, band), F32),
        in_specs=[pl.BlockSpec(memory_space=pltpu.VMEM), pl.BlockSpec(memory_space=pltpu.SMEM)],
        out_specs=pl.BlockSpec(memory_space=pltpu.VMEM),
        name="rel_bias_table",
    )(bucket, rel_bias.astype(F32))


def _disc_kernel(lr_ref, li_ref, ldt_ref, br_ref, bi_ref, are_ref, aim_ref, bbr_ref, bbi_ref):
    lr = lr_ref[0]
    li = li_ref[0]
    dt = jnp.exp(ldt_ref[0])
    mag = jnp.exp(lr * dt)
    a_re = mag * jnp.cos(li * dt)
    a_im = mag * jnp.sin(li * dt)
    den = lr * lr + li * li
    f_re = ((a_re - 1.0) * lr + a_im * li) / den
    f_im = (a_im * lr - (a_re - 1.0) * li) / den
    br = br_ref[0]
    bi = bi_ref[0]
    are_ref[0] = a_re
    aim_ref[0] = a_im
    bbr_ref[0] = f_re * br - f_im * bi
    bbi_ref[0] = f_re * bi + f_im * br


def _discretize(lam_re, lam_im, log_dt, b_re, b_im):
    depth, g, p = lam_re.shape
    c = b_re.shape[-1]
    spec_gp = pl.BlockSpec((1, g, 1, p), lambda l: (l, 0, 0, 0))
    spec_g1 = pl.BlockSpec((1, g, 1, 1), lambda l: (l, 0, 0, 0))
    spec_gcp = pl.BlockSpec((1, g, c, p), lambda l: (l, 0, 0, 0))
    return pl.pallas_call(
        _disc_kernel,
        grid=(depth,),
        out_shape=(jax.ShapeDtypeStruct((depth, g, 1, p), F32), jax.ShapeDtypeStruct((depth, g, 1, p), F32),
                   jax.ShapeDtypeStruct((depth, g, c, p), F32), jax.ShapeDtypeStruct((depth, g, c, p), F32)),
        in_specs=[spec_gp, spec_gp, spec_g1, spec_gcp, spec_gcp],
        out_specs=(spec_gp, spec_gp, spec_gcp, spec_gcp),
        name="ssm_discretize",
    )(lam_re.astype(F32).reshape(depth, g, 1, p), lam_im.astype(F32).reshape(depth, g, 1, p),
      log_dt.astype(F32).reshape(depth, g, 1, 1),
      jnp.swapaxes(b_re.astype(F32), -1, -2), jnp.swapaxes(b_im.astype(F32), -1, -2))


def _mixer_kernel(x_ref, kinit_ref, vinit_ref, hinit_ref, wqkv_ref, wu_ref, wgl_ref, bias_ref,
                  wap_ref, a_ref, wb_ref, wc_ref, dskip_ref, wglu_ref, wout_ref, lng_ref, lnb_ref,
                  x1_ref, kout_ref, vout_ref, hout_ref,
                  qkv_scr, attn_scr, kwin, vwin, h_scr, bu_scr, u_scr,
                  *, t_len, alpha, mask_missing_chunks, row_block, seqs_per_iter):
    c = pl.program_id(1)
    n_c = pl.num_programs(1)
    rows, d_model = x_ref.shape
    win = kinit_ref.shape[1]
    key_pad = kwin.shape[1]
    q_slabs = attn_scr.shape[0]
    gq2 = q_slabs // N_KV_HEADS
    gp = a_ref.shape[1]
    d_ssm = u_scr.shape[1]
    n_gb = wb_ref.shape[0]
    gb_state = gp // n_gb
    scale = HEAD_DIM ** -0.5

    @pl.when(c == 0)
    def _():
        kwin[:, 0:win, :] = kinit_ref[...]
        vwin[:, 0:win, :] = vinit_ref[...]
        kwin[:, win + t_len:, :] = jnp.zeros((SEQS_PER_GROUP, key_pad - win - t_len, kwin.shape[2]), F32)
        vwin[:, win + t_len:, :] = jnp.zeros((SEQS_PER_GROUP, key_pad - win - t_len, vwin.shape[2]), F32)
        h_scr[...] = hinit_ref[...]

    xb = x_ref[...].astype(BF16)

    qkv = jnp.dot(xb, wqkv_ref[...], preferred_element_type=F32)
    for j in range(q_slabs + 2):
        qkv_scr[j] = qkv[:, j * LANES:(j + 1) * LANES]

    def append_kv(b, carry):
        seq_rows = pl.ds(b, t_len, stride=SEQS_PER_GROUP)
        kwin[b, win:win + t_len, :] = qkv_scr[q_slabs, seq_rows, :]
        vwin[b, win:win + t_len, :] = qkv_scr[q_slabs + 1, seq_rows, :]
        return carry

    lax.fori_loop(0, SEQS_PER_GROUP, append_kv, 0)

    def attend_group(i, carry):
        chains = [(i * seqs_per_iter + u, hk) for u in range(seqs_per_iter) for hk in range(N_KV_HEADS)]
        scores, values = [], []
        for b, hk in chains:
            seq_rows = pl.ds(b, t_len, stride=SEQS_PER_GROUP)
            parts = []
            for s in range(gq2):
                qs = qkv_scr[hk * gq2 + s, seq_rows, :]
                parts += [qs[:, :HEAD_DIM], qs[:, HEAD_DIM:]]
            q4 = jnp.concatenate(parts, axis=0).astype(BF16)
            kh = kwin[b, :, hk * HEAD_DIM:(hk + 1) * HEAD_DIM].astype(BF16)
            values.append(vwin[b, :, hk * HEAD_DIM:(hk + 1) * HEAD_DIM].astype(BF16))
            scores.append(lax.dot_general(q4, kh, (((1,), (1,)), ((), ())), preferred_element_type=F32))
        probs = []
        for (b, hk), s_ in zip(chains, scores):
            s_ = s_ * scale + bias_ref[hk]
            if mask_missing_chunks:
                key = lax.broadcasted_iota(jnp.int32, s_.shape, 1)
                first_valid = (WINDOW // CHUNK - jnp.minimum(c, WINDOW // CHUNK)) * CHUNK
                s_ = jnp.where(key >= first_valid, s_, NEG_INF)
            m = jnp.max(s_, axis=-1, keepdims=True)
            probs.append(jnp.exp(s_ - m).astype(BF16))
        ones = jnp.ones((key_pad, HEAD_DIM), BF16)
        outs = [jnp.dot(p, vh, preferred_element_type=F32) / jnp.dot(p, ones, preferred_element_type=F32)
                for p, vh in zip(probs, values)]
        for (b, hk), o in zip(chains, outs):
            for s in range(gq2):
                pair = jnp.concatenate([o[(2 * s) * t_len:(2 * s + 1) * t_len],
                                        o[(2 * s + 1) * t_len:(2 * s + 2) * t_len]], axis=1)
                attn_scr[hk * gq2 + s, pl.ds(b, t_len, stride=SEQS_PER_GROUP), :] = pair
        return carry

    lax.fori_loop(0, SEQS_PER_GROUP // seqs_per_iter, attend_group, 0)

    def slide_window(b, carry):
        knext = kwin[b, t_len:t_len + win, :]
        vnext = vwin[b, t_len:t_len + win, :]
        kwin[b, 0:win, :] = knext
        vwin[b, 0:win, :] = vnext
        return carry

    lax.fori_loop(0, SEQS_PER_GROUP, slide_window, 0)

    @pl.when(c == n_c - 1)
    def _():
        kout_ref[...] = kwin[:, 0:win, :]
        vout_ref[...] = vwin[:, 0:win, :]

    u = jnp.dot(xb, wu_ref[...], preferred_element_type=F32)
    u_scr[...] = u
    ch_b = d_ssm // n_gb
    for j in range(n_gb):
        bu = jnp.dot(u[:, j * ch_b:(j + 1) * ch_b].astype(BF16), wb_ref[j], preferred_element_type=F32)
        bu_scr[:, j * gb_state:(j + 1) * gb_state] = bu[:, :gb_state]
        bu_scr[:, gp + j * gb_state:gp + (j + 1) * gb_state] = bu[:, gb_state:]

    scan_lanes = 4 * LANES
    for lc in range(gp // scan_lanes):
        re_l = slice(lc * scan_lanes, (lc + 1) * scan_lanes)
        im_l = slice(gp + lc * scan_lanes, gp + (lc + 1) * scan_lanes)
        a_re = jnp.broadcast_to(a_ref[0:1, re_l], (SEQS_PER_GROUP, scan_lanes))
        a_im = jnp.broadcast_to(a_ref[1:2, re_l], (SEQS_PER_GROUP, scan_lanes))

        def step(t, h, re_l=re_l, im_l=im_l, a_re=a_re, a_im=a_im):
            h_re, h_im = h
            r0 = pl.multiple_of(t * SEQS_PER_GROUP, SEQS_PER_GROUP)
            n_re = a_re * h_re - a_im * h_im + bu_scr[pl.ds(r0, SEQS_PER_GROUP), re_l]
            n_im = a_re * h_im + a_im * h_re + bu_scr[pl.ds(r0, SEQS_PER_GROUP), im_l]
            bu_scr[pl.ds(r0, SEQS_PER_GROUP), re_l] = n_re
            bu_scr[pl.ds(r0, SEQS_PER_GROUP), im_l] = n_im
            return n_re, n_im

        h_re, h_im = lax.fori_loop(0, t_len, step, (h_scr[:, re_l], h_scr[:, im_l]), unroll=8)
        h_scr[:, re_l] = h_re
        h_scr[:, im_l] = h_im

    @pl.when(c == n_c - 1)
    def _():
        hout_ref[...] = h_scr[...]

    for rb in range(rows // row_block):
        r = slice(rb * row_block, (rb + 1) * row_block)
        ys = []
        for j in range(n_gb):
            hcat = jnp.concatenate([bu_scr[r, j * gb_state:(j + 1) * gb_state],
                                    bu_scr[r, gp + j * gb_state:gp + (j + 1) * gb_state]], axis=1)
            ys.append(jnp.dot(hcat.astype(BF16), wc_ref[j], preferred_element_type=F32))
        y = jnp.concatenate(ys, axis=1)
        z = jax.nn.gelu(y + dskip_ref[...] * u_scr[r, :])
        glu = jnp.dot(z.astype(BF16), wglu_ref[...], preferred_element_type=F32)
        br_b = glu[:, :d_model] * jax.nn.sigmoid(glu[:, d_model:])
        attn = jnp.concatenate([attn_scr[s, r, :] for s in range(q_slabs)], axis=1)
        br_a = jnp.dot(attn.astype(BF16), wap_ref[...], preferred_element_type=F32)
        xr = x_ref[r, :]
        gl = jnp.dot(xr.astype(BF16), wgl_ref[...], preferred_element_type=F32)
        mix = jax.nn.sigmoid(gl[:, :d_model]) * br_a + jax.nn.sigmoid(gl[:, d_model:]) * br_b
        res = alpha * xr + jnp.dot(mix.astype(BF16), wout_ref[...], preferred_element_type=F32)
        mu = jnp.mean(res, axis=-1, keepdims=True)
        xc = res - mu
        var = jnp.mean(xc * xc, axis=-1, keepdims=True)
        x1_ref[r, :] = xc * lax.rsqrt(var + LN_EPS) * lng_ref[...] + lnb_ref[...]


def _mixer(x_all, row_offset_blocks, n_groups, n_steps, t_len, kinit, vinit, hinit, lw, bias, sinks, alpha,
           mask_missing_chunks):
    n_total, d_model = x_all.shape
    rows = SEQS_PER_GROUP * t_len
    win = kinit.shape[1]
    kvw = kinit.shape[2]
    gp2 = hinit.shape[1]
    gp = gp2 // 2
    q_slabs = lw["wap"].shape[0] // LANES
    d_ssm = lw["wu"].shape[1]
    band = win + t_len
    gq = bias.shape[0] // N_KV_HEADS
    key_pad = -(-(band + 1) // LANES) * LANES
    sink_col = jnp.broadcast_to(sinks.astype(F32).reshape(N_KV_HEADS * gq, 1, 1), (N_KV_HEADS * gq, t_len, 1))
    dead_cols = jnp.full((N_KV_HEADS * gq, t_len, key_pad - band - 1), NEG_INF, F32)
    bias_s = jnp.concatenate([bias[:, :t_len, :band], sink_col, dead_cols], axis=2)
    bias_s = bias_s.reshape(N_KV_HEADS, gq * t_len, key_pad)
    row_block = min(rows, MIXER_ROW_BLOCK)

    def xmap(g, c):
        return (row_offset_blocks + g * n_steps + c, 0)

    grp3 = lambda g, c: (g, 0, 0)
    grp2 = lambda g, c: (g, 0)
    in_specs = [
        pl.BlockSpec((rows, d_model), xmap),
        pl.BlockSpec((SEQS_PER_GROUP, win, kvw), grp3),
        pl.BlockSpec((SEQS_PER_GROUP, win, kvw), grp3),
        pl.BlockSpec((SEQS_PER_GROUP, gp2), grp2),
        _const_spec(lw["wqkv"].shape), _const_spec(lw["wu"].shape), _const_spec(lw["wgl"].shape),
        _const_spec(bias_s.shape), _const_spec(lw["wap"].shape),
        _const_spec(lw["a"].shape), _const_spec(lw["wb"].shape), _const_spec(lw["wc"].shape),
        _const_spec(lw["dskip"].shape), _const_spec(lw["wglu"].shape), _const_spec(lw["wout"].shape),
        _const_spec(lw["ln1g"].shape), _const_spec(lw["ln1b"].shape),
    ]
    out_specs = (
        pl.BlockSpec((rows, d_model), xmap),
        pl.BlockSpec((SEQS_PER_GROUP, win, kvw), grp3),
        pl.BlockSpec((SEQS_PER_GROUP, win, kvw), grp3),
        pl.BlockSpec((SEQS_PER_GROUP, gp2), grp2),
    )
    n_seq = n_groups * SEQS_PER_GROUP
    out_shape = (
        jax.ShapeDtypeStruct((n_total, d_model), F32),
        jax.ShapeDtypeStruct((n_seq, win, kvw), F32),
        jax.ShapeDtypeStruct((n_seq, win, kvw), F32),
        jax.ShapeDtypeStruct((n_seq, gp2), F32),
    )
    scratch = [
        pltpu.VMEM((q_slabs + 2, rows, LANES), F32),
        pltpu.VMEM((q_slabs, rows, LANES), F32),
        pltpu.VMEM((SEQS_PER_GROUP, key_pad, kvw), F32),
        pltpu.VMEM((SEQS_PER_GROUP, key_pad, kvw), F32),
        pltpu.VMEM((SEQS_PER_GROUP, gp2), F32),
        pltpu.VMEM((rows, gp2), F32),
        pltpu.VMEM((rows, d_ssm), F32),
    ]
    kern = functools.partial(_mixer_kernel, t_len=t_len, alpha=alpha,
                             mask_missing_chunks=mask_missing_chunks, row_block=row_block,
                             seqs_per_iter=ATTN_SEQS_PER_ITER)
    args = [x_all, kinit, vinit, hinit, lw["wqkv"], lw["wu"], lw["wgl"], bias_s, lw["wap"], lw["a"],
            lw["wb"], lw["wc"], lw["dskip"], lw["wglu"], lw["wout"], lw["ln1g"], lw["ln1b"]]
    return pl.pallas_call(
        kern,
        grid=(n_groups, n_steps),
        in_specs=in_specs,
        out_specs=out_specs,
        out_shape=out_shape,
        scratch_shapes=scratch,
        input_output_aliases={0: 0},
        compiler_params=_cparams(("arbitrary", "arbitrary")),
        name="mixer_t%d" % t_len,
    )(*args)


def _router_kernel(x_ref, wrt_ref, brt_ref, upper_ref, idx_ref, wts_ref, rank_ref, cnt_ref, base_scr):
    i = pl.program_id(0)
    n_exp = wrt_ref.shape[0]
    tb = x_ref.shape[0]
    per_group = n_exp // N_ROUTE_GROUPS
    lane_rep = tb // LANES

    @pl.when(i == 0)
    def _():
        base_scr[...] = jnp.zeros_like(base_scr)

    xb = x_ref[...].astype(BF16)
    logits = lax.dot_general(wrt_ref[...], xb, (((1,), (1,)), ((), ())), preferred_element_type=F32)
    neg = jnp.float32(-jnp.inf)
    big = jnp.float32(2 ** 20)
    tc = LANES
    member = lax.broadcasted_iota(jnp.int32, (per_group, tc), 0).astype(F32)
    eid = lax.broadcasted_iota(jnp.int32, (n_exp, tc), 0).astype(F32)

    def select(scores, biased):
        gs = []
        for g in range(N_ROUTE_GROUPS):
            bg = biased[g * per_group:(g + 1) * per_group, :]
            m1 = jnp.max(bg, axis=0, keepdims=True)
            first = jnp.min(jnp.where(bg == m1, member, big), axis=0, keepdims=True)
            m2 = jnp.max(jnp.where(member == first, neg, bg), axis=0, keepdims=True)
            gs.append(m1 + m2)
        keep = [jnp.zeros((1, tc), F32) for _ in range(N_ROUTE_GROUPS)]
        for _ in range(TOPK_GROUPS):
            best = gs[0]
            for g in range(1, N_ROUTE_GROUPS):
                best = jnp.maximum(best, gs[g])
            taken = jnp.zeros((1, tc), F32)
            for g in range(N_ROUTE_GROUPS):
                pick = jnp.where(gs[g] == best, 1.0 - taken, 0.0)
                taken = taken + pick
                keep[g] = keep[g] + pick
                gs[g] = jnp.where(pick > 0.5, neg, gs[g])
        vals = jnp.concatenate(
            [jnp.where(jnp.broadcast_to(keep[g], (per_group, tc)) > 0.5,
                       biased[g * per_group:(g + 1) * per_group, :], NEG_INF) for g in range(N_ROUTE_GROUPS)],
            axis=0)
        onehot = jnp.zeros((n_exp, tc), F32)
        sel_idx, sel_score = [], []
        for _ in range(TOP_K):
            m = jnp.max(vals, axis=0, keepdims=True)
            first = jnp.min(jnp.where(vals == m, eid, big), axis=0, keepdims=True)
            sel = eid == first
            sel_idx.append(first)
            sel_score.append(jnp.sum(jnp.where(sel, scores, 0.0), axis=0, keepdims=True))
            vals = jnp.where(sel, neg, vals)
            onehot = jnp.where(sel, 1.0, onehot)
        return sel_idx, sel_score, onehot

    picked = []
    for c in range(lane_rep):
        lanes = slice(c * tc, (c + 1) * tc)
        scores = jax.nn.sigmoid(logits[:, lanes])
        sel_idx, sel_score, onehot = select(scores, scores + brt_ref[...])
        total = sel_score[0]
        for k in range(1, TOP_K):
            total = total + sel_score[k]
        for k in range(TOP_K):
            idx_ref[k:k + 1, lanes] = sel_idx[k].astype(jnp.int32)
            wts_ref[k:k + 1, lanes] = sel_score[k] / total * ROUTED_SCALE
        picked.append((sel_idx, onehot))

    onehot_all = jnp.concatenate([oh for _, oh in picked], axis=1)
    prefix = jnp.dot(onehot_all.astype(BF16), upper_ref[...], preferred_element_type=F32)
    for c, (sel_idx, _) in enumerate(picked):
        lanes = slice(c * tc, (c + 1) * tc)
        pos = prefix[:, lanes] + base_scr[...]
        for k in range(TOP_K):
            rk = jnp.sum(jnp.where(eid == sel_idx[k], pos, 0.0), axis=0, keepdims=True)
            rank_ref[k:k + 1, lanes] = rk.astype(jnp.int32)
    base_scr[...] = base_scr[...] + jnp.broadcast_to(jnp.sum(onehot_all, axis=1, keepdims=True), base_scr.shape)
    cnt_ref[...] = base_scr[...]


def _router(x1, wrt, brt, tb):
    n, d_model = x1.shape
    n_exp = wrt.shape[0]
    upper = jnp.triu(jnp.ones((tb, tb), BF16), k=1)
    tok = lambda i: (0, i)
    return pl.pallas_call(
        _router_kernel,
        grid=(n // tb,),
        in_specs=[pl.BlockSpec((tb, d_model), lambda i: (i, 0)), _const_spec(wrt.shape), _const_spec(brt.shape),
                  _const_spec(upper.shape)],
        out_specs=(pl.BlockSpec((TOP_K, tb), tok), pl.BlockSpec((TOP_K, tb), tok), pl.BlockSpec((TOP_K, tb), tok),
                   pl.BlockSpec((n_exp, LANES), lambda i: (0, 0))),
        out_shape=(jax.ShapeDtypeStruct((TOP_K, n), jnp.int32), jax.ShapeDtypeStruct((TOP_K, n), F32),
                   jax.ShapeDtypeStruct((TOP_K, n), jnp.int32), jax.ShapeDtypeStruct((n_exp, LANES), F32)),
        scratch_shapes=[pltpu.VMEM((n_exp, LANES), F32)],
        compiler_params=_cparams(("arbitrary",)),
        name="router",
    )(x1, wrt, brt, upper)


def _pack_bf16_pairs(x):
    h = x.shape[1] // 2
    xb = x.astype(BF16).astype(F32)
    lo = lax.bitcast_convert_type(xb[:, :h], jnp.uint32) >> 16
    return lo | (lax.bitcast_convert_type(xb[:, h:], jnp.uint32) & jnp.uint32(0xFFFF0000))


def _unpack_lo(w):
    return lax.bitcast_convert_type(w << 16, F32)


def _unpack_hi(w):
    return lax.bitcast_convert_type(w & jnp.uint32(0xFFFF0000), F32)


def _dest_kernel(cstart_ref, idx_ref, rank_ref, dest_ref):
    idx = idx_ref[...]
    rank = rank_ref[...]

    def body(e, acc):
        return jnp.where(idx == e, cstart_ref[e] + rank, acc)

    dest_ref[...] = lax.fori_loop(0, cstart_ref.shape[0], body, jnp.zeros_like(rank), unroll=8)


def _dest(cstart, idx_t, rank_t, tb):
    n = idx_t.shape[1]
    tok = lambda i: (0, i)
    return pl.pallas_call(
        _dest_kernel,
        grid=(n // tb,),
        in_specs=[pl.BlockSpec(memory_space=pltpu.SMEM), pl.BlockSpec((TOP_K, tb), tok),
                  pl.BlockSpec((TOP_K, tb), tok)],
        out_specs=pl.BlockSpec((TOP_K, tb), tok),
        out_shape=jax.ShapeDtypeStruct((TOP_K, n), jnp.int32),
        compiler_params=_cparams(("arbitrary",)),
        name="slot_index",
    )(cstart, idx_t, rank_t)


def _dispatch_kernel(dest_ref, x_ref, xs_ref, buf, sems, *, n_steps):
    i = pl.program_id(0)
    tb = x_ref.shape[0]
    slot = lax.rem(i, 2)

    def drain(s):
        for k in range(TOP_K):
            pltpu.make_async_copy(buf.at[pl.ds(s * tb, tb)], xs_ref.at[pl.ds(0, tb)], sems.at[s]).wait()

    @pl.when(i >= 2)
    def _():
        drain(slot)

    words = _pack_bf16_pairs(x_ref[...])
    for j in range(buf.shape[1]):
        buf[pl.ds(slot * tb, tb), j, :] = words[:, j * LANES:(j + 1) * LANES]

    def issue(t, carry):
        for k in range(TOP_K):
            pltpu.make_async_copy(buf.at[pl.ds(slot * tb + t, 1)], xs_ref.at[pl.ds(dest_ref[k, t], 1)],
                                  sems.at[slot]).start(priority=k % 2)
        return carry

    lax.fori_loop(0, tb, issue, 0, unroll=2)

    @pl.when(i == n_steps - 1)
    def _():
        drain(slot)
        if n_steps >= 2:
            drain(1 - slot)


def _dispatch(dest, x1, n_slots, tb):
    n, d_model = x1.shape
    row_tiles = d_model // 2 // LANES
    return pl.pallas_call(
        functools.partial(_dispatch_kernel, n_steps=n // tb),
        grid=(n // tb,),
        in_specs=[pl.BlockSpec((TOP_K, tb), lambda i: (0, i), memory_space=pltpu.SMEM),
                  pl.BlockSpec((tb, d_model), lambda i: (i, 0))],
        out_specs=pl.BlockSpec(memory_space=pl.ANY),
        out_shape=jax.ShapeDtypeStruct((n_slots, row_tiles, LANES), jnp.uint32),
        scratch_shapes=[pltpu.VMEM((2 * tb, row_tiles, LANES), jnp.uint32), pltpu.SemaphoreType.DMA((2,))],
        compiler_params=_cparams(("arbitrary",)),
        name="dispatch",
    )(dest, x1)


def _expert_kernel(wt_ref, we_ref, wn_ref, ws_ref, lo_ref, hi_ref, xs_ref, wg_hbm, wu_hbm, wd_hbm, os_ref,
                   wgu_scr, wd_scr, stage_g, stage_u, stage_d, wsem, *, bm, layer):
    w = pl.program_id(0)
    row_tiles = xs_ref.shape[0] // bm
    f = stage_g.shape[2]
    prev = jnp.maximum(w - 1, 0)
    lo = lo_ref[w]
    hi = hi_ref[w]
    live = hi > lo
    expert = we_ref[w]
    slot = ws_ref[w]
    new_expert = jnp.logical_or(w == 0, expert != we_ref[prev])
    new_tile = jnp.logical_or(w == 0, wt_ref[w] != wt_ref[prev])

    def weight_copies(e, s):
        return [pltpu.make_async_copy(wg_hbm.at[layer, e], stage_g.at[s], wsem.at[s]),
                pltpu.make_async_copy(wu_hbm.at[layer, e], stage_u.at[s], wsem.at[s]),
                pltpu.make_async_copy(wd_hbm.at[layer, e], stage_d.at[s], wsem.at[s])]

    @pl.when(jnp.logical_and(live, new_expert))
    def _():
        @pl.when(w == 0)
        def _():
            for cp in weight_copies(expert, slot):
                cp.start()

        for cp in weight_copies(expert, slot):
            cp.wait()
        wgu_scr[:, :f] = stage_g[slot].astype(BF16)
        wgu_scr[:, f:] = stage_u[slot].astype(BF16)
        wd_scr[...] = stage_d[slot].astype(BF16)
        nxt = wn_ref[w]

        @pl.when(nxt != expert)
        def _():
            for cp in weight_copies(nxt, 1 - slot):
                cp.start()

    @pl.when(live)
    def _():
        words = [xs_ref[pl.ds(j, bm, stride=row_tiles), :] for j in range(row_tiles)]
        x = jnp.concatenate([_unpack_lo(word).astype(BF16) for word in words]
                            + [_unpack_hi(word).astype(BF16) for word in words], axis=1)
        gu = jnp.dot(x, wgu_scr[...], preferred_element_type=F32)
        h = jax.nn.silu(gu[:, :f]) * gu[:, f:]
        out = _pack_bf16_pairs(jnp.dot(h.astype(BF16), wd_scr[...], preferred_element_type=F32))
        out_tiles = os_ref.shape[0] // bm

        @pl.when(new_tile)
        def _():
            for j in range(out_tiles):
                os_ref[pl.ds(j, bm, stride=out_tiles), :] = out[:, j * LANES:(j + 1) * LANES]

        @pl.when(jnp.logical_not(new_tile))
        def _():
            row = lax.broadcasted_iota(jnp.int32, (bm, LANES), 0)
            mine = jnp.logical_and(row >= lo, row < hi)
            for j in range(out_tiles):
                rows_j = pl.ds(j, bm, stride=out_tiles)
                os_ref[rows_j, :] = jnp.where(mine, out[:, j * LANES:(j + 1) * LANES], os_ref[rows_j, :])


def _experts(wt, we, wn, ws, lo, hi, xs, layer, w_gate, w_up, w_down, bm):
    n_slots, row_tiles, _ = xs.shape
    d_model, f = w_gate.shape[2], w_gate.shape[3]
    xs2d = xs.reshape(n_slots * row_tiles, LANES)
    row_map = lambda w, wt, *others: (wt[w], 0)
    hbm = pl.BlockSpec(memory_space=pl.ANY)
    return pl.pallas_call(
        functools.partial(_expert_kernel, bm=bm, layer=layer),
        grid_spec=pltpu.PrefetchScalarGridSpec(
            num_scalar_prefetch=6,
            grid=(wt.shape[0],),
            in_specs=[pl.BlockSpec((bm * row_tiles, LANES), row_map), hbm, hbm, hbm],
            out_specs=pl.BlockSpec((bm * row_tiles, LANES), row_map),
            scratch_shapes=[pltpu.VMEM((d_model, 2 * f), BF16), pltpu.VMEM((f, d_model), BF16),
                            pltpu.VMEM((2, d_model, f), F32), pltpu.VMEM((2, d_model, f), F32),
                            pltpu.VMEM((2, f, d_model), F32), pltpu.SemaphoreType.DMA((2,))],
        ),
        out_shape=jax.ShapeDtypeStruct((n_slots * row_tiles, LANES), jnp.uint32),
        compiler_params=_cparams(("arbitrary",)),
        name="experts",
    )(wt, we, wn, ws, lo, hi, xs2d, w_gate, w_up, w_down).reshape(n_slots, row_tiles, LANES)


def _work_list(counts, bm, n_tiles, n_work):
    n_exp = counts.shape[0]
    cend = jnp.cumsum(counts)
    cstart = cend - counts
    has = counts > 0
    first_tile = cstart // bm
    n_items = jnp.where(has, (cend - 1) // bm - first_tile + 1, 0)
    wend = jnp.cumsum(n_items)
    woff = wend - n_items
    total = wend[-1]
    w = jnp.arange(n_work, dtype=jnp.int32)
    valid = w < total
    w_eff = jnp.minimum(w, total - 1)
    we = jnp.minimum(jnp.sum((wend[None, :] <= w_eff[:, None]).astype(jnp.int32), axis=1), n_exp - 1)
    onehot = we[:, None] == jnp.arange(n_exp, dtype=jnp.int32)[None, :]
    pick = lambda v: jnp.sum(jnp.where(onehot, v[None, :], 0), axis=1)
    wt = pick(first_tile) + (w_eff - pick(woff))
    lo = jnp.where(valid, jnp.maximum(pick(cstart) - wt * bm, 0), 0)
    hi = jnp.where(valid, jnp.minimum(pick(cend) - wt * bm, bm), 0)
    eids = jnp.arange(n_exp, dtype=jnp.int32)
    later = jnp.logical_and(has[None, :], eids[None, :] > eids[:, None])
    next_present = jnp.min(jnp.where(later, eids[None, :], n_exp), axis=1)
    next_present = jnp.where(next_present == n_exp, eids, next_present)
    ws = (pick(jnp.cumsum(has.astype(jnp.int32))) - 1) % 2
    wn = pick(next_present)
    i32 = lambda v: v.astype(jnp.int32)
    return i32(cstart), i32(wt), i32(we), i32(wn), i32(ws), i32(lo), i32(hi)


def _rows_view(ref):
    rows, rt, lanes = ref.shape
    return ref.reshape(rows * rt, lanes)


def _combine_kernel(dest0_ref, destn_ref, x_ref, wts_ref, os_ref, wsgu_ref, wsd_ref, lng_ref, lnb_ref, *rest,
                    alpha, n_prompt_blocks):
    final = n_prompt_blocks is not None
    n_out = 2 if final else 1
    outs, gbufs, sems = rest[:n_out], rest[n_out:n_out + TOP_K], rest[n_out + TOP_K]
    tb, d_model = x_ref.shape
    f = wsd_ref.shape[0]
    rt = gbufs[0].shape[1]
    i = pl.program_id(0)
    slot = lax.rem(i, 2)
    nslot = 1 - slot

    def row_copy(dref, k, t, s):
        return pltpu.make_async_copy(os_ref.at[pl.ds(dref[k, t], 1)], gbufs[k].at[pl.ds(s * tb + t, 1)], sems.at[s])

    def slot_copy(k, s):
        return pltpu.make_async_copy(os_ref.at[pl.ds(0, tb)], gbufs[k].at[pl.ds(s * tb, tb)], sems.at[s])

    @pl.when(i == 0)
    def _():
        def issue(t, carry):
            for k in range(TOP_K):
                row_copy(dest0_ref, k, t, 0).start(priority=k % 2)
            return carry

        lax.fori_loop(0, tb, issue, 0, unroll=2)

    for k in range(TOP_K):
        slot_copy(k, slot).wait()

    views = [_rows_view(g) for g in gbufs]
    w = wts_ref[...]
    rb_rows = 4 * SUBLANES
    units = [(k, j) for k in range(TOP_K) for j in range(rt)]
    view_base = slot * (tb * rt)
    blocks = []
    for rb in range(tb // rb_rows):
        acc_lo = [None] * rt
        acc_hi = [None] * rt
        wk = None
        for tt in range(rb_rows):
            t = rb * rb_rows + tt
            for k in range(TOP_K):
                row_copy(destn_ref, k, t, nslot).start(priority=k % 2)
            for k, j in units[tt * len(units) // rb_rows:(tt + 1) * len(units) // rb_rows]:
                if j == 0:
                    wk = jnp.broadcast_to(w[rb * rb_rows:(rb + 1) * rb_rows, k:k + 1], (rb_rows, LANES))
                word = views[k][pl.ds(view_base + rb * rb_rows * rt + j, rb_rows, stride=rt), :]
                lo = _unpack_lo(word) * wk
                hi = _unpack_hi(word) * wk
                acc_lo[j] = lo if k == 0 else acc_lo[j] + lo
                acc_hi[j] = hi if k == 0 else acc_hi[j] + hi
        blocks.append(jnp.concatenate(acc_lo + acc_hi, axis=1))
    routed = jnp.concatenate(blocks, axis=0)

    @pl.when(i == pl.num_programs(0) - 1)
    def _():
        for k in range(TOP_K):
            slot_copy(k, nslot).wait()

    x = x_ref[...]
    gu = jnp.dot(x.astype(BF16), wsgu_ref[...], preferred_element_type=F32)
    h = jax.nn.silu(gu[:, :f]) * gu[:, f:]
    shared = jnp.dot(h.astype(BF16), wsd_ref[...], preferred_element_type=F32)
    res = alpha * x + (routed + shared)
    mu = jnp.mean(res, axis=-1, keepdims=True)
    xc = res - mu
    var = jnp.mean(xc * xc, axis=-1, keepdims=True)
    y = xc * lax.rsqrt(var + LN_EPS) * lng_ref[...] + lnb_ref[...]
    if not final:
        outs[0][...] = y
        return

    slab = rest[n_out + TOP_K + 1]
    tq = tb // SEQS_PER_GROUP
    for j in range(d_model // LANES):
        slab[j] = y[:, j * LANES:(j + 1) * LANES]

    def write(o_ref):
        for b in range(SEQS_PER_GROUP):
            for j in range(d_model // LANES):
                o_ref[b, :, j * LANES:(j + 1) * LANES] = slab[j, pl.ds(b, tq, stride=SEQS_PER_GROUP), :]

    i = pl.program_id(0)

    @pl.when(i < n_prompt_blocks)
    def _():
        write(outs[0])

    @pl.when(i >= n_prompt_blocks)
    def _():
        write(outs[1])


def _combine(dest, x1, wts_nk, os_, wsgu, wsd, lng, lnb, alpha, tb, final_shapes=None):
    n, d_model = x1.shape
    rt = os_.shape[1]
    scratch = [pltpu.VMEM((2 * tb, rt, LANES), jnp.uint32) for _ in range(TOP_K)] + [pltpu.SemaphoreType.DMA((2,))]
    last_block = n // tb - 1
    if final_shapes is None:
        n_prompt_blocks = None
        out_specs = pl.BlockSpec((tb, d_model), lambda i: (i, 0))
        out_shape = jax.ShapeDtypeStruct((n, d_model), F32)
    else:
        batch, seq, dec_batch, dec_seq = final_shapes
        tq = tb // SEQS_PER_GROUP
        assert dec_seq == tq and seq % tq == 0
        wpg = seq // tq
        n_prompt_blocks = batch * seq // tb
        last = n_prompt_blocks - 1
        out_specs = (
            pl.BlockSpec((None, SEQS_PER_GROUP, tq, d_model),
                         lambda i: (jnp.minimum(i, last) // wpg, 0, jnp.minimum(i, last) % wpg, 0)),
            pl.BlockSpec((None, SEQS_PER_GROUP, tq, d_model), lambda i: (jnp.maximum(i - n_prompt_blocks, 0), 0, 0, 0)),
        )
        out_shape = (jax.ShapeDtypeStruct((batch // SEQS_PER_GROUP, SEQS_PER_GROUP, seq, d_model), F32),
                     jax.ShapeDtypeStruct((dec_batch // SEQS_PER_GROUP, SEQS_PER_GROUP, dec_seq, d_model), F32))
        scratch.append(pltpu.VMEM((d_model // LANES, tb, LANES), F32))
    return pl.pallas_call(
        functools.partial(_combine_kernel, alpha=alpha, n_prompt_blocks=n_prompt_blocks),
        grid=(n // tb,),
        in_specs=[pl.BlockSpec((TOP_K, tb), lambda i: (0, 0), memory_space=pltpu.SMEM),
                  pl.BlockSpec((TOP_K, tb), lambda i: (0, jnp.minimum(i + 1, last_block)), memory_space=pltpu.SMEM),
                  pl.BlockSpec((tb, d_model), lambda i: (i, 0)),
                  pl.BlockSpec((tb, TOP_K), lambda i: (i, 0)),
                  pl.BlockSpec(memory_space=pl.ANY),
                  _const_spec(wsgu.shape), _const_spec(wsd.shape), _const_spec(lng.shape), _const_spec(lnb.shape)],
        out_specs=out_specs,
        out_shape=out_shape,
        scratch_shapes=scratch,
        compiler_params=_cparams(("arbitrary",)),
        name="combine" if final_shapes is None else "combine_out",
    )(dest, dest, x1, wts_nk, os_, wsgu, wsd, lng, lnb)


def _to_time_major(x):
    b, t, d = x.shape
    return x.reshape(b // SEQS_PER_GROUP, SEQS_PER_GROUP, t, d).transpose(0, 2, 1, 3).reshape(b * t, d)


def _from_time_major(x, b, t):
    d = x.shape[-1]
    return x.reshape(b // SEQS_PER_GROUP, t, SEQS_PER_GROUP, d).transpose(0, 2, 1, 3).reshape(b, t, d)


def _block_diag_in(bbt, n_gb):
    g, c, p = bbt.shape
    gl = g // n_gb
    eye = jnp.eye(gl, dtype=bbt.dtype)
    return jnp.einsum("jgcp,gh->jgchp", bbt.reshape(n_gb, gl, c, p), eye).reshape(n_gb, gl * c, gl * p)


def _block_diag_out(cm, n_gb):
    g, c, p = cm.shape
    gl = g // n_gb
    eye = jnp.eye(gl, dtype=cm.dtype)
    return jnp.einsum("jgcp,gh->jhpgc", cm.reshape(n_gb, gl, c, p), eye).reshape(n_gb, gl * p, gl * c)


def kernel(x_prompt, x_sample, cache_k, cache_v, state_ssm_re, state_ssm_im, rel_bias, w_in, attn_sinks, w_attn_proj, lam_re, lam_im, log_dt, b_re, b_im, c_re, c_im, d_skip, w_glu, w_out, ln1_g, ln1_b, w_router, b_router, w_gate, w_up, w_down, ws_gate, ws_up, ws_down, ln2_g, ln2_b):
    batch, seq, d_model = x_prompt.shape
    dec_batch, dec_seq, _ = x_sample.shape
    depth = w_in.shape[0]
    win = cache_k.shape[2]
    n_q_heads = attn_sinks.shape[1]
    q_w = n_q_heads * HEAD_DIM
    kv_w = N_KV_HEADS * HEAD_DIM
    d_ssm = d_skip.shape[1]
    n_groups_ssm, state_dim = lam_re.shape[1], lam_re.shape[2]
    gp = n_groups_ssm * state_dim
    n_exp = w_router.shape[2]
    alpha = (2 * depth) ** 0.25
    assert win == WINDOW and seq % CHUNK == 0 and dec_seq <= CHUNK
    assert batch % SEQS_PER_GROUP == 0 and dec_batch % SEQS_PER_GROUP == 0
    qch = (PAST_LEN + np.arange(dec_seq)) // CHUNK
    kch = np.concatenate([PAST_LEN - win + np.arange(win), PAST_LEN + np.arange(dec_seq)]) // CHUNK
    assert np.all((kch[None, :] >= qch[:, None] - WINDOW // CHUNK) & (kch[None, :] <= qch[:, None]))

    n_p = batch * seq
    n_s = dec_batch * dec_seq
    n_tok = n_p + n_s
    n_gb = n_groups_ssm // SSM_GROUPS_PER_BATCH

    bias = _rel_bias_table(rel_bias)
    a_re, a_im, bbt_re, bbt_im = _discretize(lam_re, lam_im, log_dt, b_re, b_im)

    x_all = jnp.concatenate([_to_time_major(x_prompt.astype(F32)), _to_time_major(x_sample.astype(F32))], axis=0)
    zeros_kv = jnp.zeros((batch, win, kv_w), F32)
    zeros_h = jnp.zeros((batch, 2 * gp), F32)

    tb_route = 512 if n_tok % 512 == 0 else 256
    tb_move = 256
    bm = 512
    n_slots = n_tok * TOP_K
    assert n_slots % bm == 0
    n_tiles = n_slots // bm
    n_work = n_tiles + n_exp - 1

    outs = {k: [] for k in ("kp", "vp", "hp", "ks", "vs", "hs")}
    for l in range(depth):
        wl = w_in[l].astype(BF16)
        lw = dict(
            wqkv=wl[:, :q_w + 2 * kv_w], wu=wl[:, q_w + 2 * kv_w:q_w + 2 * kv_w + d_ssm],
            wgl=wl[:, q_w + 2 * kv_w + d_ssm:], wap=w_attn_proj[l].astype(BF16),
            a=jnp.stack([a_re[l].reshape(gp), a_im[l].reshape(gp)]),
            wb=jnp.concatenate([_block_diag_in(bbt_re[l], n_gb), _block_diag_in(bbt_im[l], n_gb)], axis=2).astype(BF16),
            wc=jnp.concatenate([_block_diag_out(c_re[l].astype(F32), n_gb),
                                -_block_diag_out(c_im[l].astype(F32), n_gb)], axis=1).astype(BF16),
            dskip=d_skip[l].astype(F32).reshape(1, d_ssm), wglu=w_glu[l].astype(BF16), wout=w_out[l].astype(BF16),
            ln1g=ln1_g[l].astype(F32).reshape(1, d_model), ln1b=ln1_b[l].astype(F32).reshape(1, d_model),
        )
        x1, kp, vp, hp = _mixer(x_all, 0, batch // SEQS_PER_GROUP, seq // CHUNK, CHUNK, zeros_kv, zeros_kv, zeros_h,
                                lw, bias, attn_sinks[l], alpha, True)
        hinit = jnp.concatenate([state_ssm_re[l].astype(F32).reshape(dec_batch, gp),
                                 state_ssm_im[l].astype(F32).reshape(dec_batch, gp)], axis=1)
        x1, ks, vs, hs = _mixer(x1, n_p // (SEQS_PER_GROUP * dec_seq), dec_batch // SEQS_PER_GROUP, 1, dec_seq,
                                cache_k[l].astype(F32).reshape(dec_batch, win, kv_w),
                                cache_v[l].astype(F32).reshape(dec_batch, win, kv_w), hinit,
                                lw, bias, attn_sinks[l], alpha, False)
        for name, val in zip(("kp", "vp", "hp", "ks", "vs", "hs"), (kp, vp, hp, ks, vs, hs)):
            outs[name].append(val)

        wrt = w_router[l].astype(BF16).T
        brt = jnp.broadcast_to(b_router[l].astype(F32)[:, None], (n_exp, LANES))
        idx_t, wts_t, rank_t, cnt = _router(x1, wrt, brt, tb_route)
        cstart, wt, we, wn, wslot, lo, hi = _work_list(cnt[:, 0].astype(jnp.int32), bm, n_tiles, n_work)
        dest = _dest(cstart, idx_t, rank_t, tb_route)
        xs = _dispatch(dest, x1, n_slots, tb_route)
        os_ = _experts(wt, we, wn, wslot, lo, hi, xs, l, w_gate.astype(F32), w_up.astype(F32), w_down.astype(F32), bm)
        wsgu = jnp.concatenate([ws_gate[l], ws_up[l]], axis=1).astype(BF16)
        x_all = _combine(dest, x1, wts_t.T, os_, wsgu, ws_down[l].astype(BF16),
                         ln2_g[l].astype(F32).reshape(1, d_model), ln2_b[l].astype(F32).reshape(1, d_model),
                         alpha, tb_move, final_shapes=(batch, seq, dec_batch, dec_seq) if l == depth - 1 else None)

    y_prompt = x_all[0].reshape(batch, seq, d_model)
    y_sample = x_all[1].reshape(dec_batch, dec_seq, d_model)

    def kv(vals, b):
        return jnp.stack(vals).reshape(depth, b, win, N_KV_HEADS, HEAD_DIM)

    def st(vals, b, part):
        return jnp.stack(vals)[:, :, part * gp:(part + 1) * gp].reshape(depth, b, n_groups_ssm, state_dim)

    return (y_prompt, y_sample, kv(outs["kp"], batch), kv(outs["vp"], batch), st(outs["hp"], batch, 0),
            st(outs["hp"], batch, 1), kv(outs["ks"], dec_batch), kv(outs["vs"], dec_batch),
            st(outs["hs"], dec_batch, 0), st(outs["hs"], dec_batch, 1))
```
